```python
import math
import jax, jax.numpy as jnp
from jax import lax
import numpy as np

D_MODEL = 1024
BATCH = 4
SEQ = 8192
DEPTH = 2

N_A = DEPTH // 2
N_B = DEPTH - N_A
N_DENSE = (DEPTH + 1) // 2
N_MOE = DEPTH // 2

MEM_LEN = 256
HEAD_DIM = 64
MEM_HEADS = 4
MEM_DIM = MEM_HEADS * HEAD_DIM
MIX_DIM = D_MODEL - MEM_DIM

GLA_HEADS = 4
GLA_DV = MIX_DIM // GLA_HEADS
GLA_DK = GLA_DV // 2
GLA_KDIM = GLA_HEADS * GLA_DK
GLA_GATE_RANK = 16
GLA_GATE_NORM = 16.0
GLA_CHUNK = 64

DIFF_HEADS = MIX_DIM // (2 * HEAD_DIM)
DIFF_VDIM = 2 * HEAD_DIM
Q_BLOCK = 128
ROPE_THETA = 10000.0

FFN_DENSE = 2816
N_EXPERTS = 8
TOP_K = 2
FFN_EXPERT = 3584
MOE_BLOCK = 256
EPS = 1e-6

A_SPLITS = (GLA_KDIM, 2 * GLA_KDIM, 2 * GLA_KDIM + MIX_DIM, 2 * GLA_KDIM + 2 * MIX_DIM,
            2 * GLA_KDIM + 2 * MIX_DIM + GLA_GATE_RANK)
W_IN_A = 2 * GLA_KDIM + 2 * MIX_DIM + GLA_GATE_RANK + MEM_DIM
W_IN_B = MIX_DIM + MEM_DIM

kernel_name = "yoco_gla_diffattn_moe_memory"


def rmsnorm(x, g):
    xf = x.astype(jnp.float32)
    y = xf * lax.rsqrt(jnp.mean(xf * xf, axis=-1, keepdims=True) + EPS)
    return (y * g.astype(jnp.float32)).astype(x.dtype)


def rope(x, pos):
    half = x.shape[-1] // 2
    inv = ROPE_THETA ** (-jnp.arange(half, dtype=jnp.float32) / half)
    ang = pos.astype(jnp.float32)[:, None] * inv[None, :]
    cos = jnp.cos(ang)[None, :, None, :]
    sin = jnp.sin(ang)[None, :, None, :]
    x1 = x[..., :half].astype(jnp.float32)
    x2 = x[..., half:].astype(jnp.float32)
    out = jnp.concatenate([x1 * cos - x2 * sin, x1 * sin + x2 * cos], axis=-1)
    return out.astype(x.dtype)


def gla_chunked(q, k, v, gk):
    B, S, H, dk = q.shape
    dv = v.shape[-1]
    N = S // GLA_CHUNK

    def chunks(t):
        return t.astype(jnp.float32).reshape(B, N, GLA_CHUNK, H, t.shape[-1]).transpose(0, 3, 1, 2, 4)

    q = chunks(q) * GLA_DK ** -0.5
    k = chunks(k)
    v = chunks(v)
    b = jnp.cumsum(chunks(gk), axis=3)
    b_last = b[:, :, :, -1:, :]
    b_ref = b[:, :, :, GLA_CHUNK // 2 - 1:GLA_CHUNK // 2, :]
    mask = jnp.tril(jnp.ones((GLA_CHUNK, GLA_CHUNK), dtype=bool))
    scores = jnp.einsum('bhnid,bhnjd->bhnij', q * jnp.exp(b - b_ref), k * jnp.exp(b_ref - b))
    scores = jnp.where(mask, scores, 0.0)
    o_intra = jnp.einsum('bhnij,bhnjv->bhniv', scores, v)
    kv = jnp.einsum('bhncd,bhncv->nbhdv', k * jnp.exp(b_last - b), v)
    decay = jnp.exp(b_last[:, :, :, 0, :]).transpose(2, 0, 1, 3)

    def step(state, inp):
        kv_n, dec_n = inp
        return dec_n[..., None] * state + kv_n, state

    _, states = lax.scan(step, jnp.zeros((B, H, dk, dv), jnp.float32), (kv, decay))
    o_inter = jnp.einsum('bhnid,nbhdv->bhniv', q * jnp.exp(b), states)
    o = o_intra + o_inter
    return o.transpose(0, 2, 3, 1, 4).reshape(B, S, H, dv)


def memory_attention(q_mem, mem, w_mem_kv):
    B, S = q_mem.shape[:2]
    kv = mem.astype(q_mem.dtype) @ w_mem_kv
    M = kv.shape[1]
    k = kv[..., :MEM_DIM].reshape(B, M, MEM_HEADS, HEAD_DIM)
    v = kv[..., MEM_DIM:].reshape(B, M, MEM_HEADS, HEAD_DIM)
    s = jnp.einsum('bshd,bmhd->bhsm', q_mem, k).astype(jnp.float32) * HEAD_DIM ** -0.5
    p = jax.nn.softmax(s, axis=-1).astype(v.dtype)
    return jnp.einsum('bhsm,bmhd->bshd', p, v).reshape(B, S, MEM_DIM)


def diff_attention(q, k, v, lam, lambda_init, g_subln):
    B, S, H2, d = q.shape
    H = H2 // 2
    n_blk = S // Q_BLOCK
    qb = q.reshape(B, n_blk, Q_BLOCK, H2, d).transpose(1, 0, 2, 3, 4)
    kpos = jnp.arange(S)

    def block(args):
        qi, i = args
        s = jnp.einsum('bqhd,bkhd->bhqk', qi, k).astype(jnp.float32) * d ** -0.5
        qpos = i * Q_BLOCK + jnp.arange(Q_BLOCK)
        s = jnp.where(kpos[None, :] <= qpos[:, None], s, -jnp.inf)
        p = jax.nn.softmax(s, axis=-1).reshape(B, H, 2, Q_BLOCK, S)
        a = p[:, :, 0] - lam * p[:, :, 1]
        return jnp.einsum('bhqk,bkhe->bqhe', a.astype(v.dtype), v)

    o = lax.map(block, (qb, jnp.arange(n_blk)))
    o = o.transpose(1, 0, 2, 3, 4).reshape(B, S, H, 2 * d)
    o = rmsnorm(o, g_subln) * (1.0 - lambda_init)
    return o.reshape(B, S, H * 2 * d)


def gla_group(h, mem, w_in, w_gk_up, b_gk, g_gla, w_mem_kv):
    B, S, _ = h.shape
    q, k, v, g_out, gk_low, q_mem = jnp.split(h @ w_in, A_SPLITS, axis=-1)
    gk = jax.nn.log_sigmoid((gk_low @ w_gk_up + b_gk).astype(jnp.float32)) / GLA_GATE_NORM
    o = gla_chunked(q.reshape(B, S, GLA_HEADS, GLA_DK), k.reshape(B, S, GLA_HEADS, GLA_DK),
                    v.reshape(B, S, GLA_HEADS, GLA_DV), gk.reshape(B, S, GLA_HEADS, GLA_DK))
    o = rmsnorm(o.astype(h.dtype), g_gla) * jax.nn.silu(g_out.reshape(B, S, GLA_HEADS, GLA_DV))
    m = memory_attention(q_mem.reshape(B, S, MEM_HEADS, HEAD_DIM), mem, w_mem_kv)
    return jnp.concatenate([o.reshape(B, S, MIX_DIM), m], axis=-1)


def diff_group(h, mem, k_shared, v_shared, w_in, lq1, lk1, lq2, lk2, g_subln, w_mem_kv,
               lambda_init, pos):
    B, S, _ = h.shape
    proj = h @ w_in
    q = rope(proj[..., :MIX_DIM].reshape(B, S, 2 * DIFF_HEADS, HEAD_DIM), pos)
    q_mem = proj[..., MIX_DIM:].reshape(B, S, MEM_HEADS, HEAD_DIM)
    lam = (jnp.exp(jnp.sum(lq1.astype(jnp.float32) * lk1.astype(jnp.float32)))
           - jnp.exp(jnp.sum(lq2.astype(jnp.float32) * lk2.astype(jnp.float32))) + lambda_init)
    o = diff_attention(q, k_shared, v_shared, lam, lambda_init, g_subln)
    m = memory_attention(q_mem, mem, w_mem_kv)
    return jnp.concatenate([o, m], axis=-1)


def swiglu(h, w_gate, w_up, w_down):
    return (jax.nn.silu(h @ w_gate) * (h @ w_up)) @ w_down


def moe_swiglu(h, w_router, w_gate, w_up, w_down):
    B, S, D = h.shape
    T = B * S
    A = T * TOP_K
    x = h.reshape(T, D)
    logits = (x @ w_router).astype(jnp.float32)
    top_logit, top_idx = lax.top_k(logits, TOP_K)
    gates = jax.nn.softmax(top_logit, axis=-1)
    expert = top_idx.reshape(A)
    token = jnp.repeat(jnp.arange(T, dtype=jnp.int32), TOP_K)
    gate = gates.reshape(A).astype(h.dtype)
    order = jnp.argsort(expert)
    e_sorted, t_sorted, g_sorted = expert[order], token[order], gate[order]
    counts = jnp.bincount(expert, length=N_EXPERTS)
    padded = (counts + MOE_BLOCK - 1) // MOE_BLOCK * MOE_BLOCK
    start = jnp.cumsum(counts) - counts
    pend = jnp.cumsum(padded)
    pstart = pend - padded
    dest = pstart[e_sorted] + jnp.arange(A, dtype=jnp.int32) - start[e_sorted]
    n_blocks = (A + MOE_BLOCK - 1) // MOE_BLOCK + N_EXPERTS
    L = n_blocks * MOE_BLOCK
    rows_token = jnp.zeros((L,), jnp.int32).at[dest].set(t_sorted)
    rows_gate = jnp.zeros((L,), h.dtype).at[dest].set(g_sorted)
    xs = x[rows_token].reshape(n_blocks, MOE_BLOCK, D)
    block_expert = jnp.clip(jnp.searchsorted(pend, jnp.arange(n_blocks) * MOE_BLOCK, side='right'),
                            0, N_EXPERTS - 1)

    def run(args):
        xb, e = args
        return swiglu(xb, w_gate[e], w_up[e], w_down[e])

    ys = lax.map(run, (xs, block_expert)).reshape(L, D)
    out = jnp.zeros((T, D), h.dtype).at[rows_token].add(ys * rows_gate[:, None])
    return out.reshape(B, S, D)


def setup_inputs(seed: int = 0) -> dict:
    key = jax.random.key(seed)
    ks = list(jax.random.split(key, 32))
    it = iter(ks)

    def normal(shape, scale):
        return jax.random.normal(next(it), shape, jnp.float32) * scale

    def w(shape, fan_in):
        return normal(shape, fan_in ** -0.5)

    def gain(shape):
        return 1.0 + normal(shape, 0.02)

    return {
        "x": normal((BATCH, SEQ, D_MODEL), 1.0),
        "mem": normal((BATCH, MEM_LEN, D_MODEL), 1.0),
        "g_mix": gain((DEPTH, D_MODEL)),
        "g_ffn": gain((DEPTH, D_MODEL)),
        "w_mem_kv": w((DEPTH, D_MODEL, 2 * MEM_DIM), D_MODEL),
        "w_out": w((DEPTH, D_MODEL, D_MODEL), D_MODEL),
        "w_in_a": w((N_A, D_MODEL, W_IN_A), D_MODEL),
        "w_gk_up": w((N_A, GLA_GATE_RANK, GLA_KDIM), GLA_GATE_RANK),
        "b_gk": normal((N_A, GLA_KDIM), 0.01),
        "g_gla": gain((N_A, GLA_DV)),
        "g_kv": gain((D_MODEL,)),
        "w_kv": w((D_MODEL, 2 * MIX_DIM), D_MODEL),
        "w_in_b": w((N_B, D_MODEL, W_IN_B), D_MODEL),
        "lambda_q1": normal((N_B, HEAD_DIM), 0.1),
        "lambda_k1": normal((N_B, HEAD_DIM), 0.1),
        "lambda_q2": normal((N_B, HEAD_DIM), 0.1),
        "lambda_k2": normal((N_B, HEAD_DIM), 0.1),
        "g_subln": gain((N_B, DIFF_VDIM)),
        "w_dense_gate": w((N_DENSE, D_MODEL, FFN_DENSE), D_MODEL),
        "w_dense_up": w((N_DENSE, D_MODEL, FFN_DENSE), D_MODEL),
        "w_dense_down": w((N_DENSE, FFN_DENSE, D_MODEL), FFN_DENSE),
        "w_router": w((N_MOE, D_MODEL, N_EXPERTS), D_MODEL),
        "w_exp_gate": w((N_MOE, N_EXPERTS, D_MODEL, FFN_EXPERT), D_MODEL),
        "w_exp_up": w((N_MOE, N_EXPERTS, D_MODEL, FFN_EXPERT), D_MODEL),
        "w_exp_down": w((N_MOE, N_EXPERTS, FFN_EXPERT, D_MODEL), FFN_EXPERT),
        "g_final": gain((D_MODEL,)),
    }


def reference(x, mem, g_mix, g_ffn, w_mem_kv, w_out, w_in_a, w_gk_up, b_gk, g_gla, g_kv, w_kv,
              w_in_b, lambda_q1, lambda_k1, lambda_q2, lambda_k2, g_subln, w_dense_gate,
              w_dense_up, w_dense_down, w_router, w_exp_gate, w_exp_up, w_exp_down, g_final):
    B, S, _ = x.shape
    pos = jnp.arange(S)
    k_shared = None
    v_shared = None
    for l in range(DEPTH):
        h = rmsnorm(x, g_mix[l])
        if l < N_A:
            mixed = gla_group(h, mem, w_in_a[l], w_gk_up[l], b_gk[l], g_gla[l], w_mem_kv[l])
        else:
            j = l - N_A
            lambda_init = 0.8 - 0.6 * math.exp(-0.3 * l)
            mixed = diff_group(h, mem, k_shared, v_shared, w_in_b[j], lambda_q1[j], lambda_k1[j],
                               lambda_q2[j], lambda_k2[j], g_subln[j], w_mem_kv[l], lambda_init, pos)
        x = x + mixed @ w_out[l]
        h = rmsnorm(x, g_ffn[l])
        if l % 2 == 0:
            i = l // 2
            x = x + swiglu(h, w_dense_gate[i], w_dense_up[i], w_dense_down[i])
        else:
            i = l // 2
            x = x + moe_swiglu(h, w_router[i], w_exp_gate[i], w_exp_up[i], w_exp_down[i])
        if l == N_A - 1:
            kv = rmsnorm(x, g_kv) @ w_kv
            k_shared = rope(kv[..., :MIX_DIM].reshape(B, S, 2 * DIFF_HEADS, HEAD_DIM), pos)
            v_shared = kv[..., MIX_DIM:].reshape(B, S, DIFF_HEADS, DIFF_VDIM)
    return rmsnorm(x, g_final)
```

```python
import functools
import math

import jax
import jax.numpy as jnp
from jax import lax
from jax.experimental import pallas as pl
from jax.experimental.pallas import tpu as pltpu

D_MODEL = 1024
MEM_LEN = 256
HEAD_DIM = 64
MEM_HEADS = 4
MEM_DIM = MEM_HEADS * HEAD_DIM
MIX_DIM = D_MODEL - MEM_DIM
GLA_HEADS = 4
GLA_DV = MIX_DIM // GLA_HEADS
GLA_DK = GLA_DV // 2
GLA_KDIM = GLA_HEADS * GLA_DK
GLA_GATE_RANK = 16
GLA_GATE_NORM = 16.0
GLA_CHUNK = 64
DIFF_HEADS = MIX_DIM // (2 * HEAD_DIM)
DIFF_VDIM = 2 * HEAD_DIM
ROPE_THETA = 10000.0
FFN_DENSE = 2816
N_EXPERTS = 8
FFN_EXPERT = 3584
EPS = 1e-6
LAMBDA_INIT_L1 = 0.8 - 0.6 * math.exp(-0.3 * 1)

LANES = 128
SUBLANES = 8
VMEM_LIMIT_BYTES = 52 * 1024 * 1024

GLA_DK_PAD = LANES
GLA_DV_PAD = 2 * LANES

F32 = jnp.float32
BF16 = jnp.bfloat16
_NT = (((1,), (1,)), ((), ()))
_TN = (((0,), (0,)), ((), ()))


def _tiles(T):
    tm = 512 if T % 512 == 0 else T
    return tm


def _cparams(sem):
    return pltpu.CompilerParams(dimension_semantics=sem, vmem_limit_bytes=VMEM_LIMIT_BYTES)


def _rms(xf, g):
    y = xf * lax.rsqrt(jnp.mean(xf * xf, axis=-1, keepdims=True) + EPS)
    return y * g


def _silu(a):
    return a / (1.0 + jnp.exp(-a))


def _dot(a, b):
    return jnp.dot(a, b, preferred_element_type=F32)


def _in_proj_a_kernel(x_ref, g_ref, wq_ref, wk_ref, wv_ref, wgo_ref, wqm_ref, wgl_ref, wup_ref,
                      bgk_ref, q_ref, k_ref, v_ref, go_ref, qm_ref, gk_ref):
    h = _rms(x_ref[...], g_ref[...]).astype(BF16)
    q_ref[...] = _dot(h, wq_ref[...])
    k_ref[...] = _dot(h, wk_ref[...])
    v_ref[...] = _dot(h, wv_ref[...]).astype(BF16)
    go_ref[...] = _dot(h, wgo_ref[...]).astype(BF16)
    qm_ref[...] = _dot(h, wqm_ref[...]).astype(BF16)
    low = _dot(h, wgl_ref[...]).astype(BF16)
    z = _dot(low, wup_ref[...]) + bgk_ref[...]
    log_sig = jnp.minimum(z, 0.0) - jnp.log1p(jnp.exp(-jnp.abs(z)))
    gk_ref[...] = log_sig / GLA_GATE_NORM


def _in_proj_a(x, g, wq, wk, wv, wgo, wqm, wgl, wup, bgk):
    T = x.shape[0]
    tm = _tiles(T)
    row = lambda n: pl.BlockSpec((tm, n), lambda i: (i, 0))
    full = lambda a: pl.BlockSpec(a.shape, lambda i: (0, 0))
    kq, kv_, km = GLA_HEADS * GLA_DK_PAD, GLA_HEADS * GLA_DV_PAD, MEM_DIM
    return pl.pallas_call(
        _in_proj_a_kernel,
        grid=(T // tm,),
        in_specs=[row(D_MODEL)] + [full(a) for a in (g, wq, wk, wv, wgo, wqm, wgl, wup, bgk)],
        out_specs=[row(kq), row(kq), row(kv_), row(kv_), row(km), row(kq)],
        out_shape=[jax.ShapeDtypeStruct((T, kq), F32), jax.ShapeDtypeStruct((T, kq), F32),
                   jax.ShapeDtypeStruct((T, kv_), BF16), jax.ShapeDtypeStruct((T, kv_), BF16),
                   jax.ShapeDtypeStruct((T, km), BF16), jax.ShapeDtypeStruct((T, kq), F32)],
        compiler_params=_cparams(("parallel",)),
        name="in_proj_a",
    )(x, g, wq, wk, wv, wgo, wqm, wgl, wup, bgk)


def _gla_kernel(q_ref, k_ref, gk_ref, v_ref, go_ref, g_ref, o_ref, st_ref, *, n_chunks):
    C = GLA_CHUNK

    @pl.when(pl.program_id(2) == 0)
    def _():
        st_ref[...] = jnp.zeros_like(st_ref)

    r = lax.broadcasted_iota(jnp.int32, (C, C), 0)
    c = lax.broadcasted_iota(jnp.int32, (C, C), 1)
    tril = c <= r
    tril_f = tril.astype(F32)
    g = g_ref[...]
    for ci in range(n_chunks):
        sl = pl.ds(ci * C, C)
        b = jnp.dot(tril_f, gk_ref[sl, :], precision=lax.Precision.HIGHEST,
                    preferred_element_type=F32)
        b_mid = b[C // 2 - 1:C // 2, :]
        b_last = b[C - 1:C, :]
        qc = q_ref[sl, :] * (GLA_DK ** -0.5)
        kc = k_ref[sl, :]
        vc = v_ref[sl, :]
        st = st_ref[...]
        q_in = (qc * jnp.exp(b - b_mid)).astype(BF16)
        k_in = (kc * jnp.exp(b_mid - b)).astype(BF16)
        sc = lax.dot_general(q_in, k_in, _NT, preferred_element_type=F32)
        sc = jnp.where(tril, sc, 0.0)
        o = _dot(sc.astype(BF16), vc)
        q_st = (qc * jnp.exp(b)).astype(BF16)
        o = o + lax.dot_general(q_st, st.astype(BF16), _NT, preferred_element_type=F32)
        k_st = (kc * jnp.exp(b_last - b)).astype(BF16)
        kv_t = lax.dot_general(vc, k_st, _TN, preferred_element_type=F32)
        st_ref[...] = st * jnp.exp(b_last) + kv_t
        ms = jnp.sum(o * o, axis=-1, keepdims=True) * (1.0 / GLA_DV)
        on = o * lax.rsqrt(ms + EPS) * g
        go = go_ref[sl, :].astype(F32)
        o_ref[sl, :] = (on * _silu(go)).astype(BF16)


def _gla(q, k, gk, v, go, g_gla, B, S):
    blk = 512 if S % 512 == 0 else S
    n_s = S // blk
    kmap = lambda b, h, s: (b * n_s + s, h)
    return pl.pallas_call(
        functools.partial(_gla_kernel, n_chunks=blk // GLA_CHUNK),
        grid=(B, GLA_HEADS, n_s),
        in_specs=[pl.BlockSpec((blk, GLA_DK_PAD), kmap), pl.BlockSpec((blk, GLA_DK_PAD), kmap),
                  pl.BlockSpec((blk, GLA_DK_PAD), kmap), pl.BlockSpec((blk, GLA_DV_PAD), kmap),
                  pl.BlockSpec((blk, GLA_DV_PAD), kmap),
                  pl.BlockSpec((1, GLA_DV_PAD), lambda b, h, s: (0, 0))],
        out_specs=pl.BlockSpec((blk, GLA_DV_PAD), kmap),
        out_shape=jax.ShapeDtypeStruct((B * S, GLA_HEADS * GLA_DV_PAD), BF16),
        scratch_shapes=[pltpu.VMEM((GLA_DV_PAD, GLA_DK_PAD), F32)],
        compiler_params=_cparams(("parallel", "parallel", "arbitrary")),
        name="gla",
    )(q, k, gk, v, go, g_gla)


def _matmul_kernel(x_ref, w_ref, o_ref):
    o_ref[...] = _dot(x_ref[...].astype(BF16), w_ref[...]).astype(o_ref.dtype)


def _matmul(x, w, out_dtype):
    M, K = x.shape
    N = w.shape[1]
    tm = 512 if M % 512 == 0 else M
    return pl.pallas_call(
        _matmul_kernel,
        grid=(M // tm,),
        in_specs=[pl.BlockSpec((tm, K), lambda i: (i, 0)), pl.BlockSpec((K, N), lambda i: (0, 0))],
        out_specs=pl.BlockSpec((tm, N), lambda i: (i, 0)),
        out_shape=jax.ShapeDtypeStruct((M, N), out_dtype),
        compiler_params=_cparams(("parallel",)),
        name="mem_kv_proj",
    )(x, w)


def _mem_attn_kernel(q_ref, kv_ref, o_ref):
    q = q_ref[...]
    kv = kv_ref[0]
    kmat = kv[:, :MEM_DIM]
    vmat = kv[:, MEM_DIM:]
    lane = lax.broadcasted_iota(jnp.int32, kmat.shape, 1)
    acc = jnp.zeros((q.shape[0], MEM_DIM), F32)
    for h in range(MEM_HEADS):
        in_head = (lane // HEAD_DIM) == h
        kh = jnp.where(in_head, kmat, jnp.zeros_like(kmat))
        vh = jnp.where(in_head, vmat, jnp.zeros_like(vmat))
        s = lax.dot_general(q, kh, _NT, preferred_element_type=F32) * (HEAD_DIM ** -0.5)
        p = jnp.exp(s - jnp.max(s, axis=-1, keepdims=True))
        p = p / jnp.sum(p, axis=-1, keepdims=True)
        acc = acc + _dot(p.astype(BF16), vh)
    o_ref[...] = acc.astype(BF16)


def _mem_attn(qm, kv_mem, B, S):
    tm = 512 if S % 512 == 0 else S
    n_s = S // tm
    return pl.pallas_call(
        _mem_attn_kernel,
        grid=(B, n_s),
        in_specs=[pl.BlockSpec((tm, MEM_DIM), lambda b, s: (b * n_s + s, 0)),
                  pl.BlockSpec((1, MEM_LEN, 2 * MEM_DIM), lambda b, s: (b, 0, 0))],
        out_specs=pl.BlockSpec((tm, MEM_DIM), lambda b, s: (b * n_s + s, 0)),
        out_shape=jax.ShapeDtypeStruct((B * S, MEM_DIM), BF16),
        compiler_params=_cparams(("parallel", "parallel")),
        name="mem_attn",
    )(qm, kv_mem)


def _out_proj_kernel(x_ref, o_ref, m_ref, w1_ref, w2_ref, y_ref):
    y_ref[...] = x_ref[...] + _dot(o_ref[...], w1_ref[...]) + _dot(m_ref[...], w2_ref[...])


def _out_proj(x, o, m, w1, w2):
    T = x.shape[0]
    tm = _tiles(T)
    row = lambda n: pl.BlockSpec((tm, n), lambda i: (i, 0))
    full = lambda a: pl.BlockSpec(a.shape, lambda i: (0, 0))
    return pl.pallas_call(
        _out_proj_kernel,
        grid=(T // tm,),
        in_specs=[row(D_MODEL), row(o.shape[1]), row(MEM_DIM), full(w1), full(w2)],
        out_specs=row(D_MODEL),
        out_shape=jax.ShapeDtypeStruct((T, D_MODEL), F32),
        compiler_params=_cparams(("parallel",)),
        name="out_proj",
    )(x, o, m, w1, w2)


def _ffn_kernel(x_ref, g_ref, wg_ref, wu_ref, wd_ref, y_ref, h_sc, acc_sc):
    f = pl.program_id(1)

    @pl.when(f == 0)
    def _():
        h_sc[...] = _rms(x_ref[...], g_ref[...]).astype(BF16)
        acc_sc[...] = jnp.zeros_like(acc_sc)

    h = h_sc[...]
    act = (_silu(_dot(h, wg_ref[...])) * _dot(h, wu_ref[...])).astype(BF16)
    acc_sc[...] += _dot(act, wd_ref[...])

    @pl.when(f == pl.num_programs(1) - 1)
    def _():
        y_ref[...] = x_ref[...] + acc_sc[...]


def _ffn(x, g, wg, wu, wd):
    T = x.shape[0]
    tm = _tiles(T)
    F = wg.shape[1]
    tf = F // 2
    return pl.pallas_call(
        _ffn_kernel,
        grid=(T // tm, F // tf),
        in_specs=[pl.BlockSpec((tm, D_MODEL), lambda i, f: (i, 0)),
                  pl.BlockSpec((1, D_MODEL), lambda i, f: (0, 0)),
                  pl.BlockSpec((D_MODEL, tf), lambda i, f: (0, f)),
                  pl.BlockSpec((D_MODEL, tf), lambda i, f: (0, f)),
                  pl.BlockSpec((tf, D_MODEL), lambda i, f: (f, 0))],
        out_specs=pl.BlockSpec((tm, D_MODEL), lambda i, f: (i, 0)),
        out_shape=jax.ShapeDtypeStruct((T, D_MODEL), F32),
        scratch_shapes=[pltpu.VMEM((tm, D_MODEL), BF16), pltpu.VMEM((tm, D_MODEL), F32)],
        compiler_params=_cparams(("parallel", "arbitrary")),
        name="dense_ffn",
    )(x, g, wg, wu, wd)


def _proj_rope_kernel(x_ref, g_ref, wr_ref, wp_ref, cos_ref, sin_ref, r_ref, p_ref, *, scale):
    h = _rms(x_ref[...], g_ref[...]).astype(BF16)
    p_ref[...] = _dot(h, wp_ref[...]).astype(BF16)
    cos = cos_ref[...]
    sin = sin_ref[...]
    lane = lax.broadcasted_iota(jnp.int32, cos.shape, 1)
    first_half = (lane % HEAD_DIM) < (HEAD_DIM // 2)
    for j in range(wr_ref.shape[1] // LANES):
        sl = slice(j * LANES, (j + 1) * LANES)
        y = _dot(h, wr_ref[:, sl])
        partner = jnp.where(first_half, pltpu.roll(y, LANES - HEAD_DIM // 2, 1),
                            pltpu.roll(y, HEAD_DIM // 2, 1))
        r_ref[:, sl] = ((y * cos + partner * sin) * scale).astype(BF16)


def _proj_rope(x, g, w_rope, w_pass, cos, sin, S, scale):
    T = x.shape[0]
    tm = 512 if S % 512 == 0 else S
    n_s = S // tm
    row = lambda n: pl.BlockSpec((tm, n), lambda i: (i, 0))
    full = lambda a: pl.BlockSpec(a.shape, lambda i: (0, 0))
    tab = pl.BlockSpec((tm, LANES), lambda i: (i % n_s, 0))
    nr, np_ = w_rope.shape[1], w_pass.shape[1]
    return pl.pallas_call(
        functools.partial(_proj_rope_kernel, scale=scale),
        grid=(T // tm,),
        in_specs=[row(D_MODEL), full(g), full(w_rope), full(w_pass), tab, tab],
        out_specs=[row(nr), row(np_)],
        out_shape=[jax.ShapeDtypeStruct((T, nr), BF16), jax.ShapeDtypeStruct((T, np_), BF16)],
        compiler_params=_cparams(("parallel",)),
        name="proj_rope",
    )(x, g, w_rope, w_pass, cos, sin)


def _diff_attn_kernel(q_ref, k_ref, v_ref, lam_ref, g_ref, o_ref, qs_sc, m_sc, l_sc, acc_sc, *,
                      tq, tk):
    i = pl.program_id(2)
    q = q_ref[...]
    lane = lax.broadcasted_iota(jnp.int32, q.shape, 1)
    qs_sc[0:tq, :] = jnp.where(lane < HEAD_DIM, q, jnp.zeros_like(q))
    qs_sc[tq:2 * tq, :] = jnp.where(lane >= HEAD_DIM, q, jnp.zeros_like(q))
    m_sc[...] = jnp.full_like(m_sc, -1e30)
    l_sc[...] = jnp.zeros_like(l_sc)
    acc_sc[...] = jnp.zeros_like(acc_sc)

    def step(j, masked):
        kj = k_ref[pl.ds(pl.multiple_of(j * tk, tk), tk), :]
        vj = v_ref[pl.ds(pl.multiple_of(j * tk, tk), tk), :]
        s = lax.dot_general(qs_sc[...], kj, _NT, preferred_element_type=F32)
        if masked:
            qpos = i * tq + lax.broadcasted_iota(jnp.int32, s.shape, 0) % tq
            kpos = j * tk + lax.broadcasted_iota(jnp.int32, s.shape, 1)
            s = jnp.where(kpos <= qpos, s, -1e30)
        m_prev = m_sc[...]
        m_new = jnp.maximum(m_prev, jnp.max(s, axis=-1, keepdims=True))
        alpha = jnp.exp(m_prev - m_new)
        p = jnp.exp(s - m_new)
        l_sc[...] = alpha * l_sc[...] + jnp.sum(p, axis=-1, keepdims=True)
        acc_sc[...] = alpha * acc_sc[...] + _dot(p.astype(BF16), vj)
        m_sc[...] = m_new

    n_full = (i * tq) // tk

    def body(j, carry):
        step(j, False)
        return carry

    lax.fori_loop(0, n_full, body, 0)
    for d in range(tq // tk):
        step(n_full + d, True)

    lam_rows = lam_ref[...]
    dot1 = jnp.sum(lam_rows[0:1, :] * lam_rows[1:2, :], axis=-1, keepdims=True)
    dot2 = jnp.sum(lam_rows[2:3, :] * lam_rows[3:4, :], axis=-1, keepdims=True)
    lam = jnp.exp(dot1) - jnp.exp(dot2) + LAMBDA_INIT_L1
    o = acc_sc[0:tq, :] / l_sc[0:tq, :] - lam * (acc_sc[tq:2 * tq, :] / l_sc[tq:2 * tq, :])
    o_ref[...] = (_rms(o, g_ref[...]) * (1.0 - LAMBDA_INIT_L1)).astype(BF16)


def _diff_attn(q, k, v, lam_rows, g_subln, B, S):
    tq = 256 if S % 256 == 0 else S
    tk = tq
    n_q = S // tq
    return pl.pallas_call(
        functools.partial(_diff_attn_kernel, tq=tq, tk=tk),
        grid=(B, DIFF_HEADS, n_q),
        in_specs=[pl.BlockSpec((tq, DIFF_VDIM), lambda b, h, i: (b * n_q + i, h)),
                  pl.BlockSpec((S, DIFF_VDIM), lambda b, h, i: (b, h)),
                  pl.BlockSpec((S, DIFF_VDIM), lambda b, h, i: (b, h)),
                  pl.BlockSpec(lam_rows.shape, lambda b, h, i: (0, 0)),
                  pl.BlockSpec((1, DIFF_VDIM), lambda b, h, i: (0, 0))],
        out_specs=pl.BlockSpec((tq, DIFF_VDIM), lambda b, h, i: (b * n_q + i, h)),
        out_shape=jax.ShapeDtypeStruct((B * S, MIX_DIM), BF16),
        scratch_shapes=[pltpu.VMEM((2 * tq, DIFF_VDIM), BF16), pltpu.VMEM((2 * tq, 1), F32),
                        pltpu.VMEM((2 * tq, 1), F32), pltpu.VMEM((2 * tq, DIFF_VDIM), F32)],
        compiler_params=_cparams(("parallel", "parallel", "arbitrary")),
        name="diff_attn",
    )(q, k, v, lam_rows, g_subln)


_R_E0, _R_E1, _R_RANK0, _R_RANK1, _R_G0, _R_G1 = range(6)


def _router_kernel(x_ref, g_ref, wr_ref, h_ref, info_ref, cnt_ref, run_sc):
    @pl.when(pl.program_id(0) == 0)
    def _():
        run_sc[...] = jnp.zeros_like(run_sc)

    h = _rms(x_ref[...], g_ref[...])
    h_ref[...] = h
    logits = jnp.dot(h, wr_ref[...], precision=lax.Precision.HIGHEST, preferred_element_type=F32)
    tm = logits.shape[0]
    lane = lax.broadcasted_iota(jnp.int32, logits.shape, 1).astype(F32)
    neg = -jnp.inf
    lg = jnp.where(lane < N_EXPERTS, logits, neg)
    m1 = jnp.max(lg, axis=-1, keepdims=True)
    i1 = jnp.min(jnp.where(lg == m1, lane, float(LANES)), axis=-1, keepdims=True)
    lg2 = jnp.where(lane == i1, neg, lg)
    m2 = jnp.max(lg2, axis=-1, keepdims=True)
    i2 = jnp.min(jnp.where(lg2 == m2, lane, float(LANES)), axis=-1, keepdims=True)
    e = jnp.exp(m2 - m1)
    g0 = 1.0 / (1.0 + e)
    g1 = e / (1.0 + e)
    onehot = jnp.where((lane == i1) | (lane == i2), 1.0, 0.0)
    r = lax.broadcasted_iota(jnp.int32, (tm, tm), 0)
    c = lax.broadcasted_iota(jnp.int32, (tm, tm), 1)
    before = _dot(jnp.where(c < r, 1.0, 0.0).astype(BF16), onehot.astype(BF16)) + run_sc[0:1, :]
    rank0 = jnp.sum(jnp.where(lane == i1, before, 0.0), axis=-1, keepdims=True)
    rank1 = jnp.sum(jnp.where(lane == i2, before, 0.0), axis=-1, keepdims=True)
    run_sc[...] = run_sc[...] + jnp.sum(onehot, axis=0, keepdims=True)
    info = jnp.zeros_like(logits)
    for idx, val in ((_R_E0, i1), (_R_E1, i2), (_R_RANK0, rank0), (_R_RANK1, rank1),
                     (_R_G0, g0), (_R_G1, g1)):
        info = jnp.where(lane == idx, val, info)
    info_ref[...] = info
    cnt_ref[...] = run_sc[...]


def _router(x, g, wr):
    T = x.shape[0]
    tm = _tiles(T)
    return pl.pallas_call(
        _router_kernel,
        grid=(T // tm,),
        in_specs=[pl.BlockSpec((tm, D_MODEL), lambda i: (i, 0)),
                  pl.BlockSpec((1, D_MODEL), lambda i: (0, 0)),
                  pl.BlockSpec((D_MODEL, LANES), lambda i: (0, 0))],
        out_specs=[pl.BlockSpec((tm, D_MODEL), lambda i: (i, 0)),
                   pl.BlockSpec((tm, LANES), lambda i: (i, 0)),
                   pl.BlockSpec((SUBLANES, LANES), lambda i: (0, 0))],
        out_shape=[jax.ShapeDtypeStruct((T, D_MODEL), F32), jax.ShapeDtypeStruct((T, LANES), F32),
                   jax.ShapeDtypeStruct((SUBLANES, LANES), F32)],
        scratch_shapes=[pltpu.VMEM((SUBLANES, LANES), F32)],
        compiler_params=_cparams(("arbitrary",)),
        name="moe_router",
    )(x, g, wr)


def _row_copy(src, src_row, dst, dst_row, sem):
    return pltpu.make_async_copy(src.at[pl.ds(src_row, 1)], dst.at[pl.ds(dst_row, 1)], sem)


def _dispatch_kernel(d0_ref, d1_ref, h_ref, xs_in_ref, xs_ref, sem):
    del xs_in_ref
    tm = h_ref.shape[0]

    def issue(r, carry):
        _row_copy(h_ref, r, xs_ref, d0_ref[r], sem).start()
        _row_copy(h_ref, r, xs_ref, d1_ref[r], sem).start()
        return carry

    lax.fori_loop(0, tm, issue, 0)

    def drain(r, carry):
        _row_copy(h_ref, 0, xs_ref, 0, sem).wait()
        _row_copy(h_ref, 0, xs_ref, 0, sem).wait()
        return carry

    lax.fori_loop(0, tm, drain, 0)


def _dispatch(dest0, dest1, h, xs_zero):
    T = h.shape[0]
    tm = _tiles(T)
    smem = pl.BlockSpec((tm,), lambda i: (i,), memory_space=pltpu.SMEM)
    return pl.pallas_call(
        _dispatch_kernel,
        grid=(T // tm,),
        in_specs=[smem, smem, pl.BlockSpec((tm, D_MODEL), lambda i: (i, 0)),
                  pl.BlockSpec(memory_space=pl.ANY)],
        out_specs=pl.BlockSpec(memory_space=pl.ANY),
        out_shape=jax.ShapeDtypeStruct(xs_zero.shape, xs_zero.dtype),
        scratch_shapes=[pltpu.SemaphoreType.DMA],
        input_output_aliases={3: 0},
        compiler_params=_cparams(("arbitrary",)),
        name="moe_dispatch",
    )(dest0, dest1, h, xs_zero)


def _expert_kernel(bexp_ref, nused_ref, xs_ref, wg_ref, wu_ref, wd_ref, y_ref, xb_sc, acc_sc):
    del bexp_ref
    j = pl.program_id(0)
    f = pl.program_id(1)

    @pl.when(j < nused_ref[0])
    def _():
        @pl.when(f == 0)
        def _():
            xb_sc[...] = xs_ref[...].astype(BF16)
            acc_sc[...] = jnp.zeros_like(acc_sc)

        xb = xb_sc[...]
        act = (_silu(_dot(xb, wg_ref[0])) * _dot(xb, wu_ref[0])).astype(BF16)
        acc_sc[...] += _dot(act, wd_ref[0])

        @pl.when(f == pl.num_programs(1) - 1)
        def _():
            y_ref[...] = acc_sc[...]

    @pl.when((j >= nused_ref[0]) & (f == 0))
    def _():
        y_ref[...] = jnp.zeros_like(y_ref)


def _experts(block_expert, n_used, xs, wg, wu, wd, rb):
    L = xs.shape[0]
    nb = L // rb
    tf = 512
    nf = FFN_EXPERT // tf

    def blk(j, f, be, nu):
        return jnp.minimum(j, nu[0] - 1)

    def ftile(j, f, be, nu):
        return jnp.where(j < nu[0], f, nf - 1)

    return pl.pallas_call(
        _expert_kernel,
        grid_spec=pltpu.PrefetchScalarGridSpec(
            num_scalar_prefetch=2,
            grid=(nb, nf),
            in_specs=[
                pl.BlockSpec((rb, D_MODEL), lambda j, f, be, nu: (blk(j, f, be, nu), 0)),
                pl.BlockSpec((1, D_MODEL, tf),
                             lambda j, f, be, nu: (be[blk(j, f, be, nu)], 0, ftile(j, f, be, nu))),
                pl.BlockSpec((1, D_MODEL, tf),
                             lambda j, f, be, nu: (be[blk(j, f, be, nu)], 0, ftile(j, f, be, nu))),
                pl.BlockSpec((1, tf, D_MODEL),
                             lambda j, f, be, nu: (be[blk(j, f, be, nu)], ftile(j, f, be, nu), 0)),
            ],
            out_specs=pl.BlockSpec((rb, D_MODEL), lambda j, f, be, nu: (j, 0)),
            scratch_shapes=[pltpu.VMEM((rb, D_MODEL), BF16), pltpu.VMEM((rb, D_MODEL), F32)],
        ),
        out_shape=jax.ShapeDtypeStruct((L, D_MODEL), F32),
        compiler_params=_cparams(("arbitrary", "arbitrary")),
        name="moe_experts",
    )(block_expert, n_used, xs, wg, wu, wd)


def _combine_kernel(d0_ref, d1_ref, x_ref, info_ref, g_ref, y_ref, o_ref, b0_sc, b1_sc, sem):
    tm = x_ref.shape[0]

    def issue(r, carry):
        _row_copy(y_ref, d0_ref[r], b0_sc, r, sem).start()
        _row_copy(y_ref, d1_ref[r], b1_sc, r, sem).start()
        return carry

    lax.fori_loop(0, tm, issue, 0)

    def drain(r, carry):
        _row_copy(y_ref, 0, b0_sc, 0, sem).wait()
        _row_copy(y_ref, 0, b1_sc, 0, sem).wait()
        return carry

    lax.fori_loop(0, tm, drain, 0)
    info = info_ref[...]
    g0 = info[:, _R_G0:_R_G0 + 1]
    g1 = info[:, _R_G1:_R_G1 + 1]
    x = x_ref[...] + (b0_sc[...] * g0 + b1_sc[...] * g1)
    o_ref[...] = _rms(x, g_ref[...])


def _combine(dest0, dest1, x, info, g_final, y):
    T = x.shape[0]
    tm = _tiles(T)
    smem = pl.BlockSpec((tm,), lambda i: (i,), memory_space=pltpu.SMEM)
    return pl.pallas_call(
        _combine_kernel,
        grid=(T // tm,),
        in_specs=[smem, smem, pl.BlockSpec((tm, D_MODEL), lambda i: (i, 0)),
                  pl.BlockSpec((tm, LANES), lambda i: (i, 0)),
                  pl.BlockSpec((1, D_MODEL), lambda i: (0, 0)),
                  pl.BlockSpec(memory_space=pl.ANY)],
        out_specs=pl.BlockSpec((tm, D_MODEL), lambda i: (i, 0)),
        out_shape=jax.ShapeDtypeStruct((T, D_MODEL), F32),
        scratch_shapes=[pltpu.VMEM((tm, D_MODEL), F32), pltpu.VMEM((tm, D_MODEL), F32),
                        pltpu.SemaphoreType.DMA],
        compiler_params=_cparams(("arbitrary",)),
        name="moe_combine",
    )(dest0, dest1, x, info, g_final, y)


def _pad_heads_cols(w, heads, d, d_pad):
    k = w.shape[0]
    return jnp.pad(w.reshape(k, heads, d), ((0, 0), (0, 0), (0, d_pad - d))).reshape(k, heads * d_pad)


def _rope_tables(S):
    half = HEAD_DIM // 2
    inv = ROPE_THETA ** (-jnp.arange(half, dtype=F32) / half)
    ang = jnp.arange(S).astype(F32)[:, None] * inv[None, :]
    reps = LANES // half
    cos = jnp.tile(jnp.cos(ang), (1, reps))
    sign = jnp.tile(jnp.concatenate([-jnp.ones((half,), F32), jnp.ones((half,), F32)]),
                    LANES // HEAD_DIM)
    sin = jnp.tile(jnp.sin(ang), (1, reps)) * sign[None, :]
    return cos, sin


def _moe_block_rows(T):
    return 1024 if (2 * T) % 1024 == 0 else 256


def kernel(x, mem, g_mix, g_ffn, w_mem_kv, w_out, w_in_a, w_gk_up, b_gk, g_gla, g_kv, w_kv, w_in_b, lambda_q1, lambda_k1, lambda_q2, lambda_k2, g_subln, w_dense_gate, w_dense_up, w_dense_down, w_router, w_exp_gate, w_exp_up, w_exp_down, g_final):
    B, S, D = x.shape
    T = B * S
    x0 = x.reshape(T, D)
    mem2 = mem.reshape(B * MEM_LEN, D)
    row = lambda v: v.reshape(1, -1)

    wa = w_in_a[0]
    s0, s1, s2, s3, s4 = (GLA_KDIM, 2 * GLA_KDIM, 2 * GLA_KDIM + MIX_DIM, 2 * GLA_KDIM + 2 * MIX_DIM,
                          2 * GLA_KDIM + 2 * MIX_DIM + GLA_GATE_RANK)
    wq = _pad_heads_cols(wa[:, :s0], GLA_HEADS, GLA_DK, GLA_DK_PAD).astype(BF16)
    wk = _pad_heads_cols(wa[:, s0:s1], GLA_HEADS, GLA_DK, GLA_DK_PAD).astype(BF16)
    wv = _pad_heads_cols(wa[:, s1:s2], GLA_HEADS, GLA_DV, GLA_DV_PAD).astype(BF16)
    wgo = _pad_heads_cols(wa[:, s2:s3], GLA_HEADS, GLA_DV, GLA_DV_PAD).astype(BF16)
    wgl = jnp.pad(wa[:, s3:s4], ((0, 0), (0, LANES - GLA_GATE_RANK))).astype(BF16)
    wqm = wa[:, s4:].astype(BF16)
    wup = jnp.pad(_pad_heads_cols(w_gk_up[0], GLA_HEADS, GLA_DK, GLA_DK_PAD),
                  ((0, LANES - GLA_GATE_RANK), (0, 0))).astype(BF16)
    bgk = _pad_heads_cols(row(b_gk[0]), GLA_HEADS, GLA_DK, GLA_DK_PAD)
    ggla = jnp.pad(row(g_gla[0]), ((0, 0), (0, GLA_DV_PAD - GLA_DV)))
    wo_a = jnp.pad(w_out[0][:MIX_DIM].reshape(GLA_HEADS, GLA_DV, D),
                   ((0, 0), (0, GLA_DV_PAD - GLA_DV), (0, 0))).reshape(GLA_HEADS * GLA_DV_PAD, D)

    q, k, v, go, qm, gk = _in_proj_a(x0, row(g_mix[0]), wq, wk, wv, wgo, wqm, wgl, wup, bgk)
    o = _gla(q, k, gk, v, go, ggla, B, S)
    kv_mem0 = _matmul(mem2, w_mem_kv[0].astype(BF16), BF16).reshape(B, MEM_LEN, 2 * MEM_DIM)
    m = _mem_attn(qm, kv_mem0, B, S)
    x1 = _out_proj(x0, o, m, wo_a.astype(BF16), w_out[0][MIX_DIM:].astype(BF16))
    x2 = _ffn(x1, row(g_ffn[0]), w_dense_gate[0].astype(BF16), w_dense_up[0].astype(BF16),
              w_dense_down[0].astype(BF16))

    cos, sin = _rope_tables(S)
    k_sh, v_sh = _proj_rope(x2, row(g_kv), w_kv[:, :MIX_DIM].astype(BF16),
                            w_kv[:, MIX_DIM:].astype(BF16), cos, sin, S, 1.0)

    q1, qm1 = _proj_rope(x2, row(g_mix[1]), w_in_b[0][:, :MIX_DIM].astype(BF16),
                         w_in_b[0][:, MIX_DIM:].astype(BF16), cos, sin, S, HEAD_DIM ** -0.5)
    lam_rows = jnp.pad(jnp.stack([lambda_q1[0], lambda_k1[0], lambda_q2[0], lambda_k2[0]]),
                       ((0, SUBLANES - 4), (0, LANES - HEAD_DIM)))
    o1 = _diff_attn(q1, k_sh, v_sh, lam_rows, row(g_subln[0]), B, S)
    kv_mem1 = _matmul(mem2, w_mem_kv[1].astype(BF16), BF16).reshape(B, MEM_LEN, 2 * MEM_DIM)
    m1 = _mem_attn(qm1, kv_mem1, B, S)
    x3 = _out_proj(x2, o1, m1, w_out[1][:MIX_DIM].astype(BF16), w_out[1][MIX_DIM:].astype(BF16))

    wr = jnp.pad(w_router[0], ((0, 0), (0, LANES - N_EXPERTS)))
    h, info, cnt = _router(x3, row(g_ffn[1]), wr)

    rb = _moe_block_rows(T)
    counts = cnt[0, :N_EXPERTS].astype(jnp.int32)
    padded = (counts + rb - 1) // rb * rb
    pend = jnp.cumsum(padded)
    pstart = pend - padded
    e0 = info[:, _R_E0].astype(jnp.int32)
    e1 = info[:, _R_E1].astype(jnp.int32)
    eids = jnp.arange(N_EXPERTS, dtype=jnp.int32)
    start_of = lambda e: jnp.sum(jnp.where(e[:, None] == eids[None, :], pstart[None, :], 0), axis=1)
    dest0 = start_of(e0) + info[:, _R_RANK0].astype(jnp.int32)
    dest1 = start_of(e1) + info[:, _R_RANK1].astype(jnp.int32)
    nb = (2 * T) // rb + N_EXPERTS
    block_expert = jnp.clip(jnp.sum(pend[None, :] <= (jnp.arange(nb, dtype=jnp.int32) * rb)[:, None],
                                    axis=1), 0, N_EXPERTS - 1).astype(jnp.int32)
    n_used = (pend[-1:] // rb).astype(jnp.int32)

    xs = _dispatch(dest0, dest1, h, jnp.zeros((nb * rb, D), F32))
    ys = _experts(block_expert, n_used, xs, w_exp_gate[0].astype(BF16), w_exp_up[0].astype(BF16),
                  w_exp_down[0].astype(BF16), rb)
    out = _combine(dest0, dest1, x3, info, row(g_final), ys)
    return out.reshape(B, S, D)
```

```python
import functools
import math

import jax
import jax.numpy as jnp
from jax import lax
from jax.experimental import pallas as pl
from jax.experimental.pallas import tpu as pltpu

D_MODEL = 1024
MEM_LEN = 256
HEAD_DIM = 64
MEM_HEADS = 4
MEM_DIM = MEM_HEADS * HEAD_DIM
MIX_DIM = D_MODEL - MEM_DIM
GLA_HEADS = 4
GLA_DV = MIX_DIM // GLA_HEADS
GLA_DK = GLA_DV // 2
GLA_KDIM = GLA_HEADS * GLA_DK
GLA_GATE_RANK = 16
GLA_GATE_NORM = 16.0
GLA_CHUNK = 64
DIFF_HEADS = MIX_DIM // (2 * HEAD_DIM)
DIFF_VDIM = 2 * HEAD_DIM
ROPE_THETA = 10000.0
FFN_DENSE = 2816
N_EXPERTS = 8
FFN_EXPERT = 3584
EPS = 1e-6
LAMBDA_INIT_L1 = 0.8 - 0.6 * math.exp(-0.3 * 1)

LANES = 128
SUBLANES = 8
VMEM_LIMIT_BYTES = 52 * 1024 * 1024

GLA_DK_PAD = LANES
GLA_DV_PAD = 2 * LANES
DIFF_TQ = 256
DIFF_TK = 512

F32 = jnp.float32
BF16 = jnp.bfloat16
_NT = (((1,), (1,)), ((), ()))
_TN = (((0,), (0,)), ((), ()))


def _tiles(T):
    tm = 512 if T % 512 == 0 else T
    return tm


def _cparams(sem):
    return pltpu.CompilerParams(dimension_semantics=sem, vmem_limit_bytes=VMEM_LIMIT_BYTES)


def _rms(xf, g):
    y = xf * lax.rsqrt(jnp.mean(xf * xf, axis=-1, keepdims=True) + EPS)
    return y * g


def _silu(a):
    return a / (1.0 + jnp.exp(-a))


def _dot(a, b):
    return jnp.dot(a, b, preferred_element_type=F32)


def _in_proj_a_kernel(x_ref, g_ref, wq_ref, wk_ref, wv_ref, wgo_ref, wqm_ref, wgl_ref, wup_ref,
                      bgk_ref, q_ref, k_ref, v_ref, go_ref, qm_ref, gk_ref):
    h = _rms(x_ref[...], g_ref[...]).astype(BF16)
    q_ref[...] = _dot(h, wq_ref[...])
    k_ref[...] = _dot(h, wk_ref[...])
    v_ref[...] = _dot(h, wv_ref[...]).astype(BF16)
    go_ref[...] = _dot(h, wgo_ref[...]).astype(BF16)
    qm_ref[...] = _dot(h, wqm_ref[...]).astype(BF16)
    low = _dot(h, wgl_ref[...]).astype(BF16)
    z = _dot(low, wup_ref[...]) + bgk_ref[...]
    log_sig = jnp.minimum(z, 0.0) - jnp.log1p(jnp.exp(-jnp.abs(z)))
    gk_ref[...] = log_sig / GLA_GATE_NORM


def _in_proj_a(x, g, wq, wk, wv, wgo, wqm, wgl, wup, bgk):
    T = x.shape[0]
    tm = _tiles(T)
    row = lambda n: pl.BlockSpec((tm, n), lambda i: (i, 0))
    full = lambda a: pl.BlockSpec(a.shape, lambda i: (0, 0))
    kq, kv_, km = GLA_HEADS * GLA_DK_PAD, GLA_HEADS * GLA_DV_PAD, MEM_DIM
    return pl.pallas_call(
        _in_proj_a_kernel,
        grid=(T // tm,),
        in_specs=[row(D_MODEL)] + [full(a) for a in (g, wq, wk, wv, wgo, wqm, wgl, wup, bgk)],
        out_specs=[row(kq), row(kq), row(kv_), row(kv_), row(km), row(kq)],
        out_shape=[jax.ShapeDtypeStruct((T, kq), F32), jax.ShapeDtypeStruct((T, kq), F32),
                   jax.ShapeDtypeStruct((T, kv_), BF16), jax.ShapeDtypeStruct((T, kv_), BF16),
                   jax.ShapeDtypeStruct((T, km), BF16), jax.ShapeDtypeStruct((T, kq), F32)],
        compiler_params=_cparams(("parallel",)),
        name="in_proj_a",
    )(x, g, wq, wk, wv, wgo, wqm, wgl, wup, bgk)


def _gla_kernel(q_ref, k_ref, gk_ref, v_ref, go_ref, g_ref, o_ref, st_ref, *, n_chunks):
    C = GLA_CHUNK

    @pl.when(pl.program_id(2) == 0)
    def _():
        st_ref[...] = jnp.zeros_like(st_ref)

    r = lax.broadcasted_iota(jnp.int32, (C, C), 0)
    c = lax.broadcasted_iota(jnp.int32, (C, C), 1)
    tril = c <= r
    tril_f = tril.astype(F32)
    g = g_ref[...]
    for ci in range(n_chunks):
        sl = pl.ds(ci * C, C)
        b = jnp.dot(tril_f, gk_ref[sl, :], precision=lax.Precision.HIGHEST,
                    preferred_element_type=F32)
        b_mid = b[C // 2 - 1:C // 2, :]
        b_last = b[C - 1:C, :]
        qc = q_ref[sl, :] * (GLA_DK ** -0.5)
        kc = k_ref[sl, :]
        vc = v_ref[sl, :]
        st = st_ref[...]
        q_in = (qc * jnp.exp(b - b_mid)).astype(BF16)
        k_in = (kc * jnp.exp(b_mid - b)).astype(BF16)
        sc = lax.dot_general(q_in, k_in, _NT, preferred_element_type=F32)
        sc = jnp.where(tril, sc, 0.0)
        o = _dot(sc.astype(BF16), vc)
        q_st = (qc * jnp.exp(b)).astype(BF16)
        o = o + lax.dot_general(q_st, st.astype(BF16), _NT, preferred_element_type=F32)
        k_st = (kc * jnp.exp(b_last - b)).astype(BF16)
        kv_t = lax.dot_general(vc, k_st, _TN, preferred_element_type=F32)
        st_ref[...] = st * jnp.exp(b_last) + kv_t
        ms = jnp.sum(o * o, axis=-1, keepdims=True) * (1.0 / GLA_DV)
        on = o * lax.rsqrt(ms + EPS) * g
        go = go_ref[sl, :].astype(F32)
        o_ref[sl, :] = (on * _silu(go)).astype(BF16)


def _gla(q, k, gk, v, go, g_gla, B, S):
    blk = 512 if S % 512 == 0 else S
    n_s = S // blk
    kmap = lambda b, h, s: (b * n_s + s, h)
    return pl.pallas_call(
        functools.partial(_gla_kernel, n_chunks=blk // GLA_CHUNK),
        grid=(B, GLA_HEADS, n_s),
        in_specs=[pl.BlockSpec((blk, GLA_DK_PAD), kmap), pl.BlockSpec((blk, GLA_DK_PAD), kmap),
                  pl.BlockSpec((blk, GLA_DK_PAD), kmap), pl.BlockSpec((blk, GLA_DV_PAD), kmap),
                  pl.BlockSpec((blk, GLA_DV_PAD), kmap),
                  pl.BlockSpec((1, GLA_DV_PAD), lambda b, h, s: (0, 0))],
        out_specs=pl.BlockSpec((blk, GLA_DV_PAD), kmap),
        out_shape=jax.ShapeDtypeStruct((B * S, GLA_HEADS * GLA_DV_PAD), BF16),
        scratch_shapes=[pltpu.VMEM((GLA_DV_PAD, GLA_DK_PAD), F32)],
        compiler_params=_cparams(("parallel", "parallel", "arbitrary")),
        name="gla",
    )(q, k, gk, v, go, g_gla)


def _matmul_kernel(x_ref, w_ref, o_ref):
    o_ref[...] = _dot(x_ref[...].astype(BF16), w_ref[...]).astype(o_ref.dtype)


def _matmul(x, w, out_dtype):
    M, K = x.shape
    N = w.shape[1]
    tm = 512 if M % 512 == 0 else M
    return pl.pallas_call(
        _matmul_kernel,
        grid=(M // tm,),
        in_specs=[pl.BlockSpec((tm, K), lambda i: (i, 0)), pl.BlockSpec((K, N), lambda i: (0, 0))],
        out_specs=pl.BlockSpec((tm, N), lambda i: (i, 0)),
        out_shape=jax.ShapeDtypeStruct((M, N), out_dtype),
        compiler_params=_cparams(("parallel",)),
        name="mem_kv_proj",
    )(x, w)


def _mem_attn_kernel(q_ref, kv_ref, o_ref):
    q = q_ref[...]
    kv = kv_ref[0]
    kmat = kv[:, :MEM_DIM]
    vmat = kv[:, MEM_DIM:]
    lane = lax.broadcasted_iota(jnp.int32, kmat.shape, 1)
    acc = jnp.zeros((q.shape[0], MEM_DIM), F32)
    for h in range(MEM_HEADS):
        in_head = (lane // HEAD_DIM) == h
        kh = jnp.where(in_head, kmat, jnp.zeros_like(kmat))
        vh = jnp.where(in_head, vmat, jnp.zeros_like(vmat))
        s = lax.dot_general(q, kh, _NT, preferred_element_type=F32) * (HEAD_DIM ** -0.5)
        p = jnp.exp(s - jnp.max(s, axis=-1, keepdims=True))
        p = p / jnp.sum(p, axis=-1, keepdims=True)
        acc = acc + _dot(p.astype(BF16), vh)
    o_ref[...] = acc.astype(BF16)


def _mem_attn(qm, kv_mem, B, S):
    tm = 512 if S % 512 == 0 else S
    n_s = S // tm
    return pl.pallas_call(
        _mem_attn_kernel,
        grid=(B, n_s),
        in_specs=[pl.BlockSpec((tm, MEM_DIM), lambda b, s: (b * n_s + s, 0)),
                  pl.BlockSpec((1, MEM_LEN, 2 * MEM_DIM), lambda b, s: (b, 0, 0))],
        out_specs=pl.BlockSpec((tm, MEM_DIM), lambda b, s: (b * n_s + s, 0)),
        out_shape=jax.ShapeDtypeStruct((B * S, MEM_DIM), BF16),
        compiler_params=_cparams(("parallel", "parallel")),
        name="mem_attn",
    )(qm, kv_mem)


def _out_proj_kernel(x_ref, o_ref, m_ref, w1_ref, w2_ref, y_ref):
    y_ref[...] = x_ref[...] + _dot(o_ref[...], w1_ref[...]) + _dot(m_ref[...], w2_ref[...])


def _out_proj(x, o, m, w1, w2):
    T = x.shape[0]
    tm = _tiles(T)
    row = lambda n: pl.BlockSpec((tm, n), lambda i: (i, 0))
    full = lambda a: pl.BlockSpec(a.shape, lambda i: (0, 0))
    return pl.pallas_call(
        _out_proj_kernel,
        grid=(T // tm,),
        in_specs=[row(D_MODEL), row(o.shape[1]), row(MEM_DIM), full(w1), full(w2)],
        out_specs=row(D_MODEL),
        out_shape=jax.ShapeDtypeStruct((T, D_MODEL), F32),
        compiler_params=_cparams(("parallel",)),
        name="out_proj",
    )(x, o, m, w1, w2)


def _ffn_kernel(x_ref, g_ref, wg_ref, wu_ref, wd_ref, y_ref, h_sc, acc_sc):
    f = pl.program_id(1)

    @pl.when(f == 0)
    def _():
        h_sc[...] = _rms(x_ref[...], g_ref[...]).astype(BF16)
        acc_sc[...] = jnp.zeros_like(acc_sc)

    h = h_sc[...]
    act = (_silu(_dot(h, wg_ref[...])) * _dot(h, wu_ref[...])).astype(BF16)
    acc_sc[...] += _dot(act, wd_ref[...])

    @pl.when(f == pl.num_programs(1) - 1)
    def _():
        y_ref[...] = x_ref[...] + acc_sc[...]


def _ffn(x, g, wg, wu, wd):
    T = x.shape[0]
    tm = _tiles(T)
    F = wg.shape[1]
    tf = F // 2
    return pl.pallas_call(
        _ffn_kernel,
        grid=(T // tm, F // tf),
        in_specs=[pl.BlockSpec((tm, D_MODEL), lambda i, f: (i, 0)),
                  pl.BlockSpec((1, D_MODEL), lambda i, f: (0, 0)),
                  pl.BlockSpec((D_MODEL, tf), lambda i, f: (0, f)),
                  pl.BlockSpec((D_MODEL, tf), lambda i, f: (0, f)),
                  pl.BlockSpec((tf, D_MODEL), lambda i, f: (f, 0))],
        out_specs=pl.BlockSpec((tm, D_MODEL), lambda i, f: (i, 0)),
        out_shape=jax.ShapeDtypeStruct((T, D_MODEL), F32),
        scratch_shapes=[pltpu.VMEM((tm, D_MODEL), BF16), pltpu.VMEM((tm, D_MODEL), F32)],
        compiler_params=_cparams(("parallel", "arbitrary")),
        name="dense_ffn",
    )(x, g, wg, wu, wd)


def _proj_rope_kernel(x_ref, g_ref, wr_ref, wp_ref, cos_ref, sin_ref, r_ref, p_ref, *, scale,
                      transpose_pass):
    h = _rms(x_ref[...], g_ref[...]).astype(BF16)
    p = _dot(h, wp_ref[...])
    if transpose_pass:
        p_ref[0] = p.T.astype(BF16)
    else:
        p_ref[...] = p.astype(BF16)
    cos = cos_ref[...]
    sin = sin_ref[...]
    lane = lax.broadcasted_iota(jnp.int32, cos.shape, 1)
    first_half = (lane % HEAD_DIM) < (HEAD_DIM // 2)
    y_all = _dot(h, wr_ref[...])
    for j in range(wr_ref.shape[1] // LANES):
        sl = slice(j * LANES, (j + 1) * LANES)
        y = y_all[:, sl]
        partner = jnp.where(first_half, pltpu.roll(y, LANES - HEAD_DIM // 2, 1),
                            pltpu.roll(y, HEAD_DIM // 2, 1))
        r_ref[:, sl] = ((y * cos + partner * sin) * scale).astype(BF16)


def _proj_rope(x, g, w_rope, w_pass, cos, sin, S, scale, transpose_pass):
    T = x.shape[0]
    tm = DIFF_TK if S % DIFF_TK == 0 else S
    n_s = S // tm
    row = lambda n: pl.BlockSpec((tm, n), lambda i: (i, 0))
    full = lambda a: pl.BlockSpec(a.shape, lambda i: (0, 0))
    tab = pl.BlockSpec((tm, LANES), lambda i: (i % n_s, 0))
    nr, np_ = w_rope.shape[1], w_pass.shape[1]
    if transpose_pass:
        pass_spec = pl.BlockSpec((1, np_, tm), lambda i: (i, 0, 0))
        pass_shape = jax.ShapeDtypeStruct((T // tm, np_, tm), BF16)
    else:
        pass_spec = row(np_)
        pass_shape = jax.ShapeDtypeStruct((T, np_), BF16)
    return pl.pallas_call(
        functools.partial(_proj_rope_kernel, scale=scale, transpose_pass=transpose_pass),
        grid=(T // tm,),
        in_specs=[row(D_MODEL), full(g), full(w_rope), full(w_pass), tab, tab],
        out_specs=[row(nr), pass_spec],
        out_shape=[jax.ShapeDtypeStruct((T, nr), BF16), pass_shape],
        compiler_params=_cparams(("parallel",)),
        name="proj_rope",
    )(x, g, w_rope, w_pass, cos, sin)


def _diff_attn_kernel(q_ref, k_ref, vt_ref, lam_ref, g_ref, o_ref, qs_sc, m_sc, l_sc, acc_sc, *,
                      tq, tk):
    i = pl.program_id(2)
    q = q_ref[...]
    lane = lax.broadcasted_iota(jnp.int32, q.shape, 1)
    qs_sc[0:tq, :] = jnp.where(lane < HEAD_DIM, q, jnp.zeros_like(q))
    qs_sc[tq:2 * tq, :] = jnp.where(lane >= HEAD_DIM, q, jnp.zeros_like(q))
    m_sc[...] = jnp.full_like(m_sc, -1e30)
    l_sc[...] = jnp.zeros_like(l_sc)
    acc_sc[...] = jnp.zeros_like(acc_sc)

    def step(j, masked):
        kj = k_ref[pl.ds(pl.multiple_of(j * tk, tk), tk), :]
        s = lax.dot_general(kj, qs_sc[...], _NT, preferred_element_type=F32)
        if masked:
            kpos = j * tk + lax.broadcasted_iota(jnp.int32, s.shape, 0)
            qpos = i * tq + lax.broadcasted_iota(jnp.int32, s.shape, 1) % tq
            s = jnp.where(kpos <= qpos, s, -1e30)
        m_prev = m_sc[...]
        m_new = jnp.maximum(m_prev, jnp.max(s, axis=0, keepdims=True))
        alpha = jnp.exp(m_prev - m_new)
        p = jnp.exp(s - m_new)
        l_sc[...] = alpha * l_sc[...] + jnp.sum(p, axis=0, keepdims=True)
        acc_sc[...] = alpha * acc_sc[...] + _dot(vt_ref[j], p.astype(BF16))
        m_sc[...] = m_new

    n_full = (i * tq) // tk

    def body(j, carry):
        step(j, False)
        return carry

    lax.fori_loop(0, n_full, body, 0)
    step(n_full, True)

    lam_rows = lam_ref[...]
    dot1 = jnp.sum(lam_rows[0:1, :] * lam_rows[1:2, :], axis=-1, keepdims=True)
    dot2 = jnp.sum(lam_rows[2:3, :] * lam_rows[3:4, :], axis=-1, keepdims=True)
    lam = jnp.exp(dot1) - jnp.exp(dot2) + LAMBDA_INIT_L1
    ot = (acc_sc[:, 0:tq] / l_sc[:, 0:tq]
          - lam * (acc_sc[:, tq:2 * tq] / l_sc[:, tq:2 * tq]))
    ot = ot * lax.rsqrt(jnp.mean(ot * ot, axis=0, keepdims=True) + EPS)
    o_ref[...] = ((ot.T * g_ref[...]) * (1.0 - LAMBDA_INIT_L1)).astype(BF16)


def _diff_attn(q, k, vt, lam_rows, g_subln, B, S):
    tk = vt.shape[2]
    tq = DIFF_TQ if S % DIFF_TQ == 0 else S
    assert tq <= tk and tk % tq == 0 and S % tk == 0
    n_q = S // tq
    n_k = S // tk
    return pl.pallas_call(
        functools.partial(_diff_attn_kernel, tq=tq, tk=tk),
        grid=(B, DIFF_HEADS, n_q),
        in_specs=[pl.BlockSpec((tq, DIFF_VDIM), lambda b, h, i: (b * n_q + i, h)),
                  pl.BlockSpec((S, DIFF_VDIM), lambda b, h, i: (b, h)),
                  pl.BlockSpec((n_k, DIFF_VDIM, tk), lambda b, h, i: (b, h, 0)),
                  pl.BlockSpec(lam_rows.shape, lambda b, h, i: (0, 0)),
                  pl.BlockSpec((1, DIFF_VDIM), lambda b, h, i: (0, 0))],
        out_specs=pl.BlockSpec((tq, DIFF_VDIM), lambda b, h, i: (b * n_q + i, h)),
        out_shape=jax.ShapeDtypeStruct((B * S, MIX_DIM), BF16),
        scratch_shapes=[pltpu.VMEM((2 * tq, DIFF_VDIM), BF16), pltpu.VMEM((1, 2 * tq), F32),
                        pltpu.VMEM((1, 2 * tq), F32), pltpu.VMEM((DIFF_VDIM, 2 * tq), F32)],
        compiler_params=_cparams(("parallel", "parallel", "arbitrary")),
        name="diff_attn",
    )(q, k, vt, lam_rows, g_subln)


_R_E0, _R_E1, _R_RANK0, _R_RANK1, _R_G0, _R_G1 = range(6)


def _router_kernel(x_ref, g_ref, wr_ref, h_ref, info_ref, cnt_ref, run_sc):
    @pl.when(pl.program_id(0) == 0)
    def _():
        run_sc[...] = jnp.zeros_like(run_sc)

    h = _rms(x_ref[...], g_ref[...])
    h_ref[...] = h
    logits = jnp.dot(h, wr_ref[...], precision=lax.Precision.HIGHEST, preferred_element_type=F32)
    tm = logits.shape[0]
    lane = lax.broadcasted_iota(jnp.int32, logits.shape, 1).astype(F32)
    neg = -jnp.inf
    lg = jnp.where(lane < N_EXPERTS, logits, neg)
    m1 = jnp.max(lg, axis=-1, keepdims=True)
    i1 = jnp.min(jnp.where(lg == m1, lane, float(LANES)), axis=-1, keepdims=True)
    lg2 = jnp.where(lane == i1, neg, lg)
    m2 = jnp.max(lg2, axis=-1, keepdims=True)
    i2 = jnp.min(jnp.where(lg2 == m2, lane, float(LANES)), axis=-1, keepdims=True)
    e = jnp.exp(m2 - m1)
    g0 = 1.0 / (1.0 + e)
    g1 = e / (1.0 + e)
    onehot = jnp.where((lane == i1) | (lane == i2), 1.0, 0.0)
    r = lax.broadcasted_iota(jnp.int32, (tm, tm), 0)
    c = lax.broadcasted_iota(jnp.int32, (tm, tm), 1)
    before = _dot(jnp.where(c < r, 1.0, 0.0).astype(BF16), onehot.astype(BF16)) + run_sc[0:1, :]
    rank0 = jnp.sum(jnp.where(lane == i1, before, 0.0), axis=-1, keepdims=True)
    rank1 = jnp.sum(jnp.where(lane == i2, before, 0.0), axis=-1, keepdims=True)
    run_sc[...] = run_sc[...] + jnp.sum(onehot, axis=0, keepdims=True)
    info = jnp.zeros_like(logits)
    for idx, val in ((_R_E0, i1), (_R_E1, i2), (_R_RANK0, rank0), (_R_RANK1, rank1),
                     (_R_G0, g0), (_R_G1, g1)):
        info = jnp.where(lane == idx, val, info)
    info_ref[...] = info
    cnt_ref[...] = run_sc[...]


def _router(x, g, wr):
    T = x.shape[0]
    tm = _tiles(T)
    return pl.pallas_call(
        _router_kernel,
        grid=(T // tm,),
        in_specs=[pl.BlockSpec((tm, D_MODEL), lambda i: (i, 0)),
                  pl.BlockSpec((1, D_MODEL), lambda i: (0, 0)),
                  pl.BlockSpec((D_MODEL, LANES), lambda i: (0, 0))],
        out_specs=[pl.BlockSpec((tm, D_MODEL), lambda i: (i, 0)),
                   pl.BlockSpec((tm, LANES), lambda i: (i, 0)),
                   pl.BlockSpec((SUBLANES, LANES), lambda i: (0, 0))],
        out_shape=[jax.ShapeDtypeStruct((T, D_MODEL), F32), jax.ShapeDtypeStruct((T, LANES), F32),
                   jax.ShapeDtypeStruct((SUBLANES, LANES), F32)],
        scratch_shapes=[pltpu.VMEM((SUBLANES, LANES), F32)],
        compiler_params=_cparams(("arbitrary",)),
        name="moe_router",
    )(x, g, wr)


def _row_copy(src, src_row, dst, dst_row, sem):
    return pltpu.make_async_copy(src.at[pl.ds(src_row, 1)], dst.at[pl.ds(dst_row, 1)], sem)


def _dispatch_kernel(d0_ref, d1_ref, h_ref, xs_in_ref, xs_ref, sem):
    del xs_in_ref
    tm = h_ref.shape[0]

    def issue(r, carry):
        _row_copy(h_ref, r, xs_ref, d0_ref[r], sem).start()
        _row_copy(h_ref, r, xs_ref, d1_ref[r], sem).start()
        return carry

    lax.fori_loop(0, tm, issue, 0)

    def drain(r, carry):
        _row_copy(h_ref, 0, xs_ref, 0, sem).wait()
        _row_copy(h_ref, 0, xs_ref, 0, sem).wait()
        return carry

    lax.fori_loop(0, tm, drain, 0)


def _dispatch(dest0, dest1, h, xs_zero):
    T = h.shape[0]
    tm = _tiles(T)
    smem = pl.BlockSpec((tm,), lambda i: (i,), memory_space=pltpu.SMEM)
    return pl.pallas_call(
        _dispatch_kernel,
        grid=(T // tm,),
        in_specs=[smem, smem, pl.BlockSpec((tm, D_MODEL), lambda i: (i, 0)),
                  pl.BlockSpec(memory_space=pl.ANY)],
        out_specs=pl.BlockSpec(memory_space=pl.ANY),
        out_shape=jax.ShapeDtypeStruct(xs_zero.shape, xs_zero.dtype),
        scratch_shapes=[pltpu.SemaphoreType.DMA],
        input_output_aliases={3: 0},
        compiler_params=_cparams(("arbitrary",)),
        name="moe_dispatch",
    )(dest0, dest1, h, xs_zero)


def _expert_kernel(bexp_ref, nused_ref, xs_ref, wg_ref, wu_ref, wd_ref, y_ref, xb_sc, acc_sc):
    del bexp_ref
    j = pl.program_id(0)
    f = pl.program_id(1)

    @pl.when(j < nused_ref[0])
    def _():
        @pl.when(f == 0)
        def _():
            xb_sc[...] = xs_ref[...].astype(BF16)
            acc_sc[...] = jnp.zeros_like(acc_sc)

        xb = xb_sc[...]
        act = (_silu(_dot(xb, wg_ref[0])) * _dot(xb, wu_ref[0])).astype(BF16)
        acc_sc[...] += _dot(act, wd_ref[0])

        @pl.when(f == pl.num_programs(1) - 1)
        def _():
            y_ref[...] = acc_sc[...]

    @pl.when((j >= nused_ref[0]) & (f == 0))
    def _():
        y_ref[...] = jnp.zeros_like(y_ref)


def _experts(block_expert, n_used, xs, wg, wu, wd, rb):
    L = xs.shape[0]
    nb = L // rb
    tf = 512
    nf = FFN_EXPERT // tf

    def blk(j, f, be, nu):
        return jnp.minimum(j, nu[0] - 1)

    def ftile(j, f, be, nu):
        return jnp.where(j < nu[0], f, nf - 1)

    return pl.pallas_call(
        _expert_kernel,
        grid_spec=pltpu.PrefetchScalarGridSpec(
            num_scalar_prefetch=2,
            grid=(nb, nf),
            in_specs=[
                pl.BlockSpec((rb, D_MODEL), lambda j, f, be, nu: (blk(j, f, be, nu), 0)),
                pl.BlockSpec((1, D_MODEL, tf),
                             lambda j, f, be, nu: (be[blk(j, f, be, nu)], 0, ftile(j, f, be, nu))),
                pl.BlockSpec((1, D_MODEL, tf),
                             lambda j, f, be, nu: (be[blk(j, f, be, nu)], 0, ftile(j, f, be, nu))),
                pl.BlockSpec((1, tf, D_MODEL),
                             lambda j, f, be, nu: (be[blk(j, f, be, nu)], ftile(j, f, be, nu), 0)),
            ],
            out_specs=pl.BlockSpec((rb, D_MODEL), lambda j, f, be, nu: (j, 0)),
            scratch_shapes=[pltpu.VMEM((rb, D_MODEL), BF16), pltpu.VMEM((rb, D_MODEL), F32)],
        ),
        out_shape=jax.ShapeDtypeStruct((L, D_MODEL), F32),
        compiler_params=_cparams(("arbitrary", "arbitrary")),
        name="moe_experts",
    )(block_expert, n_used, xs, wg, wu, wd)


def _combine_kernel(d0_ref, d1_ref, x_ref, info_ref, g_ref, y_ref, o_ref, b0_sc, b1_sc, sem):
    tm = x_ref.shape[0]

    def issue(r, carry):
        _row_copy(y_ref, d0_ref[r], b0_sc, r, sem).start()
        _row_copy(y_ref, d1_ref[r], b1_sc, r, sem).start()
        return carry

    lax.fori_loop(0, tm, issue, 0)

    def drain(r, carry):
        _row_copy(y_ref, 0, b0_sc, 0, sem).wait()
        _row_copy(y_ref, 0, b1_sc, 0, sem).wait()
        return carry

    lax.fori_loop(0, tm, drain, 0)
    info = info_ref[...]
    g0 = info[:, _R_G0:_R_G0 + 1]
    g1 = info[:, _R_G1:_R_G1 + 1]
    x = x_ref[...] + (b0_sc[...] * g0 + b1_sc[...] * g1)
    o_ref[...] = _rms(x, g_ref[...])


def _combine(dest0, dest1, x, info, g_final, y):
    T = x.shape[0]
    tm = _tiles(T)
    smem = pl.BlockSpec((tm,), lambda i: (i,), memory_space=pltpu.SMEM)
    return pl.pallas_call(
        _combine_kernel,
        grid=(T // tm,),
        in_specs=[smem, smem, pl.BlockSpec((tm, D_MODEL), lambda i: (i, 0)),
                  pl.BlockSpec((tm, LANES), lambda i: (i, 0)),
                  pl.BlockSpec((1, D_MODEL), lambda i: (0, 0)),
                  pl.BlockSpec(memory_space=pl.ANY)],
        out_specs=pl.BlockSpec((tm, D_MODEL), lambda i: (i, 0)),
        out_shape=jax.ShapeDtypeStruct((T, D_MODEL), F32),
        scratch_shapes=[pltpu.VMEM((tm, D_MODEL), F32), pltpu.VMEM((tm, D_MODEL), F32),
                        pltpu.SemaphoreType.DMA],
        compiler_params=_cparams(("arbitrary",)),
        name="moe_combine",
    )(dest0, dest1, x, info, g_final, y)


def _pad_heads_cols(w, heads, d, d_pad):
    k = w.shape[0]
    return jnp.pad(w.reshape(k, heads, d), ((0, 0), (0, 0), (0, d_pad - d))).reshape(k, heads * d_pad)


def _rope_tables(S):
    half = HEAD_DIM // 2
    inv = ROPE_THETA ** (-jnp.arange(half, dtype=F32) / half)
    ang = jnp.arange(S).astype(F32)[:, None] * inv[None, :]
    reps = LANES // half
    cos = jnp.tile(jnp.cos(ang), (1, reps))
    sign = jnp.tile(jnp.concatenate([-jnp.ones((half,), F32), jnp.ones((half,), F32)]),
                    LANES // HEAD_DIM)
    sin = jnp.tile(jnp.sin(ang), (1, reps)) * sign[None, :]
    return cos, sin


def _moe_block_rows(T):
    return 1024 if (2 * T) % 1024 == 0 else 256


def kernel(x, mem, g_mix, g_ffn, w_mem_kv, w_out, w_in_a, w_gk_up, b_gk, g_gla, g_kv, w_kv, w_in_b, lambda_q1, lambda_k1, lambda_q2, lambda_k2, g_subln, w_dense_gate, w_dense_up, w_dense_down, w_router, w_exp_gate, w_exp_up, w_exp_down, g_final):
    B, S, D = x.shape
    T = B * S
    x0 = x.reshape(T, D)
    mem2 = mem.reshape(B * MEM_LEN, D)
    row = lambda v: v.reshape(1, -1)

    wa = w_in_a[0]
    s0, s1, s2, s3, s4 = (GLA_KDIM, 2 * GLA_KDIM, 2 * GLA_KDIM + MIX_DIM, 2 * GLA_KDIM + 2 * MIX_DIM,
                          2 * GLA_KDIM + 2 * MIX_DIM + GLA_GATE_RANK)
    wq = _pad_heads_cols(wa[:, :s0], GLA_HEADS, GLA_DK, GLA_DK_PAD).astype(BF16)
    wk = _pad_heads_cols(wa[:, s0:s1], GLA_HEADS, GLA_DK, GLA_DK_PAD).astype(BF16)
    wv = _pad_heads_cols(wa[:, s1:s2], GLA_HEADS, GLA_DV, GLA_DV_PAD).astype(BF16)
    wgo = _pad_heads_cols(wa[:, s2:s3], GLA_HEADS, GLA_DV, GLA_DV_PAD).astype(BF16)
    wgl = jnp.pad(wa[:, s3:s4], ((0, 0), (0, LANES - GLA_GATE_RANK))).astype(BF16)
    wqm = wa[:, s4:].astype(BF16)
    wup = jnp.pad(_pad_heads_cols(w_gk_up[0], GLA_HEADS, GLA_DK, GLA_DK_PAD),
                  ((0, LANES - GLA_GATE_RANK), (0, 0))).astype(BF16)
    bgk = _pad_heads_cols(row(b_gk[0]), GLA_HEADS, GLA_DK, GLA_DK_PAD)
    ggla = jnp.pad(row(g_gla[0]), ((0, 0), (0, GLA_DV_PAD - GLA_DV)))
    wo_a = jnp.pad(w_out[0][:MIX_DIM].reshape(GLA_HEADS, GLA_DV, D),
                   ((0, 0), (0, GLA_DV_PAD - GLA_DV), (0, 0))).reshape(GLA_HEADS * GLA_DV_PAD, D)

    q, k, v, go, qm, gk = _in_proj_a(x0, row(g_mix[0]), wq, wk, wv, wgo, wqm, wgl, wup, bgk)
    o = _gla(q, k, gk, v, go, ggla, B, S)
    kv_mem0 = _matmul(mem2, w_mem_kv[0].astype(BF16), BF16).reshape(B, MEM_LEN, 2 * MEM_DIM)
    m = _mem_attn(qm, kv_mem0, B, S)
    x1 = _out_proj(x0, o, m, wo_a.astype(BF16), w_out[0][MIX_DIM:].astype(BF16))
    x2 = _ffn(x1, row(g_ffn[0]), w_dense_gate[0].astype(BF16), w_dense_up[0].astype(BF16),
              w_dense_down[0].astype(BF16))

    cos, sin = _rope_tables(S)
    k_sh, vt_sh = _proj_rope(x2, row(g_kv), w_kv[:, :MIX_DIM].astype(BF16),
                             w_kv[:, MIX_DIM:].astype(BF16), cos, sin, S, 1.0, True)

    q1, qm1 = _proj_rope(x2, row(g_mix[1]), w_in_b[0][:, :MIX_DIM].astype(BF16),
                         w_in_b[0][:, MIX_DIM:].astype(BF16), cos, sin, S, HEAD_DIM ** -0.5, False)
    lam_rows = jnp.pad(jnp.stack([lambda_q1[0], lambda_k1[0], lambda_q2[0], lambda_k2[0]]),
                       ((0, SUBLANES - 4), (0, LANES - HEAD_DIM)))
    o1 = _diff_attn(q1, k_sh, vt_sh, lam_rows, row(g_subln[0]), B, S)
    kv_mem1 = _matmul(mem2, w_mem_kv[1].astype(BF16), BF16).reshape(B, MEM_LEN, 2 * MEM_DIM)
    m1 = _mem_attn(qm1, kv_mem1, B, S)
    x3 = _out_proj(x2, o1, m1, w_out[1][:MIX_DIM].astype(BF16), w_out[1][MIX_DIM:].astype(BF16))

    wr = jnp.pad(w_router[0], ((0, 0), (0, LANES - N_EXPERTS)))
    h, info, cnt = _router(x3, row(g_ffn[1]), wr)

    rb = _moe_block_rows(T)
    counts = cnt[0, :N_EXPERTS].astype(jnp.int32)
    padded = (counts + rb - 1) // rb * rb
    pend = jnp.cumsum(padded)
    pstart = pend - padded
    e0 = info[:, _R_E0].astype(jnp.int32)
    e1 = info[:, _R_E1].astype(jnp.int32)
    eids = jnp.arange(N_EXPERTS, dtype=jnp.int32)
    start_of = lambda e: jnp.sum(jnp.where(e[:, None] == eids[None, :], pstart[None, :], 0), axis=1)
    dest0 = start_of(e0) + info[:, _R_RANK0].astype(jnp.int32)
    dest1 = start_of(e1) + info[:, _R_RANK1].astype(jnp.int32)
    nb = (2 * T) // rb + N_EXPERTS
    block_expert = jnp.clip(jnp.sum(pend[None, :] <= (jnp.arange(nb, dtype=jnp.int32) * rb)[:, None],
                                    axis=1), 0, N_EXPERTS - 1).astype(jnp.int32)
    n_used = (pend[-1:] // rb).astype(jnp.int32)

    xs = _dispatch(dest0, dest1, h, jnp.zeros((nb * rb, D), F32))
    ys = _experts(block_expert, n_used, xs, w_exp_gate[0].astype(BF16), w_exp_up[0].astype(BF16),
                  w_exp_down[0].astype(BF16), rb)
    out = _combine(dest0, dest1, x3, info, row(g_final), ys)
    return out.reshape(B, S, D)
```

```python
import functools
import math

import jax
import jax.numpy as jnp
from jax import lax
from jax.experimental import pallas as pl
from jax.experimental.pallas import tpu as pltpu

D_MODEL = 1024
MEM_LEN = 256
HEAD_DIM = 64
MEM_HEADS = 4
MEM_DIM = MEM_HEADS * HEAD_DIM
MIX_DIM = D_MODEL - MEM_DIM
GLA_HEADS = 4
GLA_DV = MIX_DIM // GLA_HEADS
GLA_DK = GLA_DV // 2
GLA_KDIM = GLA_HEADS * GLA_DK
GLA_GATE_RANK = 16
GLA_GATE_NORM = 16.0
GLA_CHUNK = 64
DIFF_HEADS = MIX_DIM // (2 * HEAD_DIM)
DIFF_VDIM = 2 * HEAD_DIM
ROPE_THETA = 10000.0
FFN_DENSE = 2816
N_EXPERTS = 8
FFN_EXPERT = 3584
EPS = 1e-6
LAMBDA_INIT_L1 = 0.8 - 0.6 * math.exp(-0.3 * 1)

LANES = 128
SUBLANES = 8
VMEM_LIMIT_BYTES = 52 * 1024 * 1024

GLA_DK_PAD = LANES
GLA_DV_PAD = 2 * LANES
DIFF_TQ = 256
DIFF_TK = 512
DIFF_ONES_ROWS = 16
LOG2_E = 1.4426950408889634

F32 = jnp.float32
BF16 = jnp.bfloat16
_NT = (((1,), (1,)), ((), ()))
_TN = (((0,), (0,)), ((), ()))


def _tiles(T):
    tm = 512 if T % 512 == 0 else T
    return tm


def _cparams(sem):
    return pltpu.CompilerParams(dimension_semantics=sem, vmem_limit_bytes=VMEM_LIMIT_BYTES)


def _rms(xf, g):
    y = xf * lax.rsqrt(jnp.mean(xf * xf, axis=-1, keepdims=True) + EPS)
    return y * g


def _silu(a):
    return a / (1.0 + jnp.exp(-a))


def _dot(a, b):
    return jnp.dot(a, b, preferred_element_type=F32)


def _in_proj_a_kernel(x_ref, g_ref, wq_ref, wk_ref, wv_ref, wgo_ref, wqm_ref, wgl_ref, wup_ref,
                      bgk_ref, q_ref, k_ref, v_ref, go_ref, qm_ref, gk_ref):
    h = _rms(x_ref[...], g_ref[...]).astype(BF16)
    q_ref[...] = _dot(h, wq_ref[...])
    k_ref[...] = _dot(h, wk_ref[...])
    v_ref[...] = _dot(h, wv_ref[...]).astype(BF16)
    go_ref[...] = _dot(h, wgo_ref[...]).astype(BF16)
    qm_ref[...] = _dot(h, wqm_ref[...]).astype(BF16)
    low = _dot(h, wgl_ref[...]).astype(BF16)
    z = _dot(low, wup_ref[...]) + bgk_ref[...]
    log_sig = jnp.minimum(z, 0.0) - jnp.log1p(jnp.exp(-jnp.abs(z)))
    gk_ref[...] = log_sig / GLA_GATE_NORM


def _in_proj_a(x, g, wq, wk, wv, wgo, wqm, wgl, wup, bgk):
    T = x.shape[0]
    tm = _tiles(T)
    row = lambda n: pl.BlockSpec((tm, n), lambda i: (i, 0))
    full = lambda a: pl.BlockSpec(a.shape, lambda i: (0, 0))
    kq, kv_, km = GLA_HEADS * GLA_DK_PAD, GLA_HEADS * GLA_DV_PAD, MEM_DIM
    return pl.pallas_call(
        _in_proj_a_kernel,
        grid=(T // tm,),
        in_specs=[row(D_MODEL)] + [full(a) for a in (g, wq, wk, wv, wgo, wqm, wgl, wup, bgk)],
        out_specs=[row(kq), row(kq), row(kv_), row(kv_), row(km), row(kq)],
        out_shape=[jax.ShapeDtypeStruct((T, kq), F32), jax.ShapeDtypeStruct((T, kq), F32),
                   jax.ShapeDtypeStruct((T, kv_), BF16), jax.ShapeDtypeStruct((T, kv_), BF16),
                   jax.ShapeDtypeStruct((T, km), BF16), jax.ShapeDtypeStruct((T, kq), F32)],
        compiler_params=_cparams(("parallel",)),
        name="in_proj_a",
    )(x, g, wq, wk, wv, wgo, wqm, wgl, wup, bgk)


def _gla_kernel(q_ref, k_ref, gk_ref, v_ref, go_ref, g_ref, o_ref, st_ref, *, n_chunks):
    C = GLA_CHUNK

    @pl.when(pl.program_id(2) == 0)
    def _():
        st_ref[...] = jnp.zeros_like(st_ref)

    r = lax.broadcasted_iota(jnp.int32, (C, C), 0)
    c = lax.broadcasted_iota(jnp.int32, (C, C), 1)
    tril = c <= r
    tril_f = tril.astype(F32)
    g = g_ref[...]
    for ci in range(n_chunks):
        sl = pl.ds(ci * C, C)
        b = jnp.dot(tril_f, gk_ref[sl, :], precision=lax.Precision.HIGHEST,
                    preferred_element_type=F32)
        b_mid = b[C // 2 - 1:C // 2, :]
        b_last = b[C - 1:C, :]
        qc = q_ref[sl, :] * (GLA_DK ** -0.5)
        kc = k_ref[sl, :]
        vc = v_ref[sl, :]
        st = st_ref[...]
        q_in = (qc * jnp.exp(b - b_mid)).astype(BF16)
        k_in = (kc * jnp.exp(b_mid - b)).astype(BF16)
        sc = lax.dot_general(q_in, k_in, _NT, preferred_element_type=F32)
        sc = jnp.where(tril, sc, 0.0)
        o = _dot(sc.astype(BF16), vc)
        q_st = (qc * jnp.exp(b)).astype(BF16)
        o = o + lax.dot_general(q_st, st.astype(BF16), _NT, preferred_element_type=F32)
        k_st = (kc * jnp.exp(b_last - b)).astype(BF16)
        kv_t = lax.dot_general(vc, k_st, _TN, preferred_element_type=F32)
        st_ref[...] = st * jnp.exp(b_last) + kv_t
        ms = jnp.sum(o * o, axis=-1, keepdims=True) * (1.0 / GLA_DV)
        on = o * lax.rsqrt(ms + EPS) * g
        go = go_ref[sl, :].astype(F32)
        o_ref[sl, :] = (on * _silu(go)).astype(BF16)


def _gla(q, k, gk, v, go, g_gla, B, S):
    blk = 512 if S % 512 == 0 else S
    n_s = S // blk
    kmap = lambda b, h, s: (b * n_s + s, h)
    return pl.pallas_call(
        functools.partial(_gla_kernel, n_chunks=blk // GLA_CHUNK),
        grid=(B, GLA_HEADS, n_s),
        in_specs=[pl.BlockSpec((blk, GLA_DK_PAD), kmap), pl.BlockSpec((blk, GLA_DK_PAD), kmap),
                  pl.BlockSpec((blk, GLA_DK_PAD), kmap), pl.BlockSpec((blk, GLA_DV_PAD), kmap),
                  pl.BlockSpec((blk, GLA_DV_PAD), kmap),
                  pl.BlockSpec((1, GLA_DV_PAD), lambda b, h, s: (0, 0))],
        out_specs=pl.BlockSpec((blk, GLA_DV_PAD), kmap),
        out_shape=jax.ShapeDtypeStruct((B * S, GLA_HEADS * GLA_DV_PAD), BF16),
        scratch_shapes=[pltpu.VMEM((GLA_DV_PAD, GLA_DK_PAD), F32)],
        compiler_params=_cparams(("parallel", "parallel", "arbitrary")),
        name="gla",
    )(q, k, gk, v, go, g_gla)


def _matmul_kernel(x_ref, w_ref, o_ref):
    o_ref[...] = _dot(x_ref[...].astype(BF16), w_ref[...]).astype(o_ref.dtype)


def _matmul(x, w, out_dtype):
    M, K = x.shape
    N = w.shape[1]
    tm = 512 if M % 512 == 0 else M
    return pl.pallas_call(
        _matmul_kernel,
        grid=(M // tm,),
        in_specs=[pl.BlockSpec((tm, K), lambda i: (i, 0)), pl.BlockSpec((K, N), lambda i: (0, 0))],
        out_specs=pl.BlockSpec((tm, N), lambda i: (i, 0)),
        out_shape=jax.ShapeDtypeStruct((M, N), out_dtype),
        compiler_params=_cparams(("parallel",)),
        name="mem_kv_proj",
    )(x, w)


def _mem_attn_kernel(q_ref, kv_ref, o_ref):
    q = q_ref[...]
    kv = kv_ref[0]
    kmat = kv[:, :MEM_DIM]
    vmat = kv[:, MEM_DIM:]
    lane = lax.broadcasted_iota(jnp.int32, kmat.shape, 1)
    acc = jnp.zeros((q.shape[0], MEM_DIM), F32)
    for h in range(MEM_HEADS):
        in_head = (lane // HEAD_DIM) == h
        kh = jnp.where(in_head, kmat, jnp.zeros_like(kmat))
        vh = jnp.where(in_head, vmat, jnp.zeros_like(vmat))
        s = lax.dot_general(q, kh, _NT, preferred_element_type=F32) * (HEAD_DIM ** -0.5)
        p = jnp.exp(s - jnp.max(s, axis=-1, keepdims=True))
        p = p / jnp.sum(p, axis=-1, keepdims=True)
        acc = acc + _dot(p.astype(BF16), vh)
    o_ref[...] = acc.astype(BF16)


def _mem_attn(qm, kv_mem, B, S):
    tm = 512 if S % 512 == 0 else S
    n_s = S // tm
    return pl.pallas_call(
        _mem_attn_kernel,
        grid=(B, n_s),
        in_specs=[pl.BlockSpec((tm, MEM_DIM), lambda b, s: (b * n_s + s, 0)),
                  pl.BlockSpec((1, MEM_LEN, 2 * MEM_DIM), lambda b, s: (b, 0, 0))],
        out_specs=pl.BlockSpec((tm, MEM_DIM), lambda b, s: (b * n_s + s, 0)),
        out_shape=jax.ShapeDtypeStruct((B * S, MEM_DIM), BF16),
        compiler_params=_cparams(("parallel", "parallel")),
        name="mem_attn",
    )(qm, kv_mem)


def _out_proj_kernel(x_ref, o_ref, m_ref, w1_ref, w2_ref, y_ref):
    y_ref[...] = x_ref[...] + _dot(o_ref[...], w1_ref[...]) + _dot(m_ref[...], w2_ref[...])


def _out_proj(x, o, m, w1, w2):
    T = x.shape[0]
    tm = _tiles(T)
    row = lambda n: pl.BlockSpec((tm, n), lambda i: (i, 0))
    full = lambda a: pl.BlockSpec(a.shape, lambda i: (0, 0))
    return pl.pallas_call(
        _out_proj_kernel,
        grid=(T // tm,),
        in_specs=[row(D_MODEL), row(o.shape[1]), row(MEM_DIM), full(w1), full(w2)],
        out_specs=row(D_MODEL),
        out_shape=jax.ShapeDtypeStruct((T, D_MODEL), F32),
        compiler_params=_cparams(("parallel",)),
        name="out_proj",
    )(x, o, m, w1, w2)


def _ffn_kernel(x_ref, g_ref, wg_ref, wu_ref, wd_ref, y_ref, h_sc, acc_sc):
    f = pl.program_id(1)

    @pl.when(f == 0)
    def _():
        h_sc[...] = _rms(x_ref[...], g_ref[...]).astype(BF16)
        acc_sc[...] = jnp.zeros_like(acc_sc)

    h = h_sc[...]
    act = (_silu(_dot(h, wg_ref[...])) * _dot(h, wu_ref[...])).astype(BF16)
    acc_sc[...] += _dot(act, wd_ref[...])

    @pl.when(f == pl.num_programs(1) - 1)
    def _():
        y_ref[...] = x_ref[...] + acc_sc[...]


def _ffn(x, g, wg, wu, wd):
    T = x.shape[0]
    tm = _tiles(T)
    F = wg.shape[1]
    tf = F // 2
    return pl.pallas_call(
        _ffn_kernel,
        grid=(T // tm, F // tf),
        in_specs=[pl.BlockSpec((tm, D_MODEL), lambda i, f: (i, 0)),
                  pl.BlockSpec((1, D_MODEL), lambda i, f: (0, 0)),
                  pl.BlockSpec((D_MODEL, tf), lambda i, f: (0, f)),
                  pl.BlockSpec((D_MODEL, tf), lambda i, f: (0, f)),
                  pl.BlockSpec((tf, D_MODEL), lambda i, f: (f, 0))],
        out_specs=pl.BlockSpec((tm, D_MODEL), lambda i, f: (i, 0)),
        out_shape=jax.ShapeDtypeStruct((T, D_MODEL), F32),
        scratch_shapes=[pltpu.VMEM((tm, D_MODEL), BF16), pltpu.VMEM((tm, D_MODEL), F32)],
        compiler_params=_cparams(("parallel", "arbitrary")),
        name="dense_ffn",
    )(x, g, wg, wu, wd)


def _proj_rope_kernel(x_ref, g_ref, wr_ref, wp_ref, cos_ref, sin_ref, r_ref, p_ref, *, scale,
                      transpose_pass):
    h = _rms(x_ref[...], g_ref[...]).astype(BF16)
    p = _dot(h, wp_ref[...])
    if transpose_pass:
        pt = p.T.astype(BF16)
        rows = DIFF_VDIM + DIFF_ONES_ROWS
        for hh in range(p.shape[1] // DIFF_VDIM):
            p_ref[0, hh * rows:hh * rows + DIFF_VDIM, :] = pt[hh * DIFF_VDIM:(hh + 1) * DIFF_VDIM, :]
            p_ref[0, hh * rows + DIFF_VDIM:(hh + 1) * rows, :] = jnp.ones(
                (DIFF_ONES_ROWS, pt.shape[1]), BF16)
    else:
        p_ref[...] = p.astype(BF16)
    cos = cos_ref[...]
    sin = sin_ref[...]
    lane = lax.broadcasted_iota(jnp.int32, cos.shape, 1)
    first_half = (lane % HEAD_DIM) < (HEAD_DIM // 2)
    y_all = _dot(h, wr_ref[...])
    for j in range(wr_ref.shape[1] // LANES):
        sl = slice(j * LANES, (j + 1) * LANES)
        y = y_all[:, sl]
        partner = jnp.where(first_half, pltpu.roll(y, LANES - HEAD_DIM // 2, 1),
                            pltpu.roll(y, HEAD_DIM // 2, 1))
        r_ref[:, sl] = ((y * cos + partner * sin) * scale).astype(BF16)


def _proj_rope(x, g, w_rope, w_pass, cos, sin, S, scale, transpose_pass):
    T = x.shape[0]
    tm = DIFF_TK if S % DIFF_TK == 0 else S
    n_s = S // tm
    row = lambda n: pl.BlockSpec((tm, n), lambda i: (i, 0))
    full = lambda a: pl.BlockSpec(a.shape, lambda i: (0, 0))
    tab = pl.BlockSpec((tm, LANES), lambda i: (i % n_s, 0))
    nr, np_ = w_rope.shape[1], w_pass.shape[1]
    if transpose_pass:
        nt = np_ // DIFF_VDIM * (DIFF_VDIM + DIFF_ONES_ROWS)
        pass_spec = pl.BlockSpec((1, nt, tm), lambda i: (i, 0, 0))
        pass_shape = jax.ShapeDtypeStruct((T // tm, nt, tm), BF16)
    else:
        pass_spec = row(np_)
        pass_shape = jax.ShapeDtypeStruct((T, np_), BF16)
    return pl.pallas_call(
        functools.partial(_proj_rope_kernel, scale=scale, transpose_pass=transpose_pass),
        grid=(T // tm,),
        in_specs=[row(D_MODEL), full(g), full(w_rope), full(w_pass), tab, tab],
        out_specs=[row(nr), pass_spec],
        out_shape=[jax.ShapeDtypeStruct((T, nr), BF16), pass_shape],
        compiler_params=_cparams(("parallel",)),
        name="proj_rope",
    )(x, g, w_rope, w_pass, cos, sin)


def _diff_attn_kernel(q_ref, k_ref, vt_ref, lam_ref, g_ref, o_ref, qs_sc, m_sc, acc_sc, s0_sc, s1_sc,
                      *, tq, tk):
    i = pl.program_id(2)
    q = q_ref[...]
    lane = lax.broadcasted_iota(jnp.int32, q.shape, 1)
    qs_sc[0:tq, :] = jnp.where(lane < HEAD_DIM, q, jnp.zeros_like(q))
    qs_sc[tq:2 * tq, :] = jnp.where(lane >= HEAD_DIM, q, jnp.zeros_like(q))
    m_sc[...] = jnp.full_like(m_sc, -1e30)
    acc_sc[...] = jnp.zeros_like(acc_sc)

    def scores(j, s_ref):
        kj = k_ref[pl.ds(pl.multiple_of(j * tk, tk), tk), :]
        s_ref[...] = lax.dot_general(kj, qs_sc[...], _NT, preferred_element_type=F32)

    def update(j, s_ref, masked):
        s = s_ref[...]
        if masked:
            kpos = j * tk + lax.broadcasted_iota(jnp.int32, s.shape, 0)
            qpos = i * tq + lax.broadcasted_iota(jnp.int32, s.shape, 1) % tq
            s = jnp.where(kpos <= qpos, s, -1e30)
        m_prev = m_sc[...]
        m_new = jnp.maximum(m_prev, jnp.max(s, axis=0, keepdims=True))
        alpha = jnp.exp2(m_prev - m_new)
        p = jnp.exp2(s - m_new).astype(BF16)
        acc_sc[...] = alpha * acc_sc[...] + _dot(vt_ref[j], p)
        m_sc[...] = m_new

    n_full = (i * tq) // tk
    scores(0, s0_sc)

    def body(jj, carry):
        a = 2 * jj
        scores(a + 1, s1_sc)
        update(a, s0_sc, False)
        scores(a + 2, s0_sc)
        update(a + 1, s1_sc, False)
        return carry

    lax.fori_loop(0, n_full // 2, body, 0)

    @pl.when(n_full % 2 == 1)
    def _():
        scores(n_full, s1_sc)
        update(n_full - 1, s0_sc, False)
        update(n_full, s1_sc, True)

    @pl.when(n_full % 2 == 0)
    def _():
        update(n_full, s0_sc, True)

    lam_rows = lam_ref[...]
    dot1 = jnp.sum(lam_rows[0:1, :] * lam_rows[1:2, :], axis=-1, keepdims=True)
    dot2 = jnp.sum(lam_rows[2:3, :] * lam_rows[3:4, :], axis=-1, keepdims=True)
    lam = jnp.exp(dot1) - jnp.exp(dot2) + LAMBDA_INIT_L1
    dv = DIFF_VDIM
    ot = (acc_sc[0:dv, 0:tq] / acc_sc[dv:dv + 1, 0:tq]
          - lam * (acc_sc[0:dv, tq:2 * tq] / acc_sc[dv:dv + 1, tq:2 * tq]))
    ot = ot * lax.rsqrt(jnp.mean(ot * ot, axis=0, keepdims=True) + EPS)
    o_ref[...] = ((ot.T * g_ref[...]) * (1.0 - LAMBDA_INIT_L1)).astype(BF16)


def _diff_attn(q, k, vt, lam_rows, g_subln, B, S):
    tk = vt.shape[2]
    tq = DIFF_TQ if S % DIFF_TQ == 0 else S
    assert tq <= tk and tk % tq == 0 and S % tk == 0
    n_q = S // tq
    n_k = S // tk
    rows = DIFF_VDIM + DIFF_ONES_ROWS
    return pl.pallas_call(
        functools.partial(_diff_attn_kernel, tq=tq, tk=tk),
        grid=(B, DIFF_HEADS, n_q),
        in_specs=[pl.BlockSpec((tq, DIFF_VDIM), lambda b, h, i: (b * n_q + i, h)),
                  pl.BlockSpec((S, DIFF_VDIM), lambda b, h, i: (b, h)),
                  pl.BlockSpec((n_k, rows, tk), lambda b, h, i: (b, h, 0)),
                  pl.BlockSpec(lam_rows.shape, lambda b, h, i: (0, 0)),
                  pl.BlockSpec((1, DIFF_VDIM), lambda b, h, i: (0, 0))],
        out_specs=pl.BlockSpec((tq, DIFF_VDIM), lambda b, h, i: (b * n_q + i, h)),
        out_shape=jax.ShapeDtypeStruct((B * S, MIX_DIM), BF16),
        scratch_shapes=[pltpu.VMEM((2 * tq, DIFF_VDIM), BF16), pltpu.VMEM((1, 2 * tq), F32),
                        pltpu.VMEM((rows, 2 * tq), F32), pltpu.VMEM((tk, 2 * tq), F32),
                        pltpu.VMEM((tk, 2 * tq), F32)],
        compiler_params=_cparams(("parallel", "parallel", "arbitrary")),
        name="diff_attn",
    )(q, k, vt, lam_rows, g_subln)


_R_E0, _R_E1, _R_RANK0, _R_RANK1, _R_G0, _R_G1 = range(6)


def _router_kernel(x_ref, g_ref, wr_ref, h_ref, info_ref, cnt_ref, run_sc):
    @pl.when(pl.program_id(0) == 0)
    def _():
        run_sc[...] = jnp.zeros_like(run_sc)

    h = _rms(x_ref[...], g_ref[...])
    h_ref[...] = h
    logits = jnp.dot(h, wr_ref[...], precision=lax.Precision.HIGHEST, preferred_element_type=F32)
    tm = logits.shape[0]
    lane = lax.broadcasted_iota(jnp.int32, logits.shape, 1).astype(F32)
    neg = -jnp.inf
    lg = jnp.where(lane < N_EXPERTS, logits, neg)
    m1 = jnp.max(lg, axis=-1, keepdims=True)
    i1 = jnp.min(jnp.where(lg == m1, lane, float(LANES)), axis=-1, keepdims=True)
    lg2 = jnp.where(lane == i1, neg, lg)
    m2 = jnp.max(lg2, axis=-1, keepdims=True)
    i2 = jnp.min(jnp.where(lg2 == m2, lane, float(LANES)), axis=-1, keepdims=True)
    e = jnp.exp(m2 - m1)
    g0 = 1.0 / (1.0 + e)
    g1 = e / (1.0 + e)
    onehot = jnp.where((lane == i1) | (lane == i2), 1.0, 0.0)
    r = lax.broadcasted_iota(jnp.int32, (tm, tm), 0)
    c = lax.broadcasted_iota(jnp.int32, (tm, tm), 1)
    before = _dot(jnp.where(c < r, 1.0, 0.0).astype(BF16), onehot.astype(BF16)) + run_sc[0:1, :]
    rank0 = jnp.sum(jnp.where(lane == i1, before, 0.0), axis=-1, keepdims=True)
    rank1 = jnp.sum(jnp.where(lane == i2, before, 0.0), axis=-1, keepdims=True)
    run_sc[...] = run_sc[...] + jnp.sum(onehot, axis=0, keepdims=True)
    info = jnp.zeros_like(logits)
    for idx, val in ((_R_E0, i1), (_R_E1, i2), (_R_RANK0, rank0), (_R_RANK1, rank1),
                     (_R_G0, g0), (_R_G1, g1)):
        info = jnp.where(lane == idx, val, info)
    info_ref[...] = info
    cnt_ref[...] = run_sc[...]


def _router(x, g, wr):
    T = x.shape[0]
    tm = _tiles(T)
    return pl.pallas_call(
        _router_kernel,
        grid=(T // tm,),
        in_specs=[pl.BlockSpec((tm, D_MODEL), lambda i: (i, 0)),
                  pl.BlockSpec((1, D_MODEL), lambda i: (0, 0)),
                  pl.BlockSpec((D_MODEL, LANES), lambda i: (0, 0))],
        out_specs=[pl.BlockSpec((tm, D_MODEL), lambda i: (i, 0)),
                   pl.BlockSpec((tm, LANES), lambda i: (i, 0)),
                   pl.BlockSpec((SUBLANES, LANES), lambda i: (0, 0))],
        out_shape=[jax.ShapeDtypeStruct((T, D_MODEL), F32), jax.ShapeDtypeStruct((T, LANES), F32),
                   jax.ShapeDtypeStruct((SUBLANES, LANES), F32)],
        scratch_shapes=[pltpu.VMEM((SUBLANES, LANES), F32)],
        compiler_params=_cparams(("arbitrary",)),
        name="moe_router",
    )(x, g, wr)


def _row_copy(src, src_row, dst, dst_row, sem):
    return pltpu.make_async_copy(src.at[pl.ds(src_row, 1)], dst.at[pl.ds(dst_row, 1)], sem)


def _dispatch_kernel(d0_ref, d1_ref, h_ref, xs_in_ref, xs_ref, sem):
    del xs_in_ref
    tm = h_ref.shape[0]

    def issue(r, carry):
        _row_copy(h_ref, r, xs_ref, d0_ref[r], sem).start()
        _row_copy(h_ref, r, xs_ref, d1_ref[r], sem).start()
        return carry

    lax.fori_loop(0, tm, issue, 0)

    def drain(r, carry):
        _row_copy(h_ref, 0, xs_ref, 0, sem).wait()
        _row_copy(h_ref, 0, xs_ref, 0, sem).wait()
        return carry

    lax.fori_loop(0, tm, drain, 0)


def _dispatch(dest0, dest1, h, xs_zero):
    T = h.shape[0]
    tm = _tiles(T)
    smem = pl.BlockSpec((tm,), lambda i: (i,), memory_space=pltpu.SMEM)
    return pl.pallas_call(
        _dispatch_kernel,
        grid=(T // tm,),
        in_specs=[smem, smem, pl.BlockSpec((tm, D_MODEL), lambda i: (i, 0)),
                  pl.BlockSpec(memory_space=pl.ANY)],
        out_specs=pl.BlockSpec(memory_space=pl.ANY),
        out_shape=jax.ShapeDtypeStruct(xs_zero.shape, xs_zero.dtype),
        scratch_shapes=[pltpu.SemaphoreType.DMA],
        input_output_aliases={3: 0},
        compiler_params=_cparams(("arbitrary",)),
        name="moe_dispatch",
    )(dest0, dest1, h, xs_zero)


def _expert_kernel(bexp_ref, nused_ref, xs_ref, wg_ref, wu_ref, wd_ref, y_ref, xb_sc, acc_sc):
    del bexp_ref
    j = pl.program_id(0)
    f = pl.program_id(1)

    @pl.when(j < nused_ref[0])
    def _():
        @pl.when(f == 0)
        def _():
            xb_sc[...] = xs_ref[...].astype(BF16)
            acc_sc[...] = jnp.zeros_like(acc_sc)

        xb = xb_sc[...]
        act = (_silu(_dot(xb, wg_ref[0])) * _dot(xb, wu_ref[0])).astype(BF16)
        acc_sc[...] += _dot(act, wd_ref[0])

        @pl.when(f == pl.num_programs(1) - 1)
        def _():
            y_ref[...] = acc_sc[...]

    @pl.when((j >= nused_ref[0]) & (f == 0))
    def _():
        y_ref[...] = jnp.zeros_like(y_ref)


def _experts(block_expert, n_used, xs, wg, wu, wd, rb):
    L = xs.shape[0]
    nb = L // rb
    tf = 512
    nf = FFN_EXPERT // tf

    def blk(j, f, be, nu):
        return jnp.minimum(j, nu[0] - 1)

    def ftile(j, f, be, nu):
        return jnp.where(j < nu[0], f, nf - 1)

    return pl.pallas_call(
        _expert_kernel,
        grid_spec=pltpu.PrefetchScalarGridSpec(
            num_scalar_prefetch=2,
            grid=(nb, nf),
            in_specs=[
                pl.BlockSpec((rb, D_MODEL), lambda j, f, be, nu: (blk(j, f, be, nu), 0)),
                pl.BlockSpec((1, D_MODEL, tf),
                             lambda j, f, be, nu: (be[blk(j, f, be, nu)], 0, ftile(j, f, be, nu))),
                pl.BlockSpec((1, D_MODEL, tf),
                             lambda j, f, be, nu: (be[blk(j, f, be, nu)], 0, ftile(j, f, be, nu))),
                pl.BlockSpec((1, tf, D_MODEL),
                             lambda j, f, be, nu: (be[blk(j, f, be, nu)], ftile(j, f, be, nu), 0)),
            ],
            out_specs=pl.BlockSpec((rb, D_MODEL), lambda j, f, be, nu: (j, 0)),
            scratch_shapes=[pltpu.VMEM((rb, D_MODEL), BF16), pltpu.VMEM((rb, D_MODEL), F32)],
        ),
        out_shape=jax.ShapeDtypeStruct((L, D_MODEL), F32),
        compiler_params=_cparams(("arbitrary", "arbitrary")),
        name="moe_experts",
    )(block_expert, n_used, xs, wg, wu, wd)


def _combine_kernel(d0_ref, d1_ref, x_ref, info_ref, g_ref, y_ref, o_ref, b0_sc, b1_sc, sem):
    tm = x_ref.shape[0]

    def issue(r, carry):
        _row_copy(y_ref, d0_ref[r], b0_sc, r, sem).start()
        _row_copy(y_ref, d1_ref[r], b1_sc, r, sem).start()
        return carry

    lax.fori_loop(0, tm, issue, 0)

    def drain(r, carry):
        _row_copy(y_ref, 0, b0_sc, 0, sem).wait()
        _row_copy(y_ref, 0, b1_sc, 0, sem).wait()
        return carry

    lax.fori_loop(0, tm, drain, 0)
    info = info_ref[...]
    g0 = info[:, _R_G0:_R_G0 + 1]
    g1 = info[:, _R_G1:_R_G1 + 1]
    x = x_ref[...] + (b0_sc[...] * g0 + b1_sc[...] * g1)
    o_ref[...] = _rms(x, g_ref[...])


def _combine(dest0, dest1, x, info, g_final, y):
    T = x.shape[0]
    tm = _tiles(T)
    smem = pl.BlockSpec((tm,), lambda i: (i,), memory_space=pltpu.SMEM)
    return pl.pallas_call(
        _combine_kernel,
        grid=(T // tm,),
        in_specs=[smem, smem, pl.BlockSpec((tm, D_MODEL), lambda i: (i, 0)),
                  pl.BlockSpec((tm, LANES), lambda i: (i, 0)),
                  pl.BlockSpec((1, D_MODEL), lambda i: (0, 0)),
                  pl.BlockSpec(memory_space=pl.ANY)],
        out_specs=pl.BlockSpec((tm, D_MODEL), lambda i: (i, 0)),
        out_shape=jax.ShapeDtypeStruct((T, D_MODEL), F32),
        scratch_shapes=[pltpu.VMEM((tm, D_MODEL), F32), pltpu.VMEM((tm, D_MODEL), F32),
                        pltpu.SemaphoreType.DMA],
        compiler_params=_cparams(("arbitrary",)),
        name="moe_combine",
    )(dest0, dest1, x, info, g_final, y)


def _pad_heads_cols(w, heads, d, d_pad):
    k = w.shape[0]
    return jnp.pad(w.reshape(k, heads, d), ((0, 0), (0, 0), (0, d_pad - d))).reshape(k, heads * d_pad)


def _rope_tables(S):
    half = HEAD_DIM // 2
    inv = ROPE_THETA ** (-jnp.arange(half, dtype=F32) / half)
    ang = jnp.arange(S).astype(F32)[:, None] * inv[None, :]
    reps = LANES // half
    cos = jnp.tile(jnp.cos(ang), (1, reps))
    sign = jnp.tile(jnp.concatenate([-jnp.ones((half,), F32), jnp.ones((half,), F32)]),
                    LANES // HEAD_DIM)
    sin = jnp.tile(jnp.sin(ang), (1, reps)) * sign[None, :]
    return cos, sin


def _moe_block_rows(T):
    return 1024 if (2 * T) % 1024 == 0 else 256


def kernel(x, mem, g_mix, g_ffn, w_mem_kv, w_out, w_in_a, w_gk_up, b_gk, g_gla, g_kv, w_kv, w_in_b, lambda_q1, lambda_k1, lambda_q2, lambda_k2, g_subln, w_dense_gate, w_dense_up, w_dense_down, w_router, w_exp_gate, w_exp_up, w_exp_down, g_final):
    B, S, D = x.shape
    T = B * S
    x0 = x.reshape(T, D)
    mem2 = mem.reshape(B * MEM_LEN, D)
    row = lambda v: v.reshape(1, -1)

    wa = w_in_a[0]
    s0, s1, s2, s3, s4 = (GLA_KDIM, 2 * GLA_KDIM, 2 * GLA_KDIM + MIX_DIM, 2 * GLA_KDIM + 2 * MIX_DIM,
                          2 * GLA_KDIM + 2 * MIX_DIM + GLA_GATE_RANK)
    wq = _pad_heads_cols(wa[:, :s0], GLA_HEADS, GLA_DK, GLA_DK_PAD).astype(BF16)
    wk = _pad_heads_cols(wa[:, s0:s1], GLA_HEADS, GLA_DK, GLA_DK_PAD).astype(BF16)
    wv = _pad_heads_cols(wa[:, s1:s2], GLA_HEADS, GLA_DV, GLA_DV_PAD).astype(BF16)
    wgo = _pad_heads_cols(wa[:, s2:s3], GLA_HEADS, GLA_DV, GLA_DV_PAD).astype(BF16)
    wgl = jnp.pad(wa[:, s3:s4], ((0, 0), (0, LANES - GLA_GATE_RANK))).astype(BF16)
    wqm = wa[:, s4:].astype(BF16)
    wup = jnp.pad(_pad_heads_cols(w_gk_up[0], GLA_HEADS, GLA_DK, GLA_DK_PAD),
                  ((0, LANES - GLA_GATE_RANK), (0, 0))).astype(BF16)
    bgk = _pad_heads_cols(row(b_gk[0]), GLA_HEADS, GLA_DK, GLA_DK_PAD)
    ggla = jnp.pad(row(g_gla[0]), ((0, 0), (0, GLA_DV_PAD - GLA_DV)))
    wo_a = jnp.pad(w_out[0][:MIX_DIM].reshape(GLA_HEADS, GLA_DV, D),
                   ((0, 0), (0, GLA_DV_PAD - GLA_DV), (0, 0))).reshape(GLA_HEADS * GLA_DV_PAD, D)

    q, k, v, go, qm, gk = _in_proj_a(x0, row(g_mix[0]), wq, wk, wv, wgo, wqm, wgl, wup, bgk)
    o = _gla(q, k, gk, v, go, ggla, B, S)
    kv_mem0 = _matmul(mem2, w_mem_kv[0].astype(BF16), BF16).reshape(B, MEM_LEN, 2 * MEM_DIM)
    m = _mem_attn(qm, kv_mem0, B, S)
    x1 = _out_proj(x0, o, m, wo_a.astype(BF16), w_out[0][MIX_DIM:].astype(BF16))
    x2 = _ffn(x1, row(g_ffn[0]), w_dense_gate[0].astype(BF16), w_dense_up[0].astype(BF16),
              w_dense_down[0].astype(BF16))

    cos, sin = _rope_tables(S)
    k_sh, vt_sh = _proj_rope(x2, row(g_kv), w_kv[:, :MIX_DIM].astype(BF16),
                             w_kv[:, MIX_DIM:].astype(BF16), cos, sin, S, 1.0, True)

    q1, qm1 = _proj_rope(x2, row(g_mix[1]), w_in_b[0][:, :MIX_DIM].astype(BF16),
                         w_in_b[0][:, MIX_DIM:].astype(BF16), cos, sin, S,
                         HEAD_DIM ** -0.5 * LOG2_E, False)
    lam_rows = jnp.pad(jnp.stack([lambda_q1[0], lambda_k1[0], lambda_q2[0], lambda_k2[0]]),
                       ((0, SUBLANES - 4), (0, LANES - HEAD_DIM)))
    o1 = _diff_attn(q1, k_sh, vt_sh, lam_rows, row(g_subln[0]), B, S)
    kv_mem1 = _matmul(mem2, w_mem_kv[1].astype(BF16), BF16).reshape(B, MEM_LEN, 2 * MEM_DIM)
    m1 = _mem_attn(qm1, kv_mem1, B, S)
    x3 = _out_proj(x2, o1, m1, w_out[1][:MIX_DIM].astype(BF16), w_out[1][MIX_DIM:].astype(BF16))

    wr = jnp.pad(w_router[0], ((0, 0), (0, LANES - N_EXPERTS)))
    h, info, cnt = _router(x3, row(g_ffn[1]), wr)

    rb = _moe_block_rows(T)
    counts = cnt[0, :N_EXPERTS].astype(jnp.int32)
    padded = (counts + rb - 1) // rb * rb
    pend = jnp.cumsum(padded)
    pstart = pend - padded
    e0 = info[:, _R_E0].astype(jnp.int32)
    e1 = info[:, _R_E1].astype(jnp.int32)
    eids = jnp.arange(N_EXPERTS, dtype=jnp.int32)
    start_of = lambda e: jnp.sum(jnp.where(e[:, None] == eids[None, :], pstart[None, :], 0), axis=1)
    dest0 = start_of(e0) + info[:, _R_RANK0].astype(jnp.int32)
    dest1 = start_of(e1) + info[:, _R_RANK1].astype(jnp.int32)
    nb = (2 * T) // rb + N_EXPERTS
    block_expert = jnp.clip(jnp.sum(pend[None, :] <= (jnp.arange(nb, dtype=jnp.int32) * rb)[:, None],
                                    axis=1), 0, N_EXPERTS - 1).astype(jnp.int32)
    n_used = (pend[-1:] // rb).astype(jnp.int32)

    xs = _dispatch(dest0, dest1, h, jnp.zeros((nb * rb, D), F32))
    ys = _experts(block_expert, n_used, xs, w_exp_gate[0].astype(BF16), w_exp_up[0].astype(BF16),
                  w_exp_down[0].astype(BF16), rb)
    out = _combine(dest0, dest1, x3, info, row(g_final), ys)
    return out.reshape(B, S, D)
```

```python
import functools
import math

import jax
import jax.numpy as jnp
from jax import lax
from jax.experimental import pallas as pl
from jax.experimental.pallas import tpu as pltpu

D_MODEL = 1024
MEM_LEN = 256
HEAD_DIM = 64
MEM_HEADS = 4
MEM_DIM = MEM_HEADS * HEAD_DIM
MIX_DIM = D_MODEL - MEM_DIM
GLA_HEADS = 4
GLA_DV = MIX_DIM // GLA_HEADS
GLA_DK = GLA_DV // 2
GLA_KDIM = GLA_HEADS * GLA_DK
GLA_GATE_RANK = 16
GLA_GATE_NORM = 16.0
GLA_CHUNK = 64
DIFF_HEADS = MIX_DIM // (2 * HEAD_DIM)
DIFF_VDIM = 2 * HEAD_DIM
ROPE_THETA = 10000.0
FFN_DENSE = 2816
N_EXPERTS = 8
FFN_EXPERT = 3584
EPS = 1e-6
LAMBDA_INIT_L1 = 0.8 - 0.6 * math.exp(-0.3 * 1)

LANES = 128
SUBLANES = 8
VMEM_LIMIT_BYTES = 52 * 1024 * 1024

GLA_DK_PAD = LANES
GLA_DV_PAD = 2 * LANES
DIFF_TQ = 512
ROW_DMA_UNROLL = 8
DIFF_TK = 512
DIFF_ONES_ROWS = 16
LOG2_E = 1.4426950408889634

F32 = jnp.float32
BF16 = jnp.bfloat16
_NT = (((1,), (1,)), ((), ()))
_TN = (((0,), (0,)), ((), ()))


def _tiles(T):
    tm = 512 if T % 512 == 0 else T
    return tm


def _cparams(sem):
    return pltpu.CompilerParams(dimension_semantics=sem, vmem_limit_bytes=VMEM_LIMIT_BYTES)


def _rms(xf, g):
    y = xf * lax.rsqrt(jnp.mean(xf * xf, axis=-1, keepdims=True) + EPS)
    return y * g


def _silu(a):
    return a / (1.0 + jnp.exp(-a))


def _dot(a, b):
    return jnp.dot(a, b, preferred_element_type=F32)


def _in_proj_a_kernel(x_ref, g_ref, wq_ref, wk_ref, wv_ref, wgo_ref, wqm_ref, wgl_ref, wup_ref,
                      bgk_ref, q_ref, k_ref, v_ref, go_ref, qm_ref, gk_ref):
    h = _rms(x_ref[...], g_ref[...]).astype(BF16)
    q_ref[...] = _dot(h, wq_ref[...])
    k_ref[...] = _dot(h, wk_ref[...])
    v_ref[...] = _dot(h, wv_ref[...]).astype(BF16)
    go_ref[...] = _dot(h, wgo_ref[...]).astype(BF16)
    qm_ref[...] = _dot(h, wqm_ref[...]).astype(BF16)
    low = _dot(h, wgl_ref[...]).astype(BF16)
    z = _dot(low, wup_ref[...]) + bgk_ref[...]
    log_sig = jnp.minimum(z, 0.0) - jnp.log1p(jnp.exp(-jnp.abs(z)))
    gk_ref[...] = log_sig / GLA_GATE_NORM


def _in_proj_a(x, g, wq, wk, wv, wgo, wqm, wgl, wup, bgk):
    T = x.shape[0]
    tm = _tiles(T)
    row = lambda n: pl.BlockSpec((tm, n), lambda i: (i, 0))
    full = lambda a: pl.BlockSpec(a.shape, lambda i: (0, 0))
    kq, kv_, km = GLA_HEADS * GLA_DK_PAD, GLA_HEADS * GLA_DV_PAD, MEM_DIM
    return pl.pallas_call(
        _in_proj_a_kernel,
        grid=(T // tm,),
        in_specs=[row(D_MODEL)] + [full(a) for a in (g, wq, wk, wv, wgo, wqm, wgl, wup, bgk)],
        out_specs=[row(kq), row(kq), row(kv_), row(kv_), row(km), row(kq)],
        out_shape=[jax.ShapeDtypeStruct((T, kq), F32), jax.ShapeDtypeStruct((T, kq), F32),
                   jax.ShapeDtypeStruct((T, kv_), BF16), jax.ShapeDtypeStruct((T, kv_), BF16),
                   jax.ShapeDtypeStruct((T, km), BF16), jax.ShapeDtypeStruct((T, kq), F32)],
        compiler_params=_cparams(("parallel",)),
        name="in_proj_a",
    )(x, g, wq, wk, wv, wgo, wqm, wgl, wup, bgk)


def _gla_kernel(q_ref, k_ref, gk_ref, v_ref, go_ref, g_ref, o_ref, st_ref, *, n_chunks):
    C = GLA_CHUNK

    @pl.when(pl.program_id(2) == 0)
    def _():
        st_ref[...] = jnp.zeros_like(st_ref)

    r = lax.broadcasted_iota(jnp.int32, (C, C), 0)
    c = lax.broadcasted_iota(jnp.int32, (C, C), 1)
    tril = c <= r
    tril_f = tril.astype(F32)
    g = g_ref[...]
    for ci in range(n_chunks):
        sl = pl.ds(ci * C, C)
        b = jnp.dot(tril_f, gk_ref[sl, :], precision=lax.Precision.HIGHEST,
                    preferred_element_type=F32)
        b_mid = b[C // 2 - 1:C // 2, :]
        b_last = b[C - 1:C, :]
        qc = q_ref[sl, :] * (GLA_DK ** -0.5)
        kc = k_ref[sl, :]
        vc = v_ref[sl, :]
        st = st_ref[...]
        q_in = (qc * jnp.exp(b - b_mid)).astype(BF16)
        k_in = (kc * jnp.exp(b_mid - b)).astype(BF16)
        sc = lax.dot_general(q_in, k_in, _NT, preferred_element_type=F32)
        sc = jnp.where(tril, sc, 0.0)
        o = _dot(sc.astype(BF16), vc)
        q_st = (qc * jnp.exp(b)).astype(BF16)
        o = o + lax.dot_general(q_st, st.astype(BF16), _NT, preferred_element_type=F32)
        k_st = (kc * jnp.exp(b_last - b)).astype(BF16)
        kv_t = lax.dot_general(vc, k_st, _TN, preferred_element_type=F32)
        st_ref[...] = st * jnp.exp(b_last) + kv_t
        ms = jnp.sum(o * o, axis=-1, keepdims=True) * (1.0 / GLA_DV)
        on = o * lax.rsqrt(ms + EPS) * g
        go = go_ref[sl, :].astype(F32)
        o_ref[sl, :] = (on * _silu(go)).astype(BF16)


def _gla(q, k, gk, v, go, g_gla, B, S):
    blk = 512 if S % 512 == 0 else S
    n_s = S // blk
    kmap = lambda b, h, s: (b * n_s + s, h)
    return pl.pallas_call(
        functools.partial(_gla_kernel, n_chunks=blk // GLA_CHUNK),
        grid=(B, GLA_HEADS, n_s),
        in_specs=[pl.BlockSpec((blk, GLA_DK_PAD), kmap), pl.BlockSpec((blk, GLA_DK_PAD), kmap),
                  pl.BlockSpec((blk, GLA_DK_PAD), kmap), pl.BlockSpec((blk, GLA_DV_PAD), kmap),
                  pl.BlockSpec((blk, GLA_DV_PAD), kmap),
                  pl.BlockSpec((1, GLA_DV_PAD), lambda b, h, s: (0, 0))],
        out_specs=pl.BlockSpec((blk, GLA_DV_PAD), kmap),
        out_shape=jax.ShapeDtypeStruct((B * S, GLA_HEADS * GLA_DV_PAD), BF16),
        scratch_shapes=[pltpu.VMEM((GLA_DV_PAD, GLA_DK_PAD), F32)],
        compiler_params=_cparams(("parallel", "parallel", "arbitrary")),
        name="gla",
    )(q, k, gk, v, go, g_gla)


def _matmul_kernel(x_ref, w_ref, o_ref):
    o_ref[...] = _dot(x_ref[...].astype(BF16), w_ref[...]).astype(o_ref.dtype)


def _matmul(x, w, out_dtype):
    M, K = x.shape
    N = w.shape[1]
    tm = 512 if M % 512 == 0 else M
    return pl.pallas_call(
        _matmul_kernel,
        grid=(M // tm,),
        in_specs=[pl.BlockSpec((tm, K), lambda i: (i, 0)), pl.BlockSpec((K, N), lambda i: (0, 0))],
        out_specs=pl.BlockSpec((tm, N), lambda i: (i, 0)),
        out_shape=jax.ShapeDtypeStruct((M, N), out_dtype),
        compiler_params=_cparams(("parallel",)),
        name="mem_kv_proj",
    )(x, w)


def _mem_attn_kernel(q_ref, kv_ref, o_ref):
    q = q_ref[...]
    kv = kv_ref[0]
    kmat = kv[:, :MEM_DIM]
    vmat = kv[:, MEM_DIM:]
    lane = lax.broadcasted_iota(jnp.int32, kmat.shape, 1)
    acc = jnp.zeros((q.shape[0], MEM_DIM), F32)
    for h in range(MEM_HEADS):
        in_head = (lane // HEAD_DIM) == h
        kh = jnp.where(in_head, kmat, jnp.zeros_like(kmat))
        vh = jnp.where(in_head, vmat, jnp.zeros_like(vmat))
        s = lax.dot_general(q, kh, _NT, preferred_element_type=F32) * (HEAD_DIM ** -0.5)
        p = jnp.exp(s - jnp.max(s, axis=-1, keepdims=True))
        p = p / jnp.sum(p, axis=-1, keepdims=True)
        acc = acc + _dot(p.astype(BF16), vh)
    o_ref[...] = acc.astype(BF16)


def _mem_attn(qm, kv_mem, B, S):
    tm = 512 if S % 512 == 0 else S
    n_s = S // tm
    return pl.pallas_call(
        _mem_attn_kernel,
        grid=(B, n_s),
        in_specs=[pl.BlockSpec((tm, MEM_DIM), lambda b, s: (b * n_s + s, 0)),
                  pl.BlockSpec((1, MEM_LEN, 2 * MEM_DIM), lambda b, s: (b, 0, 0))],
        out_specs=pl.BlockSpec((tm, MEM_DIM), lambda b, s: (b * n_s + s, 0)),
        out_shape=jax.ShapeDtypeStruct((B * S, MEM_DIM), BF16),
        compiler_params=_cparams(("parallel", "parallel")),
        name="mem_attn",
    )(qm, kv_mem)


def _out_proj_kernel(x_ref, o_ref, m_ref, w1_ref, w2_ref, y_ref):
    y_ref[...] = x_ref[...] + _dot(o_ref[...], w1_ref[...]) + _dot(m_ref[...], w2_ref[...])


def _out_proj(x, o, m, w1, w2):
    T = x.shape[0]
    tm = _tiles(T)
    row = lambda n: pl.BlockSpec((tm, n), lambda i: (i, 0))
    full = lambda a: pl.BlockSpec(a.shape, lambda i: (0, 0))
    return pl.pallas_call(
        _out_proj_kernel,
        grid=(T // tm,),
        in_specs=[row(D_MODEL), row(o.shape[1]), row(MEM_DIM), full(w1), full(w2)],
        out_specs=row(D_MODEL),
        out_shape=jax.ShapeDtypeStruct((T, D_MODEL), F32),
        compiler_params=_cparams(("parallel",)),
        name="out_proj",
    )(x, o, m, w1, w2)


def _ffn_kernel(x_ref, g_ref, wg_ref, wu_ref, wd_ref, y_ref, h_sc, acc_sc):
    f = pl.program_id(1)

    @pl.when(f == 0)
    def _():
        h_sc[...] = _rms(x_ref[...], g_ref[...]).astype(BF16)
        acc_sc[...] = jnp.zeros_like(acc_sc)

    h = h_sc[...]
    act = (_silu(_dot(h, wg_ref[...])) * _dot(h, wu_ref[...])).astype(BF16)
    acc_sc[...] += _dot(act, wd_ref[...])

    @pl.when(f == pl.num_programs(1) - 1)
    def _():
        y_ref[...] = x_ref[...] + acc_sc[...]


def _ffn(x, g, wg, wu, wd):
    T = x.shape[0]
    tm = _tiles(T)
    F = wg.shape[1]
    tf = F // 2
    return pl.pallas_call(
        _ffn_kernel,
        grid=(T // tm, F // tf),
        in_specs=[pl.BlockSpec((tm, D_MODEL), lambda i, f: (i, 0)),
                  pl.BlockSpec((1, D_MODEL), lambda i, f: (0, 0)),
                  pl.BlockSpec((D_MODEL, tf), lambda i, f: (0, f)),
                  pl.BlockSpec((D_MODEL, tf), lambda i, f: (0, f)),
                  pl.BlockSpec((tf, D_MODEL), lambda i, f: (f, 0))],
        out_specs=pl.BlockSpec((tm, D_MODEL), lambda i, f: (i, 0)),
        out_shape=jax.ShapeDtypeStruct((T, D_MODEL), F32),
        scratch_shapes=[pltpu.VMEM((tm, D_MODEL), BF16), pltpu.VMEM((tm, D_MODEL), F32)],
        compiler_params=_cparams(("parallel", "arbitrary")),
        name="dense_ffn",
    )(x, g, wg, wu, wd)


def _proj_rope_kernel(x_ref, g_ref, wr_ref, wp_ref, cos_ref, sin_ref, r_ref, p_ref, *, scale,
                      transpose_pass):
    h = _rms(x_ref[...], g_ref[...]).astype(BF16)
    p = _dot(h, wp_ref[...])
    if transpose_pass:
        pt = p.T.astype(BF16)
        rows = DIFF_VDIM + DIFF_ONES_ROWS
        for hh in range(p.shape[1] // DIFF_VDIM):
            p_ref[0, hh * rows:hh * rows + DIFF_VDIM, :] = pt[hh * DIFF_VDIM:(hh + 1) * DIFF_VDIM, :]
            p_ref[0, hh * rows + DIFF_VDIM:(hh + 1) * rows, :] = jnp.ones(
                (DIFF_ONES_ROWS, pt.shape[1]), BF16)
    else:
        p_ref[...] = p.astype(BF16)
    cos = cos_ref[...]
    sin = sin_ref[...]
    lane = lax.broadcasted_iota(jnp.int32, cos.shape, 1)
    first_half = (lane % HEAD_DIM) < (HEAD_DIM // 2)
    y_all = _dot(h, wr_ref[...])
    for j in range(wr_ref.shape[1] // LANES):
        sl = slice(j * LANES, (j + 1) * LANES)
        y = y_all[:, sl]
        partner = jnp.where(first_half, pltpu.roll(y, LANES - HEAD_DIM // 2, 1),
                            pltpu.roll(y, HEAD_DIM // 2, 1))
        r_ref[:, sl] = ((y * cos + partner * sin) * scale).astype(BF16)


def _proj_rope(x, g, w_rope, w_pass, cos, sin, S, scale, transpose_pass):
    T = x.shape[0]
    tm = DIFF_TK if S % DIFF_TK == 0 else S
    n_s = S // tm
    row = lambda n: pl.BlockSpec((tm, n), lambda i: (i, 0))
    full = lambda a: pl.BlockSpec(a.shape, lambda i: (0, 0))
    tab = pl.BlockSpec((tm, LANES), lambda i: (i % n_s, 0))
    nr, np_ = w_rope.shape[1], w_pass.shape[1]
    if transpose_pass:
        nt = np_ // DIFF_VDIM * (DIFF_VDIM + DIFF_ONES_ROWS)
        pass_spec = pl.BlockSpec((1, nt, tm), lambda i: (i, 0, 0))
        pass_shape = jax.ShapeDtypeStruct((T // tm, nt, tm), BF16)
    else:
        pass_spec = row(np_)
        pass_shape = jax.ShapeDtypeStruct((T, np_), BF16)
    return pl.pallas_call(
        functools.partial(_proj_rope_kernel, scale=scale, transpose_pass=transpose_pass),
        grid=(T // tm,),
        in_specs=[row(D_MODEL), full(g), full(w_rope), full(w_pass), tab, tab],
        out_specs=[row(nr), pass_spec],
        out_shape=[jax.ShapeDtypeStruct((T, nr), BF16), pass_shape],
        compiler_params=_cparams(("parallel",)),
        name="proj_rope",
    )(x, g, w_rope, w_pass, cos, sin)


def _diff_attn_kernel(q_ref, k_ref, vt_ref, lam_ref, g_ref, o_ref, qs_sc, m_sc, acc_sc, s0_sc, s1_sc,
                      *, tq, tk):
    i = pl.program_id(2)
    q = q_ref[...]
    lane = lax.broadcasted_iota(jnp.int32, q.shape, 1)
    qs_sc[0:tq, :] = jnp.where(lane < HEAD_DIM, q, jnp.zeros_like(q))
    qs_sc[tq:2 * tq, :] = jnp.where(lane >= HEAD_DIM, q, jnp.zeros_like(q))
    m_sc[...] = jnp.full_like(m_sc, -1e30)
    acc_sc[...] = jnp.zeros_like(acc_sc)

    def scores(j, s_ref):
        kj = k_ref[pl.ds(pl.multiple_of(j * tk, tk), tk), :]
        s_ref[...] = lax.dot_general(kj, qs_sc[...], _NT, preferred_element_type=F32)

    def update(j, s_ref, masked):
        s = s_ref[...]
        if masked:
            kpos = j * tk + lax.broadcasted_iota(jnp.int32, s.shape, 0)
            qpos = i * tq + lax.broadcasted_iota(jnp.int32, s.shape, 1) % tq
            s = jnp.where(kpos <= qpos, s, -1e30)
        m_prev = m_sc[...]
        m_new = jnp.maximum(m_prev, jnp.max(s, axis=0, keepdims=True))
        alpha = jnp.exp2(m_prev - m_new)
        p = jnp.exp2(s - m_new).astype(BF16)
        acc_sc[...] = alpha * acc_sc[...] + _dot(vt_ref[j], p)
        m_sc[...] = m_new

    n_full = (i * tq) // tk
    scores(0, s0_sc)

    def body(jj, carry):
        a = 2 * jj
        scores(a + 1, s1_sc)
        update(a, s0_sc, False)
        scores(a + 2, s0_sc)
        update(a + 1, s1_sc, False)
        return carry

    lax.fori_loop(0, n_full // 2, body, 0)

    @pl.when(n_full % 2 == 1)
    def _():
        scores(n_full, s1_sc)
        update(n_full - 1, s0_sc, False)
        update(n_full, s1_sc, True)

    @pl.when(n_full % 2 == 0)
    def _():
        update(n_full, s0_sc, True)

    lam_rows = lam_ref[...]
    dot1 = jnp.sum(lam_rows[0:1, :] * lam_rows[1:2, :], axis=-1, keepdims=True)
    dot2 = jnp.sum(lam_rows[2:3, :] * lam_rows[3:4, :], axis=-1, keepdims=True)
    lam = jnp.exp(dot1) - jnp.exp(dot2) + LAMBDA_INIT_L1
    dv = DIFF_VDIM
    ot = (acc_sc[0:dv, 0:tq] / acc_sc[dv:dv + 1, 0:tq]
          - lam * (acc_sc[0:dv, tq:2 * tq] / acc_sc[dv:dv + 1, tq:2 * tq]))
    ot = ot * lax.rsqrt(jnp.mean(ot * ot, axis=0, keepdims=True) + EPS)
    o_ref[...] = ((ot.T * g_ref[...]) * (1.0 - LAMBDA_INIT_L1)).astype(BF16)


def _diff_attn(q, k, vt, lam_rows, g_subln, B, S):
    tk = vt.shape[2]
    tq = DIFF_TQ if S % DIFF_TQ == 0 else S
    assert tq <= tk and tk % tq == 0 and S % tk == 0
    n_q = S // tq
    n_k = S // tk
    rows = DIFF_VDIM + DIFF_ONES_ROWS
    return pl.pallas_call(
        functools.partial(_diff_attn_kernel, tq=tq, tk=tk),
        grid=(B, DIFF_HEADS, n_q),
        in_specs=[pl.BlockSpec((tq, DIFF_VDIM), lambda b, h, i: (b * n_q + i, h)),
                  pl.BlockSpec((S, DIFF_VDIM), lambda b, h, i: (b, h)),
                  pl.BlockSpec((n_k, rows, tk), lambda b, h, i: (b, h, 0)),
                  pl.BlockSpec(lam_rows.shape, lambda b, h, i: (0, 0)),
                  pl.BlockSpec((1, DIFF_VDIM), lambda b, h, i: (0, 0))],
        out_specs=pl.BlockSpec((tq, DIFF_VDIM), lambda b, h, i: (b * n_q + i, h)),
        out_shape=jax.ShapeDtypeStruct((B * S, MIX_DIM), BF16),
        scratch_shapes=[pltpu.VMEM((2 * tq, DIFF_VDIM), BF16), pltpu.VMEM((1, 2 * tq), F32),
                        pltpu.VMEM((rows, 2 * tq), F32), pltpu.VMEM((tk, 2 * tq), F32),
                        pltpu.VMEM((tk, 2 * tq), F32)],
        compiler_params=_cparams(("parallel", "parallel", "arbitrary")),
        name="diff_attn",
    )(q, k, vt, lam_rows, g_subln)


_R_E0, _R_E1, _R_RANK0, _R_RANK1, _R_G0, _R_G1 = range(6)


def _router_kernel(x_ref, g_ref, wr_ref, h_ref, info_ref, cnt_ref, run_sc):
    @pl.when(pl.program_id(0) == 0)
    def _():
        run_sc[...] = jnp.zeros_like(run_sc)

    h = _rms(x_ref[...], g_ref[...])
    h_ref[...] = h
    logits = jnp.dot(h, wr_ref[...], precision=lax.Precision.HIGHEST, preferred_element_type=F32)
    tm = logits.shape[0]
    lane = lax.broadcasted_iota(jnp.int32, logits.shape, 1).astype(F32)
    neg = -jnp.inf
    lg = jnp.where(lane < N_EXPERTS, logits, neg)
    m1 = jnp.max(lg, axis=-1, keepdims=True)
    i1 = jnp.min(jnp.where(lg == m1, lane, float(LANES)), axis=-1, keepdims=True)
    lg2 = jnp.where(lane == i1, neg, lg)
    m2 = jnp.max(lg2, axis=-1, keepdims=True)
    i2 = jnp.min(jnp.where(lg2 == m2, lane, float(LANES)), axis=-1, keepdims=True)
    e = jnp.exp(m2 - m1)
    g0 = 1.0 / (1.0 + e)
    g1 = e / (1.0 + e)
    onehot = jnp.where((lane == i1) | (lane == i2), 1.0, 0.0)
    r = lax.broadcasted_iota(jnp.int32, (tm, tm), 0)
    c = lax.broadcasted_iota(jnp.int32, (tm, tm), 1)
    before = _dot(jnp.where(c < r, 1.0, 0.0).astype(BF16), onehot.astype(BF16)) + run_sc[0:1, :]
    rank0 = jnp.sum(jnp.where(lane == i1, before, 0.0), axis=-1, keepdims=True)
    rank1 = jnp.sum(jnp.where(lane == i2, before, 0.0), axis=-1, keepdims=True)
    run_sc[...] = run_sc[...] + jnp.sum(onehot, axis=0, keepdims=True)
    info = jnp.zeros_like(logits)
    for idx, val in ((_R_E0, i1), (_R_E1, i2), (_R_RANK0, rank0), (_R_RANK1, rank1),
                     (_R_G0, g0), (_R_G1, g1)):
        info = jnp.where(lane == idx, val, info)
    info_ref[...] = info
    cnt_ref[...] = run_sc[...]


def _router(x, g, wr):
    T = x.shape[0]
    tm = _tiles(T)
    return pl.pallas_call(
        _router_kernel,
        grid=(T // tm,),
        in_specs=[pl.BlockSpec((tm, D_MODEL), lambda i: (i, 0)),
                  pl.BlockSpec((1, D_MODEL), lambda i: (0, 0)),
                  pl.BlockSpec((D_MODEL, LANES), lambda i: (0, 0))],
        out_specs=[pl.BlockSpec((tm, D_MODEL), lambda i: (i, 0)),
                   pl.BlockSpec((tm, LANES), lambda i: (i, 0)),
                   pl.BlockSpec((SUBLANES, LANES), lambda i: (0, 0))],
        out_shape=[jax.ShapeDtypeStruct((T, D_MODEL), F32), jax.ShapeDtypeStruct((T, LANES), F32),
                   jax.ShapeDtypeStruct((SUBLANES, LANES), F32)],
        scratch_shapes=[pltpu.VMEM((SUBLANES, LANES), F32)],
        compiler_params=_cparams(("arbitrary",)),
        name="moe_router",
    )(x, g, wr)


def _row_copy(src, src_row, dst, dst_row, sem):
    return pltpu.make_async_copy(src.at[pl.ds(src_row, 1)], dst.at[pl.ds(dst_row, 1)], sem)


def _dispatch_kernel(d0_ref, d1_ref, h_ref, xs_in_ref, xs_ref, sem):
    del xs_in_ref
    tm = h_ref.shape[0]

    def issue(r, carry):
        _row_copy(h_ref, r, xs_ref, d0_ref[r], sem).start()
        _row_copy(h_ref, r, xs_ref, d1_ref[r], sem).start()
        return carry

    lax.fori_loop(0, tm, issue, 0, unroll=ROW_DMA_UNROLL)

    def drain(r, carry):
        _row_copy(h_ref, 0, xs_ref, 0, sem).wait()
        _row_copy(h_ref, 0, xs_ref, 0, sem).wait()
        return carry

    lax.fori_loop(0, tm, drain, 0, unroll=ROW_DMA_UNROLL)


def _dispatch(dest0, dest1, h, xs_zero):
    T = h.shape[0]
    tm = _tiles(T)
    smem = pl.BlockSpec((tm,), lambda i: (i,), memory_space=pltpu.SMEM)
    return pl.pallas_call(
        _dispatch_kernel,
        grid=(T // tm,),
        in_specs=[smem, smem, pl.BlockSpec((tm, D_MODEL), lambda i: (i, 0)),
                  pl.BlockSpec(memory_space=pl.ANY)],
        out_specs=pl.BlockSpec(memory_space=pl.ANY),
        out_shape=jax.ShapeDtypeStruct(xs_zero.shape, xs_zero.dtype),
        scratch_shapes=[pltpu.SemaphoreType.DMA],
        input_output_aliases={3: 0},
        compiler_params=_cparams(("arbitrary",)),
        name="moe_dispatch",
    )(dest0, dest1, h, xs_zero)


def _expert_kernel(bexp_ref, nused_ref, xs_ref, wg_ref, wu_ref, wd_ref, y_ref, xb_sc, acc_sc):
    del bexp_ref
    j = pl.program_id(0)
    f = pl.program_id(1)

    @pl.when(j < nused_ref[0])
    def _():
        @pl.when(f == 0)
        def _():
            xb_sc[...] = xs_ref[...].astype(BF16)
            acc_sc[...] = jnp.zeros_like(acc_sc)

        xb = xb_sc[...]
        act = (_silu(_dot(xb, wg_ref[0])) * _dot(xb, wu_ref[0])).astype(BF16)
        acc_sc[...] += _dot(act, wd_ref[0])

        @pl.when(f == pl.num_programs(1) - 1)
        def _():
            y_ref[...] = acc_sc[...]

    @pl.when((j >= nused_ref[0]) & (f == 0))
    def _():
        y_ref[...] = jnp.zeros_like(y_ref)


def _experts(block_expert, n_used, xs, wg, wu, wd, rb):
    L = xs.shape[0]
    nb = L // rb
    tf = 512
    nf = FFN_EXPERT // tf

    def blk(j, f, be, nu):
        return jnp.minimum(j, nu[0] - 1)

    def ftile(j, f, be, nu):
        return jnp.where(j < nu[0], f, nf - 1)

    return pl.pallas_call(
        _expert_kernel,
        grid_spec=pltpu.PrefetchScalarGridSpec(
            num_scalar_prefetch=2,
            grid=(nb, nf),
            in_specs=[
                pl.BlockSpec((rb, D_MODEL), lambda j, f, be, nu: (blk(j, f, be, nu), 0)),
                pl.BlockSpec((1, D_MODEL, tf),
                             lambda j, f, be, nu: (be[blk(j, f, be, nu)], 0, ftile(j, f, be, nu))),
                pl.BlockSpec((1, D_MODEL, tf),
                             lambda j, f, be, nu: (be[blk(j, f, be, nu)], 0, ftile(j, f, be, nu))),
                pl.BlockSpec((1, tf, D_MODEL),
                             lambda j, f, be, nu: (be[blk(j, f, be, nu)], ftile(j, f, be, nu), 0)),
            ],
            out_specs=pl.BlockSpec((rb, D_MODEL), lambda j, f, be, nu: (j, 0)),
            scratch_shapes=[pltpu.VMEM((rb, D_MODEL), BF16), pltpu.VMEM((rb, D_MODEL), F32)],
        ),
        out_shape=jax.ShapeDtypeStruct((L, D_MODEL), F32),
        compiler_params=_cparams(("arbitrary", "arbitrary")),
        name="moe_experts",
    )(block_expert, n_used, xs, wg, wu, wd)


def _combine_kernel(d0_ref, d1_ref, x_ref, info_ref, g_ref, y_ref, o_ref, b0_sc, b1_sc, sem):
    tm = x_ref.shape[0]

    def issue(r, carry):
        _row_copy(y_ref, d0_ref[r], b0_sc, r, sem).start()
        _row_copy(y_ref, d1_ref[r], b1_sc, r, sem).start()
        return carry

    lax.fori_loop(0, tm, issue, 0, unroll=ROW_DMA_UNROLL)

    def drain(r, carry):
        _row_copy(y_ref, 0, b0_sc, 0, sem).wait()
        _row_copy(y_ref, 0, b1_sc, 0, sem).wait()
        return carry

    lax.fori_loop(0, tm, drain, 0, unroll=ROW_DMA_UNROLL)
    info = info_ref[...]
    g0 = info[:, _R_G0:_R_G0 + 1]
    g1 = info[:, _R_G1:_R_G1 + 1]
    x = x_ref[...] + (b0_sc[...] * g0 + b1_sc[...] * g1)
    o_ref[...] = _rms(x, g_ref[...])


def _combine(dest0, dest1, x, info, g_final, y):
    T = x.shape[0]
    tm = _tiles(T)
    smem = pl.BlockSpec((tm,), lambda i: (i,), memory_space=pltpu.SMEM)
    return pl.pallas_call(
        _combine_kernel,
        grid=(T // tm,),
        in_specs=[smem, smem, pl.BlockSpec((tm, D_MODEL), lambda i: (i, 0)),
                  pl.BlockSpec((tm, LANES), lambda i: (i, 0)),
                  pl.BlockSpec((1, D_MODEL), lambda i: (0, 0)),
                  pl.BlockSpec(memory_space=pl.ANY)],
        out_specs=pl.BlockSpec((tm, D_MODEL), lambda i: (i, 0)),
        out_shape=jax.ShapeDtypeStruct((T, D_MODEL), F32),
        scratch_shapes=[pltpu.VMEM((tm, D_MODEL), F32), pltpu.VMEM((tm, D_MODEL), F32),
                        pltpu.SemaphoreType.DMA],
        compiler_params=_cparams(("arbitrary",)),
        name="moe_combine",
    )(dest0, dest1, x, info, g_final, y)


def _pad_heads_cols(w, heads, d, d_pad):
    k = w.shape[0]
    return jnp.pad(w.reshape(k, heads, d), ((0, 0), (0, 0), (0, d_pad - d))).reshape(k, heads * d_pad)


def _rope_tables(S):
    half = HEAD_DIM // 2
    inv = ROPE_THETA ** (-jnp.arange(half, dtype=F32) / half)
    ang = jnp.arange(S).astype(F32)[:, None] * inv[None, :]
    reps = LANES // half
    cos = jnp.tile(jnp.cos(ang), (1, reps))
    sign = jnp.tile(jnp.concatenate([-jnp.ones((half,), F32), jnp.ones((half,), F32)]),
                    LANES // HEAD_DIM)
    sin = jnp.tile(jnp.sin(ang), (1, reps)) * sign[None, :]
    return cos, sin


def _moe_block_rows(T):
    return 1024 if (2 * T) % 1024 == 0 else 256


def kernel(x, mem, g_mix, g_ffn, w_mem_kv, w_out, w_in_a, w_gk_up, b_gk, g_gla, g_kv, w_kv, w_in_b, lambda_q1, lambda_k1, lambda_q2, lambda_k2, g_subln, w_dense_gate, w_dense_up, w_dense_down, w_router, w_exp_gate, w_exp_up, w_exp_down, g_final):
    B, S, D = x.shape
    T = B * S
    x0 = x.reshape(T, D)
    mem2 = mem.reshape(B * MEM_LEN, D)
    row = lambda v: v.reshape(1, -1)

    wa = w_in_a[0]
    s0, s1, s2, s3, s4 = (GLA_KDIM, 2 * GLA_KDIM, 2 * GLA_KDIM + MIX_DIM, 2 * GLA_KDIM + 2 * MIX_DIM,
                          2 * GLA_KDIM + 2 * MIX_DIM + GLA_GATE_RANK)
    wq = _pad_heads_cols(wa[:, :s0], GLA_HEADS, GLA_DK, GLA_DK_PAD).astype(BF16)
    wk = _pad_heads_cols(wa[:, s0:s1], GLA_HEADS, GLA_DK, GLA_DK_PAD).astype(BF16)
    wv = _pad_heads_cols(wa[:, s1:s2], GLA_HEADS, GLA_DV, GLA_DV_PAD).astype(BF16)
    wgo = _pad_heads_cols(wa[:, s2:s3], GLA_HEADS, GLA_DV, GLA_DV_PAD).astype(BF16)
    wgl = jnp.pad(wa[:, s3:s4], ((0, 0), (0, LANES - GLA_GATE_RANK))).astype(BF16)
    wqm = wa[:, s4:].astype(BF16)
    wup = jnp.pad(_pad_heads_cols(w_gk_up[0], GLA_HEADS, GLA_DK, GLA_DK_PAD),
                  ((0, LANES - GLA_GATE_RANK), (0, 0))).astype(BF16)
    bgk = _pad_heads_cols(row(b_gk[0]), GLA_HEADS, GLA_DK, GLA_DK_PAD)
    ggla = jnp.pad(row(g_gla[0]), ((0, 0), (0, GLA_DV_PAD - GLA_DV)))
    wo_a = jnp.pad(w_out[0][:MIX_DIM].reshape(GLA_HEADS, GLA_DV, D),
                   ((0, 0), (0, GLA_DV_PAD - GLA_DV), (0, 0))).reshape(GLA_HEADS * GLA_DV_PAD, D)

    q, k, v, go, qm, gk = _in_proj_a(x0, row(g_mix[0]), wq, wk, wv, wgo, wqm, wgl, wup, bgk)
    o = _gla(q, k, gk, v, go, ggla, B, S)
    kv_mem0 = _matmul(mem2, w_mem_kv[0].astype(BF16), BF16).reshape(B, MEM_LEN, 2 * MEM_DIM)
    m = _mem_attn(qm, kv_mem0, B, S)
    x1 = _out_proj(x0, o, m, wo_a.astype(BF16), w_out[0][MIX_DIM:].astype(BF16))
    x2 = _ffn(x1, row(g_ffn[0]), w_dense_gate[0].astype(BF16), w_dense_up[0].astype(BF16),
              w_dense_down[0].astype(BF16))

    cos, sin = _rope_tables(S)
    k_sh, vt_sh = _proj_rope(x2, row(g_kv), w_kv[:, :MIX_DIM].astype(BF16),
                             w_kv[:, MIX_DIM:].astype(BF16), cos, sin, S, 1.0, True)

    q1, qm1 = _proj_rope(x2, row(g_mix[1]), w_in_b[0][:, :MIX_DIM].astype(BF16),
                         w_in_b[0][:, MIX_DIM:].astype(BF16), cos, sin, S,
                         HEAD_DIM ** -0.5 * LOG2_E, False)
    lam_rows = jnp.pad(jnp.stack([lambda_q1[0], lambda_k1[0], lambda_q2[0], lambda_k2[0]]),
                       ((0, SUBLANES - 4), (0, LANES - HEAD_DIM)))
    o1 = _diff_attn(q1, k_sh, vt_sh, lam_rows, row(g_subln[0]), B, S)
    kv_mem1 = _matmul(mem2, w_mem_kv[1].astype(BF16), BF16).reshape(B, MEM_LEN, 2 * MEM_DIM)
    m1 = _mem_attn(qm1, kv_mem1, B, S)
    x3 = _out_proj(x2, o1, m1, w_out[1][:MIX_DIM].astype(BF16), w_out[1][MIX_DIM:].astype(BF16))

    wr = jnp.pad(w_router[0], ((0, 0), (0, LANES - N_EXPERTS)))
    h, info, cnt = _router(x3, row(g_ffn[1]), wr)

    rb = _moe_block_rows(T)
    counts = cnt[0, :N_EXPERTS].astype(jnp.int32)
    padded = (counts + rb - 1) // rb * rb
    pend = jnp.cumsum(padded)
    pstart = pend - padded
    e0 = info[:, _R_E0].astype(jnp.int32)
    e1 = info[:, _R_E1].astype(jnp.int32)
    eids = jnp.arange(N_EXPERTS, dtype=jnp.int32)
    start_of = lambda e: jnp.sum(jnp.where(e[:, None] == eids[None, :], pstart[None, :], 0), axis=1)
    dest0 = start_of(e0) + info[:, _R_RANK0].astype(jnp.int32)
    dest1 = start_of(e1) + info[:, _R_RANK1].astype(jnp.int32)
    nb = (2 * T) // rb + N_EXPERTS
    block_expert = jnp.clip(jnp.sum(pend[None, :] <= (jnp.arange(nb, dtype=jnp.int32) * rb)[:, None],
                                    axis=1), 0, N_EXPERTS - 1).astype(jnp.int32)
    n_used = (pend[-1:] // rb).astype(jnp.int32)

    xs = _dispatch(dest0, dest1, h, jnp.zeros((nb * rb, D), F32))
    ys = _experts(block_expert, n_used, xs, w_exp_gate[0].astype(BF16), w_exp_up[0].astype(BF16),
                  w_exp_down[0].astype(BF16), rb)
    out = _combine(dest0, dest1, x3, info, row(g_final), ys)
    return out.reshape(B, S, D)
```

```python
import functools
import math

import jax
import jax.numpy as jnp
from jax import lax
from jax.experimental import pallas as pl
from jax.experimental.pallas import tpu as pltpu

D_MODEL = 1024
MEM_LEN = 256
HEAD_DIM = 64
MEM_HEADS = 4
MEM_DIM = MEM_HEADS * HEAD_DIM
MIX_DIM = D_MODEL - MEM_DIM
GLA_HEADS = 4
GLA_DV = MIX_DIM // GLA_HEADS
GLA_DK = GLA_DV // 2
GLA_KDIM = GLA_HEADS * GLA_DK
GLA_GATE_RANK = 16
GLA_GATE_NORM = 16.0
GLA_CHUNK = 64
DIFF_HEADS = MIX_DIM // (2 * HEAD_DIM)
DIFF_VDIM = 2 * HEAD_DIM
ROPE_THETA = 10000.0
FFN_DENSE = 2816
N_EXPERTS = 8
FFN_EXPERT = 3584
EPS = 1e-6
LAMBDA_INIT_L1 = 0.8 - 0.6 * math.exp(-0.3 * 1)

LANES = 128
SUBLANES = 8
VMEM_LIMIT_BYTES = 52 * 1024 * 1024

GLA_DK_PAD = LANES
GLA_DV_PAD = 2 * LANES
DIFF_TQ = 512
ROW_DMA_UNROLL = 8
DIFF_TK = 512
DIFF_ONES_ROWS = 16
LOG2_E = 1.4426950408889634

F32 = jnp.float32
BF16 = jnp.bfloat16
_NT = (((1,), (1,)), ((), ()))
_TN = (((0,), (0,)), ((), ()))


def _tiles(T):
    tm = 512 if T % 512 == 0 else T
    return tm


def _cparams(sem):
    return pltpu.CompilerParams(dimension_semantics=sem, vmem_limit_bytes=VMEM_LIMIT_BYTES)


def _rms(xf, g):
    y = xf * lax.rsqrt(jnp.mean(xf * xf, axis=-1, keepdims=True) + EPS)
    return y * g


def _silu(a):
    return a / (1.0 + jnp.exp(-a))


def _dot(a, b):
    return jnp.dot(a, b, preferred_element_type=F32)


def _in_proj_a_kernel(x_ref, g_ref, wq_ref, wk_ref, wv_ref, wgo_ref, wqm_ref, wgl_ref, wup_ref,
                      bgk_ref, q_ref, k_ref, v_ref, go_ref, qm_ref, gk_ref):
    h = _rms(x_ref[...], g_ref[...]).astype(BF16)
    q_ref[...] = _dot(h, wq_ref[...])
    k_ref[...] = _dot(h, wk_ref[...])
    v_ref[...] = _dot(h, wv_ref[...]).astype(BF16)
    go_ref[...] = _dot(h, wgo_ref[...]).astype(BF16)
    qm_ref[...] = _dot(h, wqm_ref[...]).astype(BF16)
    low = _dot(h, wgl_ref[...]).astype(BF16)
    z = _dot(low, wup_ref[...]) + bgk_ref[...]
    log_sig = jnp.minimum(z, 0.0) - jnp.log1p(jnp.exp(-jnp.abs(z)))
    gk_ref[...] = log_sig / GLA_GATE_NORM


def _in_proj_a(x, g, wq, wk, wv, wgo, wqm, wgl, wup, bgk):
    T = x.shape[0]
    tm = _tiles(T)
    row = lambda n: pl.BlockSpec((tm, n), lambda i: (i, 0))
    full = lambda a: pl.BlockSpec(a.shape, lambda i: (0, 0))
    kq, kv_, km = GLA_HEADS * GLA_DK_PAD, GLA_HEADS * GLA_DV_PAD, MEM_DIM
    return pl.pallas_call(
        _in_proj_a_kernel,
        grid=(T // tm,),
        in_specs=[row(D_MODEL)] + [full(a) for a in (g, wq, wk, wv, wgo, wqm, wgl, wup, bgk)],
        out_specs=[row(kq), row(kq), row(kv_), row(kv_), row(km), row(kq)],
        out_shape=[jax.ShapeDtypeStruct((T, kq), F32), jax.ShapeDtypeStruct((T, kq), F32),
                   jax.ShapeDtypeStruct((T, kv_), BF16), jax.ShapeDtypeStruct((T, kv_), BF16),
                   jax.ShapeDtypeStruct((T, km), BF16), jax.ShapeDtypeStruct((T, kq), F32)],
        compiler_params=_cparams(("parallel",)),
        name="in_proj_a",
    )(x, g, wq, wk, wv, wgo, wqm, wgl, wup, bgk)


def _gla_kernel(q_ref, k_ref, gk_ref, v_ref, go_ref, g_ref, o_ref, st_ref, *, n_chunks):
    C = GLA_CHUNK

    @pl.when(pl.program_id(2) == 0)
    def _():
        st_ref[...] = jnp.zeros_like(st_ref)

    r = lax.broadcasted_iota(jnp.int32, (C, C), 0)
    c = lax.broadcasted_iota(jnp.int32, (C, C), 1)
    tril = c <= r
    tril_f = tril.astype(F32)
    g = g_ref[...]
    chunks = [pl.ds(ci * C, C) for ci in range(n_chunks)]
    bs = [jnp.dot(tril_f, gk_ref[sl, :], precision=lax.Precision.HIGHEST,
                  preferred_element_type=F32) for sl in chunks]
    q_in, k_in, q_state, k_state, decay = [], [], [], [], []
    for sl, b in zip(chunks, bs):
        b_mid = b[C // 2 - 1:C // 2, :]
        b_last = b[C - 1:C, :]
        qc = q_ref[sl, :] * (GLA_DK ** -0.5)
        kc = k_ref[sl, :]
        q_in.append((qc * jnp.exp(b - b_mid)).astype(BF16))
        k_in.append((kc * jnp.exp(b_mid - b)).astype(BF16))
        q_state.append((qc * jnp.exp(b)).astype(BF16))
        k_state.append((kc * jnp.exp(b_last - b)).astype(BF16))
        decay.append(jnp.exp(b_last))
    scores = [jnp.where(tril, lax.dot_general(qi, ki, _NT, preferred_element_type=F32), 0.0)
              for qi, ki in zip(q_in, k_in)]
    o_intra = [_dot(sc.astype(BF16), v_ref[sl, :]) for sl, sc in zip(chunks, scores)]
    kv_t = [lax.dot_general(v_ref[sl, :], ks, _TN, preferred_element_type=F32)
            for sl, ks in zip(chunks, k_state)]

    st = st_ref[...]
    for ci in range(n_chunks):
        sl = pl.ds(ci * C, C)
        o = o_intra[ci] + lax.dot_general(q_state[ci], st.astype(BF16), _NT,
                                          preferred_element_type=F32)
        st = st * decay[ci] + kv_t[ci]
        ms = jnp.sum(o * o, axis=-1, keepdims=True) * (1.0 / GLA_DV)
        on = o * lax.rsqrt(ms + EPS) * g
        go = go_ref[sl, :].astype(F32)
        o_ref[sl, :] = (on * _silu(go)).astype(BF16)
    st_ref[...] = st


def _gla(q, k, gk, v, go, g_gla, B, S):
    blk = 512 if S % 512 == 0 else S
    n_s = S // blk
    kmap = lambda b, h, s: (b * n_s + s, h)
    return pl.pallas_call(
        functools.partial(_gla_kernel, n_chunks=blk // GLA_CHUNK),
        grid=(B, GLA_HEADS, n_s),
        in_specs=[pl.BlockSpec((blk, GLA_DK_PAD), kmap), pl.BlockSpec((blk, GLA_DK_PAD), kmap),
                  pl.BlockSpec((blk, GLA_DK_PAD), kmap), pl.BlockSpec((blk, GLA_DV_PAD), kmap),
                  pl.BlockSpec((blk, GLA_DV_PAD), kmap),
                  pl.BlockSpec((1, GLA_DV_PAD), lambda b, h, s: (0, 0))],
        out_specs=pl.BlockSpec((blk, GLA_DV_PAD), kmap),
        out_shape=jax.ShapeDtypeStruct((B * S, GLA_HEADS * GLA_DV_PAD), BF16),
        scratch_shapes=[pltpu.VMEM((GLA_DV_PAD, GLA_DK_PAD), F32)],
        compiler_params=_cparams(("parallel", "parallel", "arbitrary")),
        name="gla",
    )(q, k, gk, v, go, g_gla)


def _matmul_kernel(x_ref, w_ref, o_ref):
    o_ref[...] = _dot(x_ref[...].astype(BF16), w_ref[...]).astype(o_ref.dtype)


def _matmul(x, w, out_dtype):
    M, K = x.shape
    N = w.shape[1]
    tm = 512 if M % 512 == 0 else M
    return pl.pallas_call(
        _matmul_kernel,
        grid=(M // tm,),
        in_specs=[pl.BlockSpec((tm, K), lambda i: (i, 0)), pl.BlockSpec((K, N), lambda i: (0, 0))],
        out_specs=pl.BlockSpec((tm, N), lambda i: (i, 0)),
        out_shape=jax.ShapeDtypeStruct((M, N), out_dtype),
        compiler_params=_cparams(("parallel",)),
        name="mem_kv_proj",
    )(x, w)


def _mem_attn_kernel(q_ref, kv_ref, o_ref):
    q = q_ref[...]
    kv = kv_ref[0]
    kmat = kv[:, :MEM_DIM]
    vmat = kv[:, MEM_DIM:]
    lane = lax.broadcasted_iota(jnp.int32, kmat.shape, 1)
    acc = jnp.zeros((q.shape[0], MEM_DIM), F32)
    for h in range(MEM_HEADS):
        in_head = (lane // HEAD_DIM) == h
        kh = jnp.where(in_head, kmat, jnp.zeros_like(kmat))
        vh = jnp.where(in_head, vmat, jnp.zeros_like(vmat))
        s = lax.dot_general(q, kh, _NT, preferred_element_type=F32) * (HEAD_DIM ** -0.5)
        p = jnp.exp(s - jnp.max(s, axis=-1, keepdims=True))
        p = p / jnp.sum(p, axis=-1, keepdims=True)
        acc = acc + _dot(p.astype(BF16), vh)
    o_ref[...] = acc.astype(BF16)


def _mem_attn(qm, kv_mem, B, S):
    tm = 512 if S % 512 == 0 else S
    n_s = S // tm
    return pl.pallas_call(
        _mem_attn_kernel,
        grid=(B, n_s),
        in_specs=[pl.BlockSpec((tm, MEM_DIM), lambda b, s: (b * n_s + s, 0)),
                  pl.BlockSpec((1, MEM_LEN, 2 * MEM_DIM), lambda b, s: (b, 0, 0))],
        out_specs=pl.BlockSpec((tm, MEM_DIM), lambda b, s: (b * n_s + s, 0)),
        out_shape=jax.ShapeDtypeStruct((B * S, MEM_DIM), BF16),
        compiler_params=_cparams(("parallel", "parallel")),
        name="mem_attn",
    )(qm, kv_mem)


def _out_proj_kernel(x_ref, o_ref, m_ref, w1_ref, w2_ref, y_ref):
    y_ref[...] = x_ref[...] + _dot(o_ref[...], w1_ref[...]) + _dot(m_ref[...], w2_ref[...])


def _out_proj(x, o, m, w1, w2):
    T = x.shape[0]
    tm = _tiles(T)
    row = lambda n: pl.BlockSpec((tm, n), lambda i: (i, 0))
    full = lambda a: pl.BlockSpec(a.shape, lambda i: (0, 0))
    return pl.pallas_call(
        _out_proj_kernel,
        grid=(T // tm,),
        in_specs=[row(D_MODEL), row(o.shape[1]), row(MEM_DIM), full(w1), full(w2)],
        out_specs=row(D_MODEL),
        out_shape=jax.ShapeDtypeStruct((T, D_MODEL), F32),
        compiler_params=_cparams(("parallel",)),
        name="out_proj",
    )(x, o, m, w1, w2)


def _ffn_kernel(x_ref, g_ref, wg_ref, wu_ref, wd_ref, y_ref, h_sc, acc_sc):
    f = pl.program_id(1)

    @pl.when(f == 0)
    def _():
        h_sc[...] = _rms(x_ref[...], g_ref[...]).astype(BF16)
        acc_sc[...] = jnp.zeros_like(acc_sc)

    h = h_sc[...]
    act = (_silu(_dot(h, wg_ref[...])) * _dot(h, wu_ref[...])).astype(BF16)
    acc_sc[...] += _dot(act, wd_ref[...])

    @pl.when(f == pl.num_programs(1) - 1)
    def _():
        y_ref[...] = x_ref[...] + acc_sc[...]


def _ffn(x, g, wg, wu, wd):
    T = x.shape[0]
    tm = _tiles(T)
    F = wg.shape[1]
    tf = F // 2
    return pl.pallas_call(
        _ffn_kernel,
        grid=(T // tm, F // tf),
        in_specs=[pl.BlockSpec((tm, D_MODEL), lambda i, f: (i, 0)),
                  pl.BlockSpec((1, D_MODEL), lambda i, f: (0, 0)),
                  pl.BlockSpec((D_MODEL, tf), lambda i, f: (0, f)),
                  pl.BlockSpec((D_MODEL, tf), lambda i, f: (0, f)),
                  pl.BlockSpec((tf, D_MODEL), lambda i, f: (f, 0))],
        out_specs=pl.BlockSpec((tm, D_MODEL), lambda i, f: (i, 0)),
        out_shape=jax.ShapeDtypeStruct((T, D_MODEL), F32),
        scratch_shapes=[pltpu.VMEM((tm, D_MODEL), BF16), pltpu.VMEM((tm, D_MODEL), F32)],
        compiler_params=_cparams(("parallel", "arbitrary")),
        name="dense_ffn",
    )(x, g, wg, wu, wd)


def _proj_rope_kernel(x_ref, g_ref, wr_ref, wp_ref, cos_ref, sin_ref, r_ref, p_ref, *, scale,
                      transpose_pass):
    h = _rms(x_ref[...], g_ref[...]).astype(BF16)
    p = _dot(h, wp_ref[...])
    if transpose_pass:
        pt = p.T.astype(BF16)
        rows = DIFF_VDIM + DIFF_ONES_ROWS
        for hh in range(p.shape[1] // DIFF_VDIM):
            p_ref[0, hh * rows:hh * rows + DIFF_VDIM, :] = pt[hh * DIFF_VDIM:(hh + 1) * DIFF_VDIM, :]
            p_ref[0, hh * rows + DIFF_VDIM:(hh + 1) * rows, :] = jnp.ones(
                (DIFF_ONES_ROWS, pt.shape[1]), BF16)
    else:
        p_ref[...] = p.astype(BF16)
    cos = cos_ref[...]
    sin = sin_ref[...]
    lane = lax.broadcasted_iota(jnp.int32, cos.shape, 1)
    first_half = (lane % HEAD_DIM) < (HEAD_DIM // 2)
    y_all = _dot(h, wr_ref[...])
    for j in range(wr_ref.shape[1] // LANES):
        sl = slice(j * LANES, (j + 1) * LANES)
        y = y_all[:, sl]
        partner = jnp.where(first_half, pltpu.roll(y, LANES - HEAD_DIM // 2, 1),
                            pltpu.roll(y, HEAD_DIM // 2, 1))
        r_ref[:, sl] = ((y * cos + partner * sin) * scale).astype(BF16)


def _proj_rope(x, g, w_rope, w_pass, cos, sin, S, scale, transpose_pass):
    T = x.shape[0]
    tm = DIFF_TK if S % DIFF_TK == 0 else S
    n_s = S // tm
    row = lambda n: pl.BlockSpec((tm, n), lambda i: (i, 0))
    full = lambda a: pl.BlockSpec(a.shape, lambda i: (0, 0))
    tab = pl.BlockSpec((tm, LANES), lambda i: (i % n_s, 0))
    nr, np_ = w_rope.shape[1], w_pass.shape[1]
    if transpose_pass:
        nt = np_ // DIFF_VDIM * (DIFF_VDIM + DIFF_ONES_ROWS)
        pass_spec = pl.BlockSpec((1, nt, tm), lambda i: (i, 0, 0))
        pass_shape = jax.ShapeDtypeStruct((T // tm, nt, tm), BF16)
    else:
        pass_spec = row(np_)
        pass_shape = jax.ShapeDtypeStruct((T, np_), BF16)
    return pl.pallas_call(
        functools.partial(_proj_rope_kernel, scale=scale, transpose_pass=transpose_pass),
        grid=(T // tm,),
        in_specs=[row(D_MODEL), full(g), full(w_rope), full(w_pass), tab, tab],
        out_specs=[row(nr), pass_spec],
        out_shape=[jax.ShapeDtypeStruct((T, nr), BF16), pass_shape],
        compiler_params=_cparams(("parallel",)),
        name="proj_rope",
    )(x, g, w_rope, w_pass, cos, sin)


def _diff_attn_kernel(q_ref, k_ref, vt_ref, lam_ref, g_ref, o_ref, qs_sc, m_sc, acc_sc, s0_sc, s1_sc,
                      *, tq, tk):
    i = pl.program_id(2)
    q = q_ref[...]
    lane = lax.broadcasted_iota(jnp.int32, q.shape, 1)
    qs_sc[0:tq, :] = jnp.where(lane < HEAD_DIM, q, jnp.zeros_like(q))
    qs_sc[tq:2 * tq, :] = jnp.where(lane >= HEAD_DIM, q, jnp.zeros_like(q))
    m_sc[...] = jnp.full_like(m_sc, -1e30)
    acc_sc[...] = jnp.zeros_like(acc_sc)

    def scores(j, s_ref):
        kj = k_ref[pl.ds(pl.multiple_of(j * tk, tk), tk), :]
        s_ref[...] = lax.dot_general(kj, qs_sc[...], _NT, preferred_element_type=F32)

    def update(j, s_ref, masked):
        s = s_ref[...]
        if masked:
            kpos = j * tk + lax.broadcasted_iota(jnp.int32, s.shape, 0)
            qpos = i * tq + lax.broadcasted_iota(jnp.int32, s.shape, 1) % tq
            s = jnp.where(kpos <= qpos, s, -1e30)
        m_prev = m_sc[...]
        m_new = jnp.maximum(m_prev, jnp.max(s, axis=0, keepdims=True))
        alpha = jnp.exp2(m_prev - m_new)
        p = jnp.exp2(s - m_new).astype(BF16)
        acc_sc[...] = alpha * acc_sc[...] + _dot(vt_ref[j], p)
        m_sc[...] = m_new

    n_full = (i * tq) // tk
    scores(0, s0_sc)

    def body(jj, carry):
        a = 2 * jj
        scores(a + 1, s1_sc)
        update(a, s0_sc, False)
        scores(a + 2, s0_sc)
        update(a + 1, s1_sc, False)
        return carry

    lax.fori_loop(0, n_full // 2, body, 0)

    @pl.when(n_full % 2 == 1)
    def _():
        scores(n_full, s1_sc)
        update(n_full - 1, s0_sc, False)
        update(n_full, s1_sc, True)

    @pl.when(n_full % 2 == 0)
    def _():
        update(n_full, s0_sc, True)

    lam_rows = lam_ref[...]
    dot1 = jnp.sum(lam_rows[0:1, :] * lam_rows[1:2, :], axis=-1, keepdims=True)
    dot2 = jnp.sum(lam_rows[2:3, :] * lam_rows[3:4, :], axis=-1, keepdims=True)
    lam = jnp.exp(dot1) - jnp.exp(dot2) + LAMBDA_INIT_L1
    dv = DIFF_VDIM
    ot = (acc_sc[0:dv, 0:tq] / acc_sc[dv:dv + 1, 0:tq]
          - lam * (acc_sc[0:dv, tq:2 * tq] / acc_sc[dv:dv + 1, tq:2 * tq]))
    ot = ot * lax.rsqrt(jnp.mean(ot * ot, axis=0, keepdims=True) + EPS)
    o_ref[...] = ((ot.T * g_ref[...]) * (1.0 - LAMBDA_INIT_L1)).astype(BF16)


def _diff_attn(q, k, vt, lam_rows, g_subln, B, S):
    tk = vt.shape[2]
    tq = DIFF_TQ if S % DIFF_TQ == 0 else S
    assert tq <= tk and tk % tq == 0 and S % tk == 0
    n_q = S // tq
    n_k = S // tk
    rows = DIFF_VDIM + DIFF_ONES_ROWS
    return pl.pallas_call(
        functools.partial(_diff_attn_kernel, tq=tq, tk=tk),
        grid=(B, DIFF_HEADS, n_q),
        in_specs=[pl.BlockSpec((tq, DIFF_VDIM), lambda b, h, i: (b * n_q + i, h)),
                  pl.BlockSpec((S, DIFF_VDIM), lambda b, h, i: (b, h)),
                  pl.BlockSpec((n_k, rows, tk), lambda b, h, i: (b, h, 0)),
                  pl.BlockSpec(lam_rows.shape, lambda b, h, i: (0, 0)),
                  pl.BlockSpec((1, DIFF_VDIM), lambda b, h, i: (0, 0))],
        out_specs=pl.BlockSpec((tq, DIFF_VDIM), lambda b, h, i: (b * n_q + i, h)),
        out_shape=jax.ShapeDtypeStruct((B * S, MIX_DIM), BF16),
        scratch_shapes=[pltpu.VMEM((2 * tq, DIFF_VDIM), BF16), pltpu.VMEM((1, 2 * tq), F32),
                        pltpu.VMEM((rows, 2 * tq), F32), pltpu.VMEM((tk, 2 * tq), F32),
                        pltpu.VMEM((tk, 2 * tq), F32)],
        compiler_params=_cparams(("parallel", "parallel", "arbitrary")),
        name="diff_attn",
    )(q, k, vt, lam_rows, g_subln)


_R_E0, _R_E1, _R_RANK0, _R_RANK1, _R_G0, _R_G1 = range(6)


def _router_kernel(x_ref, g_ref, wr_ref, h_ref, info_ref, cnt_ref, run_sc):
    @pl.when(pl.program_id(0) == 0)
    def _():
        run_sc[...] = jnp.zeros_like(run_sc)

    h = _rms(x_ref[...], g_ref[...])
    h_ref[...] = h
    logits = jnp.dot(h, wr_ref[...], precision=lax.Precision.HIGHEST, preferred_element_type=F32)
    tm = logits.shape[0]
    lane = lax.broadcasted_iota(jnp.int32, logits.shape, 1).astype(F32)
    neg = -jnp.inf
    lg = jnp.where(lane < N_EXPERTS, logits, neg)
    m1 = jnp.max(lg, axis=-1, keepdims=True)
    i1 = jnp.min(jnp.where(lg == m1, lane, float(LANES)), axis=-1, keepdims=True)
    lg2 = jnp.where(lane == i1, neg, lg)
    m2 = jnp.max(lg2, axis=-1, keepdims=True)
    i2 = jnp.min(jnp.where(lg2 == m2, lane, float(LANES)), axis=-1, keepdims=True)
    e = jnp.exp(m2 - m1)
    g0 = 1.0 / (1.0 + e)
    g1 = e / (1.0 + e)
    onehot = jnp.where((lane == i1) | (lane == i2), 1.0, 0.0)
    r = lax.broadcasted_iota(jnp.int32, (tm, tm), 0)
    c = lax.broadcasted_iota(jnp.int32, (tm, tm), 1)
    before = _dot(jnp.where(c < r, 1.0, 0.0).astype(BF16), onehot.astype(BF16)) + run_sc[0:1, :]
    rank0 = jnp.sum(jnp.where(lane == i1, before, 0.0), axis=-1, keepdims=True)
    rank1 = jnp.sum(jnp.where(lane == i2, before, 0.0), axis=-1, keepdims=True)
    run_sc[...] = run_sc[...] + jnp.sum(onehot, axis=0, keepdims=True)
    info = jnp.zeros_like(logits)
    for idx, val in ((_R_E0, i1), (_R_E1, i2), (_R_RANK0, rank0), (_R_RANK1, rank1),
                     (_R_G0, g0), (_R_G1, g1)):
        info = jnp.where(lane == idx, val, info)
    info_ref[...] = info
    cnt_ref[...] = run_sc[...]


def _router(x, g, wr):
    T = x.shape[0]
    tm = _tiles(T)
    return pl.pallas_call(
        _router_kernel,
        grid=(T // tm,),
        in_specs=[pl.BlockSpec((tm, D_MODEL), lambda i: (i, 0)),
                  pl.BlockSpec((1, D_MODEL), lambda i: (0, 0)),
                  pl.BlockSpec((D_MODEL, LANES), lambda i: (0, 0))],
        out_specs=[pl.BlockSpec((tm, D_MODEL), lambda i: (i, 0)),
                   pl.BlockSpec((tm, LANES), lambda i: (i, 0)),
                   pl.BlockSpec((SUBLANES, LANES), lambda i: (0, 0))],
        out_shape=[jax.ShapeDtypeStruct((T, D_MODEL), F32), jax.ShapeDtypeStruct((T, LANES), F32),
                   jax.ShapeDtypeStruct((SUBLANES, LANES), F32)],
        scratch_shapes=[pltpu.VMEM((SUBLANES, LANES), F32)],
        compiler_params=_cparams(("arbitrary",)),
        name="moe_router",
    )(x, g, wr)


def _row_copy(src, src_row, dst, dst_row, sem):
    return pltpu.make_async_copy(src.at[pl.ds(src_row, 1)], dst.at[pl.ds(dst_row, 1)], sem)


def _dispatch_kernel(d0_ref, d1_ref, h_ref, xs_in_ref, xs_ref, sem):
    del xs_in_ref
    tm = h_ref.shape[0]

    def issue(r, carry):
        _row_copy(h_ref, r, xs_ref, d0_ref[r], sem).start()
        _row_copy(h_ref, r, xs_ref, d1_ref[r], sem).start()
        return carry

    lax.fori_loop(0, tm, issue, 0, unroll=ROW_DMA_UNROLL)

    def drain(r, carry):
        _row_copy(h_ref, 0, xs_ref, 0, sem).wait()
        _row_copy(h_ref, 0, xs_ref, 0, sem).wait()
        return carry

    lax.fori_loop(0, tm, drain, 0, unroll=ROW_DMA_UNROLL)


def _dispatch(dest0, dest1, h, xs_zero):
    T = h.shape[0]
    tm = _tiles(T)
    smem = pl.BlockSpec((tm,), lambda i: (i,), memory_space=pltpu.SMEM)
    return pl.pallas_call(
        _dispatch_kernel,
        grid=(T // tm,),
        in_specs=[smem, smem, pl.BlockSpec((tm, D_MODEL), lambda i: (i, 0)),
                  pl.BlockSpec(memory_space=pl.ANY)],
        out_specs=pl.BlockSpec(memory_space=pl.ANY),
        out_shape=jax.ShapeDtypeStruct(xs_zero.shape, xs_zero.dtype),
        scratch_shapes=[pltpu.SemaphoreType.DMA],
        input_output_aliases={3: 0},
        compiler_params=_cparams(("arbitrary",)),
        name="moe_dispatch",
    )(dest0, dest1, h, xs_zero)


def _expert_kernel(bexp_ref, nused_ref, xs_ref, wg_ref, wu_ref, wd_ref, y_ref, xb_sc, acc_sc):
    del bexp_ref
    j = pl.program_id(0)
    f = pl.program_id(1)

    @pl.when(j < nused_ref[0])
    def _():
        @pl.when(f == 0)
        def _():
            xb_sc[...] = xs_ref[...].astype(BF16)
            acc_sc[...] = jnp.zeros_like(acc_sc)

        xb = xb_sc[...]
        act = (_silu(_dot(xb, wg_ref[0].astype(BF16)))
               * _dot(xb, wu_ref[0].astype(BF16))).astype(BF16)
        acc_sc[...] += _dot(act, wd_ref[0].astype(BF16))

        @pl.when(f == pl.num_programs(1) - 1)
        def _():
            y_ref[...] = acc_sc[...]

    @pl.when((j >= nused_ref[0]) & (f == 0))
    def _():
        y_ref[...] = jnp.zeros_like(y_ref)


def _experts(block_expert, n_used, xs, wg, wu, wd, rb):
    L = xs.shape[0]
    nb = L // rb
    tf = 512
    nf = FFN_EXPERT // tf

    def blk(j, f, be, nu):
        return jnp.minimum(j, nu[0] - 1)

    def ftile(j, f, be, nu):
        return jnp.where(j < nu[0], f, nf - 1)

    return pl.pallas_call(
        _expert_kernel,
        grid_spec=pltpu.PrefetchScalarGridSpec(
            num_scalar_prefetch=2,
            grid=(nb, nf),
            in_specs=[
                pl.BlockSpec((rb, D_MODEL), lambda j, f, be, nu: (blk(j, f, be, nu), 0)),
                pl.BlockSpec((1, D_MODEL, tf),
                             lambda j, f, be, nu: (be[blk(j, f, be, nu)], 0, ftile(j, f, be, nu))),
                pl.BlockSpec((1, D_MODEL, tf),
                             lambda j, f, be, nu: (be[blk(j, f, be, nu)], 0, ftile(j, f, be, nu))),
                pl.BlockSpec((1, tf, D_MODEL),
                             lambda j, f, be, nu: (be[blk(j, f, be, nu)], ftile(j, f, be, nu), 0)),
            ],
            out_specs=pl.BlockSpec((rb, D_MODEL), lambda j, f, be, nu: (j, 0)),
            scratch_shapes=[pltpu.VMEM((rb, D_MODEL), BF16), pltpu.VMEM((rb, D_MODEL), F32)],
        ),
        out_shape=jax.ShapeDtypeStruct((L, D_MODEL), F32),
        compiler_params=_cparams(("arbitrary", "arbitrary")),
        name="moe_experts",
    )(block_expert, n_used, xs, wg, wu, wd)


def _combine_kernel(d0_ref, d1_ref, x_ref, info_ref, g_ref, y_ref, o_ref, b0_sc, b1_sc, sem):
    tm = x_ref.shape[0]

    def issue(r, carry):
        _row_copy(y_ref, d0_ref[r], b0_sc, r, sem).start()
        _row_copy(y_ref, d1_ref[r], b1_sc, r, sem).start()
        return carry

    lax.fori_loop(0, tm, issue, 0, unroll=ROW_DMA_UNROLL)

    def drain(r, carry):
        _row_copy(y_ref, 0, b0_sc, 0, sem).wait()
        _row_copy(y_ref, 0, b1_sc, 0, sem).wait()
        return carry

    lax.fori_loop(0, tm, drain, 0, unroll=ROW_DMA_UNROLL)
    info = info_ref[...]
    g0 = info[:, _R_G0:_R_G0 + 1]
    g1 = info[:, _R_G1:_R_G1 + 1]
    x = x_ref[...] + (b0_sc[...] * g0 + b1_sc[...] * g1)
    o_ref[...] = _rms(x, g_ref[...])


def _combine(dest0, dest1, x, info, g_final, y):
    T = x.shape[0]
    tm = _tiles(T)
    smem = pl.BlockSpec((tm,), lambda i: (i,), memory_space=pltpu.SMEM)
    return pl.pallas_call(
        _combine_kernel,
        grid=(T // tm,),
        in_specs=[smem, smem, pl.BlockSpec((tm, D_MODEL), lambda i: (i, 0)),
                  pl.BlockSpec((tm, LANES), lambda i: (i, 0)),
                  pl.BlockSpec((1, D_MODEL), lambda i: (0, 0)),
                  pl.BlockSpec(memory_space=pl.ANY)],
        out_specs=pl.BlockSpec((tm, D_MODEL), lambda i: (i, 0)),
        out_shape=jax.ShapeDtypeStruct((T, D_MODEL), F32),
        scratch_shapes=[pltpu.VMEM((tm, D_MODEL), F32), pltpu.VMEM((tm, D_MODEL), F32),
                        pltpu.SemaphoreType.DMA],
        compiler_params=_cparams(("arbitrary",)),
        name="moe_combine",
    )(dest0, dest1, x, info, g_final, y)


def _pad_heads_cols(w, heads, d, d_pad):
    k = w.shape[0]
    return jnp.pad(w.reshape(k, heads, d), ((0, 0), (0, 0), (0, d_pad - d))).reshape(k, heads * d_pad)


def _rope_tables(S):
    half = HEAD_DIM // 2
    inv = ROPE_THETA ** (-jnp.arange(half, dtype=F32) / half)
    ang = jnp.arange(S).astype(F32)[:, None] * inv[None, :]
    reps = LANES // half
    cos = jnp.tile(jnp.cos(ang), (1, reps))
    sign = jnp.tile(jnp.concatenate([-jnp.ones((half,), F32), jnp.ones((half,), F32)]),
                    LANES // HEAD_DIM)
    sin = jnp.tile(jnp.sin(ang), (1, reps)) * sign[None, :]
    return cos, sin


def _moe_block_rows(T):
    return 1024 if (2 * T) % 1024 == 0 else 256


def kernel(x, mem, g_mix, g_ffn, w_mem_kv, w_out, w_in_a, w_gk_up, b_gk, g_gla, g_kv, w_kv, w_in_b, lambda_q1, lambda_k1, lambda_q2, lambda_k2, g_subln, w_dense_gate, w_dense_up, w_dense_down, w_router, w_exp_gate, w_exp_up, w_exp_down, g_final):
    B, S, D = x.shape
    T = B * S
    x0 = x.reshape(T, D)
    mem2 = mem.reshape(B * MEM_LEN, D)
    row = lambda v: v.reshape(1, -1)

    wa = w_in_a[0]
    s0, s1, s2, s3, s4 = (GLA_KDIM, 2 * GLA_KDIM, 2 * GLA_KDIM + MIX_DIM, 2 * GLA_KDIM + 2 * MIX_DIM,
                          2 * GLA_KDIM + 2 * MIX_DIM + GLA_GATE_RANK)
    wq = _pad_heads_cols(wa[:, :s0], GLA_HEADS, GLA_DK, GLA_DK_PAD).astype(BF16)
    wk = _pad_heads_cols(wa[:, s0:s1], GLA_HEADS, GLA_DK, GLA_DK_PAD).astype(BF16)
    wv = _pad_heads_cols(wa[:, s1:s2], GLA_HEADS, GLA_DV, GLA_DV_PAD).astype(BF16)
    wgo = _pad_heads_cols(wa[:, s2:s3], GLA_HEADS, GLA_DV, GLA_DV_PAD).astype(BF16)
    wgl = jnp.pad(wa[:, s3:s4], ((0, 0), (0, LANES - GLA_GATE_RANK))).astype(BF16)
    wqm = wa[:, s4:].astype(BF16)
    wup = jnp.pad(_pad_heads_cols(w_gk_up[0], GLA_HEADS, GLA_DK, GLA_DK_PAD),
                  ((0, LANES - GLA_GATE_RANK), (0, 0))).astype(BF16)
    bgk = _pad_heads_cols(row(b_gk[0]), GLA_HEADS, GLA_DK, GLA_DK_PAD)
    ggla = jnp.pad(row(g_gla[0]), ((0, 0), (0, GLA_DV_PAD - GLA_DV)))
    wo_a = jnp.pad(w_out[0][:MIX_DIM].reshape(GLA_HEADS, GLA_DV, D),
                   ((0, 0), (0, GLA_DV_PAD - GLA_DV), (0, 0))).reshape(GLA_HEADS * GLA_DV_PAD, D)

    q, k, v, go, qm, gk = _in_proj_a(x0, row(g_mix[0]), wq, wk, wv, wgo, wqm, wgl, wup, bgk)
    o = _gla(q, k, gk, v, go, ggla, B, S)
    kv_mem0 = _matmul(mem2, w_mem_kv[0].astype(BF16), BF16).reshape(B, MEM_LEN, 2 * MEM_DIM)
    m = _mem_attn(qm, kv_mem0, B, S)
    x1 = _out_proj(x0, o, m, wo_a.astype(BF16), w_out[0][MIX_DIM:].astype(BF16))
    x2 = _ffn(x1, row(g_ffn[0]), w_dense_gate[0].astype(BF16), w_dense_up[0].astype(BF16),
              w_dense_down[0].astype(BF16))

    cos, sin = _rope_tables(S)
    k_sh, vt_sh = _proj_rope(x2, row(g_kv), w_kv[:, :MIX_DIM].astype(BF16),
                             w_kv[:, MIX_DIM:].astype(BF16), cos, sin, S, 1.0, True)

    q1, qm1 = _proj_rope(x2, row(g_mix[1]), w_in_b[0][:, :MIX_DIM].astype(BF16),
                         w_in_b[0][:, MIX_DIM:].astype(BF16), cos, sin, S,
                         HEAD_DIM ** -0.5 * LOG2_E, False)
    lam_rows = jnp.pad(jnp.stack([lambda_q1[0], lambda_k1[0], lambda_q2[0], lambda_k2[0]]),
                       ((0, SUBLANES - 4), (0, LANES - HEAD_DIM)))
    o1 = _diff_attn(q1, k_sh, vt_sh, lam_rows, row(g_subln[0]), B, S)
    kv_mem1 = _matmul(mem2, w_mem_kv[1].astype(BF16), BF16).reshape(B, MEM_LEN, 2 * MEM_DIM)
    m1 = _mem_attn(qm1, kv_mem1, B, S)
    x3 = _out_proj(x2, o1, m1, w_out[1][:MIX_DIM].astype(BF16), w_out[1][MIX_DIM:].astype(BF16))

    wr = jnp.pad(w_router[0], ((0, 0), (0, LANES - N_EXPERTS)))
    h, info, cnt = _router(x3, row(g_ffn[1]), wr)

    rb = _moe_block_rows(T)
    counts = cnt[0, :N_EXPERTS].astype(jnp.int32)
    padded = (counts + rb - 1) // rb * rb
    pend = jnp.cumsum(padded)
    pstart = pend - padded
    e0 = info[:, _R_E0].astype(jnp.int32)
    e1 = info[:, _R_E1].astype(jnp.int32)
    eids = jnp.arange(N_EXPERTS, dtype=jnp.int32)
    start_of = lambda e: jnp.sum(jnp.where(e[:, None] == eids[None, :], pstart[None, :], 0), axis=1)
    dest0 = start_of(e0) + info[:, _R_RANK0].astype(jnp.int32)
    dest1 = start_of(e1) + info[:, _R_RANK1].astype(jnp.int32)
    nb = (2 * T) // rb + N_EXPERTS
    block_expert = jnp.clip(jnp.sum(pend[None, :] <= (jnp.arange(nb, dtype=jnp.int32) * rb)[:, None],
                                    axis=1), 0, N_EXPERTS - 1).astype(jnp.int32)
    n_used = (pend[-1:] // rb).astype(jnp.int32)

    xs = _dispatch(dest0, dest1, h, jnp.zeros((nb * rb, D), F32))
    ys = _experts(block_expert, n_used, xs, w_exp_gate[0], w_exp_up[0], w_exp_down[0], rb)
    out = _combine(dest0, dest1, x3, info, row(g_final), ys)
    return out.reshape(B, S, D)
```

```python
import functools
import math

import jax
import jax.numpy as jnp
from jax import lax
from jax.experimental import pallas as pl
from jax.experimental.pallas import tpu as pltpu

D_MODEL = 1024
MEM_LEN = 256
HEAD_DIM = 64
MEM_HEADS = 4
MEM_DIM = MEM_HEADS * HEAD_DIM
MIX_DIM = D_MODEL - MEM_DIM
GLA_HEADS = 4
GLA_DV = MIX_DIM // GLA_HEADS
GLA_DK = GLA_DV // 2
GLA_KDIM = GLA_HEADS * GLA_DK
GLA_GATE_RANK = 16
GLA_GATE_NORM = 16.0
GLA_CHUNK = 64
DIFF_HEADS = MIX_DIM // (2 * HEAD_DIM)
DIFF_VDIM = 2 * HEAD_DIM
ROPE_THETA = 10000.0
FFN_DENSE = 2816
N_EXPERTS = 8
FFN_EXPERT = 3584
EPS = 1e-6
LAMBDA_INIT_L1 = 0.8 - 0.6 * math.exp(-0.3 * 1)

LANES = 128
SUBLANES = 8
VMEM_LIMIT_BYTES = 52 * 1024 * 1024

GLA_DK_PAD = LANES
GLA_DV_PAD = 2 * LANES
DIFF_TQ = 512
ROW_DMA_UNROLL = 8
DIFF_TK = 512
DIFF_ONES_ROWS = 16
LOG2_E = 1.4426950408889634

F32 = jnp.float32
BF16 = jnp.bfloat16
_NT = (((1,), (1,)), ((), ()))
_TN = (((0,), (0,)), ((), ()))


def _tiles(T):
    tm = 512 if T % 512 == 0 else T
    return tm


def _cparams(sem):
    return pltpu.CompilerParams(dimension_semantics=sem, vmem_limit_bytes=VMEM_LIMIT_BYTES)


def _rms(xf, g):
    y = xf * lax.rsqrt(jnp.mean(xf * xf, axis=-1, keepdims=True) + EPS)
    return y * g


def _silu(a):
    return a / (1.0 + jnp.exp(-a))


def _dot(a, b):
    return jnp.dot(a, b, preferred_element_type=F32)


def _in_proj_a_kernel(x_ref, g_ref, wq_ref, wk_ref, wv_ref, wgo_ref, wqm_ref, wgl_ref, wup_ref,
                      bgk_ref, q_ref, k_ref, v_ref, go_ref, qm_ref, gk_ref):
    h = _rms(x_ref[...], g_ref[...]).astype(BF16)
    q_ref[...] = _dot(h, wq_ref[...])
    k_ref[...] = _dot(h, wk_ref[...])
    v_ref[...] = _dot(h, wv_ref[...]).astype(BF16)
    go_ref[...] = _dot(h, wgo_ref[...]).astype(BF16)
    qm_ref[...] = _dot(h, wqm_ref[...]).astype(BF16)
    low = _dot(h, wgl_ref[...]).astype(BF16)
    z = _dot(low, wup_ref[...]) + bgk_ref[...]
    log_sig = jnp.minimum(z, 0.0) - jnp.log1p(jnp.exp(-jnp.abs(z)))
    gk_ref[...] = log_sig / GLA_GATE_NORM


def _in_proj_a(x, g, wq, wk, wv, wgo, wqm, wgl, wup, bgk):
    T = x.shape[0]
    tm = _tiles(T)
    row = lambda n: pl.BlockSpec((tm, n), lambda i: (i, 0))
    full = lambda a: pl.BlockSpec(a.shape, lambda i: (0, 0))
    kq, kv_, km = GLA_HEADS * GLA_DK_PAD, GLA_HEADS * GLA_DV_PAD, MEM_DIM
    return pl.pallas_call(
        _in_proj_a_kernel,
        grid=(T // tm,),
        in_specs=[row(D_MODEL)] + [full(a) for a in (g, wq, wk, wv, wgo, wqm, wgl, wup, bgk)],
        out_specs=[row(kq), row(kq), row(kv_), row(kv_), row(km), row(kq)],
        out_shape=[jax.ShapeDtypeStruct((T, kq), F32), jax.ShapeDtypeStruct((T, kq), F32),
                   jax.ShapeDtypeStruct((T, kv_), BF16), jax.ShapeDtypeStruct((T, kv_), BF16),
                   jax.ShapeDtypeStruct((T, km), BF16), jax.ShapeDtypeStruct((T, kq), F32)],
        compiler_params=_cparams(("parallel",)),
        name="in_proj_a",
    )(x, g, wq, wk, wv, wgo, wqm, wgl, wup, bgk)


def _gla_kernel(q_ref, k_ref, gk_ref, v_ref, go_ref, g_ref, o_ref, st_ref, *, n_chunks):
    C = GLA_CHUNK

    @pl.when(pl.program_id(2) == 0)
    def _():
        st_ref[...] = jnp.zeros_like(st_ref)

    r = lax.broadcasted_iota(jnp.int32, (C, C), 0)
    c = lax.broadcasted_iota(jnp.int32, (C, C), 1)
    tril = c <= r
    tril_f = tril.astype(F32)
    g = g_ref[...]
    chunks = [pl.ds(ci * C, C) for ci in range(n_chunks)]
    bs = [jnp.dot(tril_f, gk_ref[sl, :], precision=lax.Precision.HIGHEST,
                  preferred_element_type=F32) for sl in chunks]
    q_in, k_in, q_state, k_state, decay = [], [], [], [], []
    for sl, b in zip(chunks, bs):
        b_mid = b[C // 2 - 1:C // 2, :]
        b_last = b[C - 1:C, :]
        qc = q_ref[sl, :] * (GLA_DK ** -0.5)
        kc = k_ref[sl, :]
        q_in.append((qc * jnp.exp(b - b_mid)).astype(BF16))
        k_in.append((kc * jnp.exp(b_mid - b)).astype(BF16))
        q_state.append((qc * jnp.exp(b)).astype(BF16))
        k_state.append((kc * jnp.exp(b_last - b)).astype(BF16))
        decay.append(jnp.exp(b_last))
    scores = [jnp.where(tril, lax.dot_general(qi, ki, _NT, preferred_element_type=F32), 0.0)
              for qi, ki in zip(q_in, k_in)]
    o_intra = [_dot(sc.astype(BF16), v_ref[sl, :]) for sl, sc in zip(chunks, scores)]
    kv_t = [lax.dot_general(v_ref[sl, :], ks, _TN, preferred_element_type=F32)
            for sl, ks in zip(chunks, k_state)]

    st = st_ref[...]
    for ci in range(n_chunks):
        sl = pl.ds(ci * C, C)
        o = o_intra[ci] + lax.dot_general(q_state[ci], st.astype(BF16), _NT,
                                          preferred_element_type=F32)
        st = st * decay[ci] + kv_t[ci]
        ms = jnp.sum(o * o, axis=-1, keepdims=True) * (1.0 / GLA_DV)
        on = o * lax.rsqrt(ms + EPS) * g
        go = go_ref[sl, :].astype(F32)
        o_ref[sl, :] = (on * _silu(go)).astype(BF16)
    st_ref[...] = st


def _gla(q, k, gk, v, go, g_gla, B, S):
    blk = 512 if S % 512 == 0 else S
    n_s = S // blk
    kmap = lambda b, h, s: (b * n_s + s, h)
    return pl.pallas_call(
        functools.partial(_gla_kernel, n_chunks=blk // GLA_CHUNK),
        grid=(B, GLA_HEADS, n_s),
        in_specs=[pl.BlockSpec((blk, GLA_DK_PAD), kmap), pl.BlockSpec((blk, GLA_DK_PAD), kmap),
                  pl.BlockSpec((blk, GLA_DK_PAD), kmap), pl.BlockSpec((blk, GLA_DV_PAD), kmap),
                  pl.BlockSpec((blk, GLA_DV_PAD), kmap),
                  pl.BlockSpec((1, GLA_DV_PAD), lambda b, h, s: (0, 0))],
        out_specs=pl.BlockSpec((blk, GLA_DV_PAD), kmap),
        out_shape=jax.ShapeDtypeStruct((B * S, GLA_HEADS * GLA_DV_PAD), BF16),
        scratch_shapes=[pltpu.VMEM((GLA_DV_PAD, GLA_DK_PAD), F32)],
        compiler_params=_cparams(("parallel", "parallel", "arbitrary")),
        name="gla",
    )(q, k, gk, v, go, g_gla)


def _matmul_kernel(x_ref, w_ref, o_ref):
    o_ref[...] = _dot(x_ref[...].astype(BF16), w_ref[...]).astype(o_ref.dtype)


def _matmul(x, w, out_dtype):
    M, K = x.shape
    N = w.shape[1]
    tm = 512 if M % 512 == 0 else M
    return pl.pallas_call(
        _matmul_kernel,
        grid=(M // tm,),
        in_specs=[pl.BlockSpec((tm, K), lambda i: (i, 0)), pl.BlockSpec((K, N), lambda i: (0, 0))],
        out_specs=pl.BlockSpec((tm, N), lambda i: (i, 0)),
        out_shape=jax.ShapeDtypeStruct((M, N), out_dtype),
        compiler_params=_cparams(("parallel",)),
        name="mem_kv_proj",
    )(x, w)


def _mem_attn_kernel(q_ref, kv_ref, o_ref):
    q = q_ref[...]
    kv = kv_ref[0]
    kmat = kv[:, :MEM_DIM]
    vmat = kv[:, MEM_DIM:]
    lane = lax.broadcasted_iota(jnp.int32, kmat.shape, 1)
    acc = jnp.zeros((q.shape[0], MEM_DIM), F32)
    for h in range(MEM_HEADS):
        in_head = (lane // HEAD_DIM) == h
        kh = jnp.where(in_head, kmat, jnp.zeros_like(kmat))
        vh = jnp.where(in_head, vmat, jnp.zeros_like(vmat))
        s = lax.dot_general(q, kh, _NT, preferred_element_type=F32) * (HEAD_DIM ** -0.5)
        p = jnp.exp(s - jnp.max(s, axis=-1, keepdims=True))
        p = p / jnp.sum(p, axis=-1, keepdims=True)
        acc = acc + _dot(p.astype(BF16), vh)
    o_ref[...] = acc.astype(BF16)


def _mem_attn(qm, kv_mem, B, S):
    tm = 512 if S % 512 == 0 else S
    n_s = S // tm
    return pl.pallas_call(
        _mem_attn_kernel,
        grid=(B, n_s),
        in_specs=[pl.BlockSpec((tm, MEM_DIM), lambda b, s: (b * n_s + s, 0)),
                  pl.BlockSpec((1, MEM_LEN, 2 * MEM_DIM), lambda b, s: (b, 0, 0))],
        out_specs=pl.BlockSpec((tm, MEM_DIM), lambda b, s: (b * n_s + s, 0)),
        out_shape=jax.ShapeDtypeStruct((B * S, MEM_DIM), BF16),
        compiler_params=_cparams(("parallel", "parallel")),
        name="mem_attn",
    )(qm, kv_mem)


def _out_proj_kernel(x_ref, o_ref, m_ref, w1_ref, w2_ref, y_ref):
    y_ref[...] = x_ref[...] + _dot(o_ref[...], w1_ref[...]) + _dot(m_ref[...], w2_ref[...])


def _out_proj(x, o, m, w1, w2):
    T = x.shape[0]
    tm = _tiles(T)
    row = lambda n: pl.BlockSpec((tm, n), lambda i: (i, 0))
    full = lambda a: pl.BlockSpec(a.shape, lambda i: (0, 0))
    return pl.pallas_call(
        _out_proj_kernel,
        grid=(T // tm,),
        in_specs=[row(D_MODEL), row(o.shape[1]), row(MEM_DIM), full(w1), full(w2)],
        out_specs=row(D_MODEL),
        out_shape=jax.ShapeDtypeStruct((T, D_MODEL), F32),
        compiler_params=_cparams(("parallel",)),
        name="out_proj",
    )(x, o, m, w1, w2)


def _ffn_kernel(x_ref, g_ref, wg_ref, wu_ref, wd_ref, y_ref, h_sc, acc_sc):
    f = pl.program_id(1)

    @pl.when(f == 0)
    def _():
        h_sc[...] = _rms(x_ref[...], g_ref[...]).astype(BF16)
        acc_sc[...] = jnp.zeros_like(acc_sc)

    h = h_sc[...]
    act = (_silu(_dot(h, wg_ref[...])) * _dot(h, wu_ref[...])).astype(BF16)
    acc_sc[...] += _dot(act, wd_ref[...])

    @pl.when(f == pl.num_programs(1) - 1)
    def _():
        y_ref[...] = x_ref[...] + acc_sc[...]


def _ffn(x, g, wg, wu, wd):
    T = x.shape[0]
    tm = _tiles(T)
    F = wg.shape[1]
    tf = F // 2
    return pl.pallas_call(
        _ffn_kernel,
        grid=(T // tm, F // tf),
        in_specs=[pl.BlockSpec((tm, D_MODEL), lambda i, f: (i, 0)),
                  pl.BlockSpec((1, D_MODEL), lambda i, f: (0, 0)),
                  pl.BlockSpec((D_MODEL, tf), lambda i, f: (0, f)),
                  pl.BlockSpec((D_MODEL, tf), lambda i, f: (0, f)),
                  pl.BlockSpec((tf, D_MODEL), lambda i, f: (f, 0))],
        out_specs=pl.BlockSpec((tm, D_MODEL), lambda i, f: (i, 0)),
        out_shape=jax.ShapeDtypeStruct((T, D_MODEL), F32),
        scratch_shapes=[pltpu.VMEM((tm, D_MODEL), BF16), pltpu.VMEM((tm, D_MODEL), F32)],
        compiler_params=_cparams(("parallel", "arbitrary")),
        name="dense_ffn",
    )(x, g, wg, wu, wd)


def _proj_rope_kernel(x_ref, g_ref, wr_ref, wp_ref, cos_ref, sin_ref, r_ref, p_ref, *, scale,
                      transpose_pass):
    h = _rms(x_ref[...], g_ref[...]).astype(BF16)
    p = _dot(h, wp_ref[...])
    if transpose_pass:
        pt = p.T.astype(BF16)
        rows = DIFF_VDIM + DIFF_ONES_ROWS
        for hh in range(p.shape[1] // DIFF_VDIM):
            p_ref[0, hh * rows:hh * rows + DIFF_VDIM, :] = pt[hh * DIFF_VDIM:(hh + 1) * DIFF_VDIM, :]
            p_ref[0, hh * rows + DIFF_VDIM:(hh + 1) * rows, :] = jnp.ones(
                (DIFF_ONES_ROWS, pt.shape[1]), BF16)
    else:
        p_ref[...] = p.astype(BF16)
    cos = cos_ref[...]
    sin = sin_ref[...]
    lane = lax.broadcasted_iota(jnp.int32, cos.shape, 1)
    first_half = (lane % HEAD_DIM) < (HEAD_DIM // 2)
    y_all = _dot(h, wr_ref[...])
    for j in range(wr_ref.shape[1] // LANES):
        sl = slice(j * LANES, (j + 1) * LANES)
        y = y_all[:, sl]
        partner = jnp.where(first_half, pltpu.roll(y, LANES - HEAD_DIM // 2, 1),
                            pltpu.roll(y, HEAD_DIM // 2, 1))
        r_ref[j] = ((y * cos + partner * sin) * scale).astype(BF16)


def _proj_rope(x, g, w_rope, w_pass, cos, sin, S, scale, transpose_pass):
    T = x.shape[0]
    tm = DIFF_TK if S % DIFF_TK == 0 else S
    n_s = S // tm
    row = lambda n: pl.BlockSpec((tm, n), lambda i: (i, 0))
    full = lambda a: pl.BlockSpec(a.shape, lambda i: (0, 0))
    tab = pl.BlockSpec((tm, LANES), lambda i: (i % n_s, 0))
    nr, np_ = w_rope.shape[1], w_pass.shape[1]
    if transpose_pass:
        nt = np_ // DIFF_VDIM * (DIFF_VDIM + DIFF_ONES_ROWS)
        pass_spec = pl.BlockSpec((1, nt, tm), lambda i: (i, 0, 0))
        pass_shape = jax.ShapeDtypeStruct((T // tm, nt, tm), BF16)
    else:
        pass_spec = row(np_)
        pass_shape = jax.ShapeDtypeStruct((T, np_), BF16)
    return pl.pallas_call(
        functools.partial(_proj_rope_kernel, scale=scale, transpose_pass=transpose_pass),
        grid=(T // tm,),
        in_specs=[row(D_MODEL), full(g), full(w_rope), full(w_pass), tab, tab],
        out_specs=[pl.BlockSpec((nr // LANES, tm, LANES), lambda i: (0, i, 0)), pass_spec],
        out_shape=[jax.ShapeDtypeStruct((nr // LANES, T, LANES), BF16), pass_shape],
        compiler_params=_cparams(("parallel",)),
        name="proj_rope",
    )(x, g, w_rope, w_pass, cos, sin)


def _diff_attn_kernel(q_ref, k_ref, vt_ref, lam_ref, g_ref, o_ref, qs_sc, m_sc, acc_sc, s0_sc, s1_sc,
                      *, tq, tk):
    i = pl.program_id(2)
    q = q_ref[...]
    lane = lax.broadcasted_iota(jnp.int32, q.shape, 1)
    qs_sc[0:tq, :] = jnp.where(lane < HEAD_DIM, q, jnp.zeros_like(q))
    qs_sc[tq:2 * tq, :] = jnp.where(lane >= HEAD_DIM, q, jnp.zeros_like(q))
    m_sc[...] = jnp.full_like(m_sc, -1e30)
    acc_sc[...] = jnp.zeros_like(acc_sc)

    def scores(j, s_ref):
        kj = k_ref[pl.ds(pl.multiple_of(j * tk, tk), tk), :]
        s_ref[...] = lax.dot_general(kj, qs_sc[...], _NT, preferred_element_type=F32)

    def update(j, s_ref, masked):
        s = s_ref[...]
        if masked:
            kpos = j * tk + lax.broadcasted_iota(jnp.int32, s.shape, 0)
            qpos = i * tq + lax.broadcasted_iota(jnp.int32, s.shape, 1) % tq
            s = jnp.where(kpos <= qpos, s, -1e30)
        m_prev = m_sc[...]
        m_new = jnp.maximum(m_prev, jnp.max(s, axis=0, keepdims=True))
        alpha = jnp.exp2(m_prev - m_new)
        p = jnp.exp2(s - m_new).astype(BF16)
        acc_sc[...] = alpha * acc_sc[...] + _dot(vt_ref[j], p)
        m_sc[...] = m_new

    n_full = (i * tq) // tk
    scores(0, s0_sc)

    def body(jj, carry):
        a = 2 * jj
        scores(a + 1, s1_sc)
        update(a, s0_sc, False)
        scores(a + 2, s0_sc)
        update(a + 1, s1_sc, False)
        return carry

    lax.fori_loop(0, n_full // 2, body, 0)

    @pl.when(n_full % 2 == 1)
    def _():
        scores(n_full, s1_sc)
        update(n_full - 1, s0_sc, False)
        update(n_full, s1_sc, True)

    @pl.when(n_full % 2 == 0)
    def _():
        update(n_full, s0_sc, True)

    lam_rows = lam_ref[...]
    dot1 = jnp.sum(lam_rows[0:1, :] * lam_rows[1:2, :], axis=-1, keepdims=True)
    dot2 = jnp.sum(lam_rows[2:3, :] * lam_rows[3:4, :], axis=-1, keepdims=True)
    lam = jnp.exp(dot1) - jnp.exp(dot2) + LAMBDA_INIT_L1
    dv = DIFF_VDIM
    ot = (acc_sc[0:dv, 0:tq] / acc_sc[dv:dv + 1, 0:tq]
          - lam * (acc_sc[0:dv, tq:2 * tq] / acc_sc[dv:dv + 1, tq:2 * tq]))
    ot = ot * lax.rsqrt(jnp.mean(ot * ot, axis=0, keepdims=True) + EPS)
    o_ref[...] = ((ot.T * g_ref[...]) * (1.0 - LAMBDA_INIT_L1)).astype(BF16)


def _diff_attn(q, k, vt, lam_rows, g_subln, B, S):
    tk = vt.shape[2]
    tq = DIFF_TQ if S % DIFF_TQ == 0 else S
    assert tq <= tk and tk % tq == 0 and S % tk == 0
    n_q = S // tq
    n_k = S // tk
    rows = DIFF_VDIM + DIFF_ONES_ROWS
    return pl.pallas_call(
        functools.partial(_diff_attn_kernel, tq=tq, tk=tk),
        grid=(B, DIFF_HEADS, n_q),
        in_specs=[pl.BlockSpec((None, tq, DIFF_VDIM), lambda b, h, i: (h, b * n_q + i, 0)),
                  pl.BlockSpec((None, S, DIFF_VDIM), lambda b, h, i: (h, b, 0)),
                  pl.BlockSpec((n_k, rows, tk), lambda b, h, i: (b, h, 0)),
                  pl.BlockSpec(lam_rows.shape, lambda b, h, i: (0, 0)),
                  pl.BlockSpec((1, DIFF_VDIM), lambda b, h, i: (0, 0))],
        out_specs=pl.BlockSpec((tq, DIFF_VDIM), lambda b, h, i: (b * n_q + i, h)),
        out_shape=jax.ShapeDtypeStruct((B * S, MIX_DIM), BF16),
        scratch_shapes=[pltpu.VMEM((2 * tq, DIFF_VDIM), BF16), pltpu.VMEM((1, 2 * tq), F32),
                        pltpu.VMEM((rows, 2 * tq), F32), pltpu.VMEM((tk, 2 * tq), F32),
                        pltpu.VMEM((tk, 2 * tq), F32)],
        compiler_params=_cparams(("parallel", "parallel", "arbitrary")),
        name="diff_attn",
    )(q, k, vt, lam_rows, g_subln)


_R_E0, _R_E1, _R_RANK0, _R_RANK1, _R_G0, _R_G1 = range(6)


def _router_kernel(x_ref, g_ref, wr_ref, h_ref, info_ref, cnt_ref, run_sc):
    @pl.when(pl.program_id(0) == 0)
    def _():
        run_sc[...] = jnp.zeros_like(run_sc)

    h = _rms(x_ref[...], g_ref[...])
    h_ref[...] = h
    logits = jnp.dot(h, wr_ref[...], precision=lax.Precision.HIGHEST, preferred_element_type=F32)
    tm = logits.shape[0]
    lane = lax.broadcasted_iota(jnp.int32, logits.shape, 1).astype(F32)
    neg = -jnp.inf
    lg = jnp.where(lane < N_EXPERTS, logits, neg)
    m1 = jnp.max(lg, axis=-1, keepdims=True)
    i1 = jnp.min(jnp.where(lg == m1, lane, float(LANES)), axis=-1, keepdims=True)
    lg2 = jnp.where(lane == i1, neg, lg)
    m2 = jnp.max(lg2, axis=-1, keepdims=True)
    i2 = jnp.min(jnp.where(lg2 == m2, lane, float(LANES)), axis=-1, keepdims=True)
    e = jnp.exp(m2 - m1)
    g0 = 1.0 / (1.0 + e)
    g1 = e / (1.0 + e)
    onehot = jnp.where((lane == i1) | (lane == i2), 1.0, 0.0)
    r = lax.broadcasted_iota(jnp.int32, (tm, tm), 0)
    c = lax.broadcasted_iota(jnp.int32, (tm, tm), 1)
    before = _dot(jnp.where(c < r, 1.0, 0.0).astype(BF16), onehot.astype(BF16)) + run_sc[0:1, :]
    rank0 = jnp.sum(jnp.where(lane == i1, before, 0.0), axis=-1, keepdims=True)
    rank1 = jnp.sum(jnp.where(lane == i2, before, 0.0), axis=-1, keepdims=True)
    run_sc[...] = run_sc[...] + jnp.sum(onehot, axis=0, keepdims=True)
    info = jnp.zeros_like(logits)
    for idx, val in ((_R_E0, i1), (_R_E1, i2), (_R_RANK0, rank0), (_R_RANK1, rank1),
                     (_R_G0, g0), (_R_G1, g1)):
        info = jnp.where(lane == idx, val, info)
    info_ref[...] = info
    cnt_ref[...] = run_sc[...]


def _router(x, g, wr):
    T = x.shape[0]
    tm = _tiles(T)
    return pl.pallas_call(
        _router_kernel,
        grid=(T // tm,),
        in_specs=[pl.BlockSpec((tm, D_MODEL), lambda i: (i, 0)),
                  pl.BlockSpec((1, D_MODEL), lambda i: (0, 0)),
                  pl.BlockSpec((D_MODEL, LANES), lambda i: (0, 0))],
        out_specs=[pl.BlockSpec((tm, D_MODEL), lambda i: (i, 0)),
                   pl.BlockSpec((tm, LANES), lambda i: (i, 0)),
                   pl.BlockSpec((SUBLANES, LANES), lambda i: (0, 0))],
        out_shape=[jax.ShapeDtypeStruct((T, D_MODEL), F32), jax.ShapeDtypeStruct((T, LANES), F32),
                   jax.ShapeDtypeStruct((SUBLANES, LANES), F32)],
        scratch_shapes=[pltpu.VMEM((SUBLANES, LANES), F32)],
        compiler_params=_cparams(("arbitrary",)),
        name="moe_router",
    )(x, g, wr)


def _row_copy(src, src_row, dst, dst_row, sem):
    return pltpu.make_async_copy(src.at[pl.ds(src_row, 1)], dst.at[pl.ds(dst_row, 1)], sem)


def _dispatch_kernel(d0_ref, d1_ref, h_ref, xs_in_ref, xs_ref, sem):
    del xs_in_ref
    tm = h_ref.shape[0]

    def issue(r, carry):
        _row_copy(h_ref, r, xs_ref, d0_ref[r], sem).start()
        _row_copy(h_ref, r, xs_ref, d1_ref[r], sem).start()
        return carry

    lax.fori_loop(0, tm, issue, 0, unroll=ROW_DMA_UNROLL)

    def drain(r, carry):
        _row_copy(h_ref, 0, xs_ref, 0, sem).wait()
        _row_copy(h_ref, 0, xs_ref, 0, sem).wait()
        return carry

    lax.fori_loop(0, tm, drain, 0, unroll=ROW_DMA_UNROLL)


def _dispatch(dest0, dest1, h, xs_zero):
    T = h.shape[0]
    tm = _tiles(T)
    smem = pl.BlockSpec((tm,), lambda i: (i,), memory_space=pltpu.SMEM)
    return pl.pallas_call(
        _dispatch_kernel,
        grid=(T // tm,),
        in_specs=[smem, smem, pl.BlockSpec((tm, D_MODEL), lambda i: (i, 0)),
                  pl.BlockSpec(memory_space=pl.ANY)],
        out_specs=pl.BlockSpec(memory_space=pl.ANY),
        out_shape=jax.ShapeDtypeStruct(xs_zero.shape, xs_zero.dtype),
        scratch_shapes=[pltpu.SemaphoreType.DMA],
        input_output_aliases={3: 0},
        compiler_params=_cparams(("arbitrary",)),
        name="moe_dispatch",
    )(dest0, dest1, h, xs_zero)


def _expert_kernel(bexp_ref, nused_ref, xs_ref, wg_ref, wu_ref, wd_ref, y_ref, xb_sc, acc_sc):
    del bexp_ref
    j = pl.program_id(0)
    f = pl.program_id(1)

    @pl.when(j < nused_ref[0])
    def _():
        @pl.when(f == 0)
        def _():
            xb_sc[...] = xs_ref[...].astype(BF16)
            acc_sc[...] = jnp.zeros_like(acc_sc)

        xb = xb_sc[...]
        act = (_silu(_dot(xb, wg_ref[0].astype(BF16)))
               * _dot(xb, wu_ref[0].astype(BF16))).astype(BF16)
        acc_sc[...] += _dot(act, wd_ref[0].astype(BF16))

        @pl.when(f == pl.num_programs(1) - 1)
        def _():
            y_ref[...] = acc_sc[...]

    @pl.when((j >= nused_ref[0]) & (f == 0))
    def _():
        y_ref[...] = jnp.zeros_like(y_ref)


def _experts(block_expert, n_used, xs, wg, wu, wd, rb):
    L = xs.shape[0]
    nb = L // rb
    tf = 512
    nf = FFN_EXPERT // tf

    def blk(j, f, be, nu):
        return jnp.minimum(j, nu[0] - 1)

    def ftile(j, f, be, nu):
        return jnp.where(j < nu[0], f, nf - 1)

    return pl.pallas_call(
        _expert_kernel,
        grid_spec=pltpu.PrefetchScalarGridSpec(
            num_scalar_prefetch=2,
            grid=(nb, nf),
            in_specs=[
                pl.BlockSpec((rb, D_MODEL), lambda j, f, be, nu: (blk(j, f, be, nu), 0)),
                pl.BlockSpec((1, D_MODEL, tf),
                             lambda j, f, be, nu: (be[blk(j, f, be, nu)], 0, ftile(j, f, be, nu))),
                pl.BlockSpec((1, D_MODEL, tf),
                             lambda j, f, be, nu: (be[blk(j, f, be, nu)], 0, ftile(j, f, be, nu))),
                pl.BlockSpec((1, tf, D_MODEL),
                             lambda j, f, be, nu: (be[blk(j, f, be, nu)], ftile(j, f, be, nu), 0)),
            ],
            out_specs=pl.BlockSpec((rb, D_MODEL), lambda j, f, be, nu: (j, 0)),
            scratch_shapes=[pltpu.VMEM((rb, D_MODEL), BF16), pltpu.VMEM((rb, D_MODEL), F32)],
        ),
        out_shape=jax.ShapeDtypeStruct((L, D_MODEL), F32),
        compiler_params=_cparams(("arbitrary", "arbitrary")),
        name="moe_experts",
    )(block_expert, n_used, xs, wg, wu, wd)


def _combine_kernel(d0_ref, d1_ref, x_ref, info_ref, g_ref, y_ref, o_ref, b0_sc, b1_sc, sem):
    tm = x_ref.shape[0]

    def issue(r, carry):
        _row_copy(y_ref, d0_ref[r], b0_sc, r, sem).start()
        _row_copy(y_ref, d1_ref[r], b1_sc, r, sem).start()
        return carry

    lax.fori_loop(0, tm, issue, 0, unroll=ROW_DMA_UNROLL)

    def drain(r, carry):
        _row_copy(y_ref, 0, b0_sc, 0, sem).wait()
        _row_copy(y_ref, 0, b1_sc, 0, sem).wait()
        return carry

    lax.fori_loop(0, tm, drain, 0, unroll=ROW_DMA_UNROLL)
    info = info_ref[...]
    g0 = info[:, _R_G0:_R_G0 + 1]
    g1 = info[:, _R_G1:_R_G1 + 1]
    x = x_ref[...] + (b0_sc[...] * g0 + b1_sc[...] * g1)
    o_ref[...] = _rms(x, g_ref[...])


def _combine(dest0, dest1, x, info, g_final, y):
    T = x.shape[0]
    tm = _tiles(T)
    smem = pl.BlockSpec((tm,), lambda i: (i,), memory_space=pltpu.SMEM)
    return pl.pallas_call(
        _combine_kernel,
        grid=(T // tm,),
        in_specs=[smem, smem, pl.BlockSpec((tm, D_MODEL), lambda i: (i, 0)),
                  pl.BlockSpec((tm, LANES), lambda i: (i, 0)),
                  pl.BlockSpec((1, D_MODEL), lambda i: (0, 0)),
                  pl.BlockSpec(memory_space=pl.ANY)],
        out_specs=pl.BlockSpec((tm, D_MODEL), lambda i: (i, 0)),
        out_shape=jax.ShapeDtypeStruct((T, D_MODEL), F32),
        scratch_shapes=[pltpu.VMEM((tm, D_MODEL), F32), pltpu.VMEM((tm, D_MODEL), F32),
                        pltpu.SemaphoreType.DMA],
        compiler_params=_cparams(("arbitrary",)),
        name="moe_combine",
    )(dest0, dest1, x, info, g_final, y)


def _pad_heads_cols(w, heads, d, d_pad):
    k = w.shape[0]
    return jnp.pad(w.reshape(k, heads, d), ((0, 0), (0, 0), (0, d_pad - d))).reshape(k, heads * d_pad)


def _rope_tables(S):
    half = HEAD_DIM // 2
    inv = ROPE_THETA ** (-jnp.arange(half, dtype=F32) / half)
    ang = jnp.arange(S).astype(F32)[:, None] * inv[None, :]
    reps = LANES // half
    cos = jnp.tile(jnp.cos(ang), (1, reps))
    sign = jnp.tile(jnp.concatenate([-jnp.ones((half,), F32), jnp.ones((half,), F32)]),
                    LANES // HEAD_DIM)
    sin = jnp.tile(jnp.sin(ang), (1, reps)) * sign[None, :]
    return cos, sin


def _moe_block_rows(T):
    return 1024 if (2 * T) % 1024 == 0 else 256


def kernel(x, mem, g_mix, g_ffn, w_mem_kv, w_out, w_in_a, w_gk_up, b_gk, g_gla, g_kv, w_kv, w_in_b, lambda_q1, lambda_k1, lambda_q2, lambda_k2, g_subln, w_dense_gate, w_dense_up, w_dense_down, w_router, w_exp_gate, w_exp_up, w_exp_down, g_final):
    B, S, D = x.shape
    T = B * S
    x0 = x.reshape(T, D)
    mem2 = mem.reshape(B * MEM_LEN, D)
    row = lambda v: v.reshape(1, -1)

    wa = w_in_a[0]
    s0, s1, s2, s3, s4 = (GLA_KDIM, 2 * GLA_KDIM, 2 * GLA_KDIM + MIX_DIM, 2 * GLA_KDIM + 2 * MIX_DIM,
                          2 * GLA_KDIM + 2 * MIX_DIM + GLA_GATE_RANK)
    wq = _pad_heads_cols(wa[:, :s0], GLA_HEADS, GLA_DK, GLA_DK_PAD).astype(BF16)
    wk = _pad_heads_cols(wa[:, s0:s1], GLA_HEADS, GLA_DK, GLA_DK_PAD).astype(BF16)
    wv = _pad_heads_cols(wa[:, s1:s2], GLA_HEADS, GLA_DV, GLA_DV_PAD).astype(BF16)
    wgo = _pad_heads_cols(wa[:, s2:s3], GLA_HEADS, GLA_DV, GLA_DV_PAD).astype(BF16)
    wgl = jnp.pad(wa[:, s3:s4], ((0, 0), (0, LANES - GLA_GATE_RANK))).astype(BF16)
    wqm = wa[:, s4:].astype(BF16)
    wup = jnp.pad(_pad_heads_cols(w_gk_up[0], GLA_HEADS, GLA_DK, GLA_DK_PAD),
                  ((0, LANES - GLA_GATE_RANK), (0, 0))).astype(BF16)
    bgk = _pad_heads_cols(row(b_gk[0]), GLA_HEADS, GLA_DK, GLA_DK_PAD)
    ggla = jnp.pad(row(g_gla[0]), ((0, 0), (0, GLA_DV_PAD - GLA_DV)))
    wo_a = jnp.pad(w_out[0][:MIX_DIM].reshape(GLA_HEADS, GLA_DV, D),
                   ((0, 0), (0, GLA_DV_PAD - GLA_DV), (0, 0))).reshape(GLA_HEADS * GLA_DV_PAD, D)

    q, k, v, go, qm, gk = _in_proj_a(x0, row(g_mix[0]), wq, wk, wv, wgo, wqm, wgl, wup, bgk)
    o = _gla(q, k, gk, v, go, ggla, B, S)
    kv_mem0 = _matmul(mem2, w_mem_kv[0].astype(BF16), BF16).reshape(B, MEM_LEN, 2 * MEM_DIM)
    m = _mem_attn(qm, kv_mem0, B, S)
    x1 = _out_proj(x0, o, m, wo_a.astype(BF16), w_out[0][MIX_DIM:].astype(BF16))
    x2 = _ffn(x1, row(g_ffn[0]), w_dense_gate[0].astype(BF16), w_dense_up[0].astype(BF16),
              w_dense_down[0].astype(BF16))

    cos, sin = _rope_tables(S)
    k_sh, vt_sh = _proj_rope(x2, row(g_kv), w_kv[:, :MIX_DIM].astype(BF16),
                             w_kv[:, MIX_DIM:].astype(BF16), cos, sin, S, 1.0, True)

    q1, qm1 = _proj_rope(x2, row(g_mix[1]), w_in_b[0][:, :MIX_DIM].astype(BF16),
                         w_in_b[0][:, MIX_DIM:].astype(BF16), cos, sin, S,
                         HEAD_DIM ** -0.5 * LOG2_E, False)
    lam_rows = jnp.pad(jnp.stack([lambda_q1[0], lambda_k1[0], lambda_q2[0], lambda_k2[0]]),
                       ((0, SUBLANES - 4), (0, LANES - HEAD_DIM)))
    o1 = _diff_attn(q1, k_sh, vt_sh, lam_rows, row(g_subln[0]), B, S)
    kv_mem1 = _matmul(mem2, w_mem_kv[1].astype(BF16), BF16).reshape(B, MEM_LEN, 2 * MEM_DIM)
    m1 = _mem_attn(qm1, kv_mem1, B, S)
    x3 = _out_proj(x2, o1, m1, w_out[1][:MIX_DIM].astype(BF16), w_out[1][MIX_DIM:].astype(BF16))

    wr = jnp.pad(w_router[0], ((0, 0), (0, LANES - N_EXPERTS)))
    h, info, cnt = _router(x3, row(g_ffn[1]), wr)

    rb = _moe_block_rows(T)
    counts = cnt[0, :N_EXPERTS].astype(jnp.int32)
    padded = (counts + rb - 1) // rb * rb
    pend = jnp.cumsum(padded)
    pstart = pend - padded
    e0 = info[:, _R_E0].astype(jnp.int32)
    e1 = info[:, _R_E1].astype(jnp.int32)
    eids = jnp.arange(N_EXPERTS, dtype=jnp.int32)
    start_of = lambda e: jnp.sum(jnp.where(e[:, None] == eids[None, :], pstart[None, :], 0), axis=1)
    dest0 = start_of(e0) + info[:, _R_RANK0].astype(jnp.int32)
    dest1 = start_of(e1) + info[:, _R_RANK1].astype(jnp.int32)
    nb = (2 * T) // rb + N_EXPERTS
    block_expert = jnp.clip(jnp.sum(pend[None, :] <= (jnp.arange(nb, dtype=jnp.int32) * rb)[:, None],
                                    axis=1), 0, N_EXPERTS - 1).astype(jnp.int32)
    n_used = (pend[-1:] // rb).astype(jnp.int32)

    xs = _dispatch(dest0, dest1, h, jnp.zeros((nb * rb, D), F32))
    ys = _experts(block_expert, n_used, xs, w_exp_gate[0], w_exp_up[0], w_exp_down[0], rb)
    out = _combine(dest0, dest1, x3, info, row(g_final), ys)
    return out.reshape(B, S, D)
```

```python
import functools
import math

import jax
import jax.numpy as jnp
from jax import lax
from jax.experimental import pallas as pl
from jax.experimental.pallas import tpu as pltpu

D_MODEL = 1024
MEM_LEN = 256
HEAD_DIM = 64
MEM_HEADS = 4
MEM_DIM = MEM_HEADS * HEAD_DIM
MIX_DIM = D_MODEL - MEM_DIM
GLA_HEADS = 4
GLA_DV = MIX_DIM // GLA_HEADS
GLA_DK = GLA_DV // 2
GLA_KDIM = GLA_HEADS * GLA_DK
GLA_GATE_RANK = 16
GLA_GATE_NORM = 16.0
GLA_CHUNK = 64
DIFF_HEADS = MIX_DIM // (2 * HEAD_DIM)
DIFF_VDIM = 2 * HEAD_DIM
ROPE_THETA = 10000.0
FFN_DENSE = 2816
N_EXPERTS = 8
FFN_EXPERT = 3584
EPS = 1e-6
LAMBDA_INIT_L1 = 0.8 - 0.6 * math.exp(-0.3 * 1)

LANES = 128
SUBLANES = 8
VMEM_LIMIT_BYTES = 52 * 1024 * 1024

GLA_DK_PAD = LANES
GLA_DV_PAD = 2 * LANES
DIFF_TQ = 512
ROW_DMA_UNROLL = 8
DIFF_TK = 512
DIFF_ONES_ROWS = 16
LOG2_E = 1.4426950408889634

F32 = jnp.float32
BF16 = jnp.bfloat16
_NT = (((1,), (1,)), ((), ()))
_TN = (((0,), (0,)), ((), ()))


def _tiles(T):
    tm = 512 if T % 512 == 0 else T
    return tm


def _cparams(sem):
    return pltpu.CompilerParams(dimension_semantics=sem, vmem_limit_bytes=VMEM_LIMIT_BYTES)


def _rms(xf, g):
    y = xf * lax.rsqrt(jnp.mean(xf * xf, axis=-1, keepdims=True) + EPS)
    return y * g


def _silu(a):
    return a / (1.0 + jnp.exp(-a))


def _dot(a, b):
    return jnp.dot(a, b, preferred_element_type=F32)


def _in_proj_a_kernel(x_ref, g_ref, wq_ref, wk_ref, wv_ref, wgo_ref, wqm_ref, wgl_ref, wup_ref,
                      bgk_ref, q_ref, k_ref, v_ref, go_ref, qm_ref, gk_ref):
    h = _rms(x_ref[...], g_ref[...]).astype(BF16)
    q_ref[...] = _dot(h, wq_ref[...])
    k_ref[...] = _dot(h, wk_ref[...])
    v_ref[...] = _dot(h, wv_ref[...]).astype(BF16)
    go_ref[...] = _dot(h, wgo_ref[...]).astype(BF16)
    qm_ref[...] = _dot(h, wqm_ref[...]).astype(BF16)
    low = _dot(h, wgl_ref[...]).astype(BF16)
    z = _dot(low, wup_ref[...]) + bgk_ref[...]
    log_sig = jnp.minimum(z, 0.0) - jnp.log1p(jnp.exp(-jnp.abs(z)))
    gk_ref[...] = log_sig / GLA_GATE_NORM


def _in_proj_a(x, g, wq, wk, wv, wgo, wqm, wgl, wup, bgk):
    T = x.shape[0]
    tm = _tiles(T)
    row = lambda n: pl.BlockSpec((tm, n), lambda i: (i, 0))
    full = lambda a: pl.BlockSpec(a.shape, lambda i: (0, 0))
    kq, kv_, km = GLA_HEADS * GLA_DK_PAD, GLA_HEADS * GLA_DV_PAD, MEM_DIM
    return pl.pallas_call(
        _in_proj_a_kernel,
        grid=(T // tm,),
        in_specs=[row(D_MODEL)] + [full(a) for a in (g, wq, wk, wv, wgo, wqm, wgl, wup, bgk)],
        out_specs=[row(kq), row(kq), row(kv_), row(kv_), row(km), row(kq)],
        out_shape=[jax.ShapeDtypeStruct((T, kq), F32), jax.ShapeDtypeStruct((T, kq), F32),
                   jax.ShapeDtypeStruct((T, kv_), BF16), jax.ShapeDtypeStruct((T, kv_), BF16),
                   jax.ShapeDtypeStruct((T, km), BF16), jax.ShapeDtypeStruct((T, kq), F32)],
        compiler_params=_cparams(("parallel",)),
        name="in_proj_a",
    )(x, g, wq, wk, wv, wgo, wqm, wgl, wup, bgk)


def _gla_kernel(q_ref, k_ref, gk_ref, v_ref, go_ref, g_ref, o_ref, st_ref, *, n_chunks):
    C = GLA_CHUNK

    @pl.when(pl.program_id(2) == 0)
    def _():
        st_ref[...] = jnp.zeros_like(st_ref)

    r = lax.broadcasted_iota(jnp.int32, (C, C), 0)
    c = lax.broadcasted_iota(jnp.int32, (C, C), 1)
    tril = c <= r
    tril_f = tril.astype(F32)
    g = g_ref[...]
    chunks = [pl.ds(ci * C, C) for ci in range(n_chunks)]
    bs = [jnp.dot(tril_f, gk_ref[sl, :], precision=lax.Precision.HIGHEST,
                  preferred_element_type=F32) for sl in chunks]
    q_in, k_in, q_state, k_state, decay = [], [], [], [], []
    for sl, b in zip(chunks, bs):
        b_mid = b[C // 2 - 1:C // 2, :]
        b_last = b[C - 1:C, :]
        qc = q_ref[sl, :] * (GLA_DK ** -0.5)
        kc = k_ref[sl, :]
        q_in.append((qc * jnp.exp(b - b_mid)).astype(BF16))
        k_in.append((kc * jnp.exp(b_mid - b)).astype(BF16))
        q_state.append((qc * jnp.exp(b)).astype(BF16))
        k_state.append((kc * jnp.exp(b_last - b)).astype(BF16))
        decay.append(jnp.exp(b_last))
    scores = [jnp.where(tril, lax.dot_general(qi, ki, _NT, preferred_element_type=F32), 0.0)
              for qi, ki in zip(q_in, k_in)]
    o_intra = [_dot(sc.astype(BF16), v_ref[sl, :]) for sl, sc in zip(chunks, scores)]
    kv_t = [lax.dot_general(v_ref[sl, :], ks, _TN, preferred_element_type=F32)
            for sl, ks in zip(chunks, k_state)]

    st = st_ref[...]
    for ci in range(n_chunks):
        sl = pl.ds(ci * C, C)
        o = o_intra[ci] + lax.dot_general(q_state[ci], st.astype(BF16), _NT,
                                          preferred_element_type=F32)
        st = st * decay[ci] + kv_t[ci]
        ms = jnp.sum(o * o, axis=-1, keepdims=True) * (1.0 / GLA_DV)
        on = o * lax.rsqrt(ms + EPS) * g
        go = go_ref[sl, :].astype(F32)
        o_ref[sl, :] = (on * _silu(go)).astype(BF16)
    st_ref[...] = st


def _gla(q, k, gk, v, go, g_gla, B, S):
    blk = 512 if S % 512 == 0 else S
    n_s = S // blk
    kmap = lambda b, h, s: (b * n_s + s, h)
    return pl.pallas_call(
        functools.partial(_gla_kernel, n_chunks=blk // GLA_CHUNK),
        grid=(B, GLA_HEADS, n_s),
        in_specs=[pl.BlockSpec((blk, GLA_DK_PAD), kmap), pl.BlockSpec((blk, GLA_DK_PAD), kmap),
                  pl.BlockSpec((blk, GLA_DK_PAD), kmap), pl.BlockSpec((blk, GLA_DV_PAD), kmap),
                  pl.BlockSpec((blk, GLA_DV_PAD), kmap),
                  pl.BlockSpec((1, GLA_DV_PAD), lambda b, h, s: (0, 0))],
        out_specs=pl.BlockSpec((blk, GLA_DV_PAD), kmap),
        out_shape=jax.ShapeDtypeStruct((B * S, GLA_HEADS * GLA_DV_PAD), BF16),
        scratch_shapes=[pltpu.VMEM((GLA_DV_PAD, GLA_DK_PAD), F32)],
        compiler_params=_cparams(("parallel", "parallel", "arbitrary")),
        name="gla",
    )(q, k, gk, v, go, g_gla)


def _matmul_kernel(x_ref, w_ref, o_ref):
    o_ref[...] = _dot(x_ref[...].astype(BF16), w_ref[...]).astype(o_ref.dtype)


def _matmul(x, w, out_dtype):
    M, K = x.shape
    N = w.shape[1]
    tm = 512 if M % 512 == 0 else M
    return pl.pallas_call(
        _matmul_kernel,
        grid=(M // tm,),
        in_specs=[pl.BlockSpec((tm, K), lambda i: (i, 0)), pl.BlockSpec((K, N), lambda i: (0, 0))],
        out_specs=pl.BlockSpec((tm, N), lambda i: (i, 0)),
        out_shape=jax.ShapeDtypeStruct((M, N), out_dtype),
        compiler_params=_cparams(("parallel",)),
        name="mem_kv_proj",
    )(x, w)


def _mem_attn_kernel(q_ref, kv_ref, o_ref):
    q = q_ref[...]
    kv = kv_ref[0]
    kmat = kv[:, :MEM_DIM]
    vmat = kv[:, MEM_DIM:]
    lane = lax.broadcasted_iota(jnp.int32, kmat.shape, 1)
    acc = jnp.zeros((q.shape[0], MEM_DIM), F32)
    for h in range(MEM_HEADS):
        in_head = (lane // HEAD_DIM) == h
        kh = jnp.where(in_head, kmat, jnp.zeros_like(kmat))
        vh = jnp.where(in_head, vmat, jnp.zeros_like(vmat))
        s = lax.dot_general(q, kh, _NT, preferred_element_type=F32) * (HEAD_DIM ** -0.5)
        p = jnp.exp(s - jnp.max(s, axis=-1, keepdims=True))
        p = p / jnp.sum(p, axis=-1, keepdims=True)
        acc = acc + _dot(p.astype(BF16), vh)
    o_ref[...] = acc.astype(BF16)


def _mem_attn(qm, kv_mem, B, S):
    tm = 512 if S % 512 == 0 else S
    n_s = S // tm
    return pl.pallas_call(
        _mem_attn_kernel,
        grid=(B, n_s),
        in_specs=[pl.BlockSpec((tm, MEM_DIM), lambda b, s: (b * n_s + s, 0)),
                  pl.BlockSpec((1, MEM_LEN, 2 * MEM_DIM), lambda b, s: (b, 0, 0))],
        out_specs=pl.BlockSpec((tm, MEM_DIM), lambda b, s: (b * n_s + s, 0)),
        out_shape=jax.ShapeDtypeStruct((B * S, MEM_DIM), BF16),
        compiler_params=_cparams(("parallel", "parallel")),
        name="mem_attn",
    )(qm, kv_mem)


def _out_proj_kernel(x_ref, o_ref, m_ref, w1_ref, w2_ref, y_ref):
    y_ref[...] = x_ref[...] + _dot(o_ref[...], w1_ref[...]) + _dot(m_ref[...], w2_ref[...])


def _out_proj(x, o, m, w1, w2):
    T = x.shape[0]
    tm = _tiles(T)
    row = lambda n: pl.BlockSpec((tm, n), lambda i: (i, 0))
    full = lambda a: pl.BlockSpec(a.shape, lambda i: (0, 0))
    return pl.pallas_call(
        _out_proj_kernel,
        grid=(T // tm,),
        in_specs=[row(D_MODEL), row(o.shape[1]), row(MEM_DIM), full(w1), full(w2)],
        out_specs=row(D_MODEL),
        out_shape=jax.ShapeDtypeStruct((T, D_MODEL), F32),
        compiler_params=_cparams(("parallel",)),
        name="out_proj",
    )(x, o, m, w1, w2)


def _ffn_kernel(x_ref, g_ref, wg_ref, wu_ref, wd_ref, y_ref, h_sc, acc_sc):
    f = pl.program_id(1)

    @pl.when(f == 0)
    def _():
        h_sc[...] = _rms(x_ref[...], g_ref[...]).astype(BF16)
        acc_sc[...] = jnp.zeros_like(acc_sc)

    h = h_sc[...]
    act = (_silu(_dot(h, wg_ref[...])) * _dot(h, wu_ref[...])).astype(BF16)
    acc_sc[...] += _dot(act, wd_ref[...])

    @pl.when(f == pl.num_programs(1) - 1)
    def _():
        y_ref[...] = x_ref[...] + acc_sc[...]


def _ffn(x, g, wg, wu, wd):
    T = x.shape[0]
    tm = _tiles(T)
    F = wg.shape[1]
    tf = F // 2
    return pl.pallas_call(
        _ffn_kernel,
        grid=(T // tm, F // tf),
        in_specs=[pl.BlockSpec((tm, D_MODEL), lambda i, f: (i, 0)),
                  pl.BlockSpec((1, D_MODEL), lambda i, f: (0, 0)),
                  pl.BlockSpec((D_MODEL, tf), lambda i, f: (0, f)),
                  pl.BlockSpec((D_MODEL, tf), lambda i, f: (0, f)),
                  pl.BlockSpec((tf, D_MODEL), lambda i, f: (f, 0))],
        out_specs=pl.BlockSpec((tm, D_MODEL), lambda i, f: (i, 0)),
        out_shape=jax.ShapeDtypeStruct((T, D_MODEL), F32),
        scratch_shapes=[pltpu.VMEM((tm, D_MODEL), BF16), pltpu.VMEM((tm, D_MODEL), F32)],
        compiler_params=_cparams(("parallel", "arbitrary")),
        name="dense_ffn",
    )(x, g, wg, wu, wd)


def _proj_rope_kernel(x_ref, g_ref, wr_ref, wp_ref, cos_ref, sin_ref, r_ref, p_ref, *, scale,
                      transpose_pass):
    h = _rms(x_ref[...], g_ref[...]).astype(BF16)
    p = _dot(h, wp_ref[...])
    if transpose_pass:
        pt = p.T.astype(BF16)
        rows = DIFF_VDIM + DIFF_ONES_ROWS
        for hh in range(p.shape[1] // DIFF_VDIM):
            p_ref[0, hh * rows:hh * rows + DIFF_VDIM, :] = pt[hh * DIFF_VDIM:(hh + 1) * DIFF_VDIM, :]
            p_ref[0, hh * rows + DIFF_VDIM:(hh + 1) * rows, :] = jnp.ones(
                (DIFF_ONES_ROWS, pt.shape[1]), BF16)
    else:
        p_ref[...] = p.astype(BF16)
    cos = cos_ref[...]
    sin = sin_ref[...]
    lane = lax.broadcasted_iota(jnp.int32, cos.shape, 1)
    first_half = (lane % HEAD_DIM) < (HEAD_DIM // 2)
    y_all = _dot(h, wr_ref[...])
    for j in range(wr_ref.shape[1] // LANES):
        sl = slice(j * LANES, (j + 1) * LANES)
        y = y_all[:, sl]
        partner = jnp.where(first_half, pltpu.roll(y, LANES - HEAD_DIM // 2, 1),
                            pltpu.roll(y, HEAD_DIM // 2, 1))
        roped = (y * cos + partner * sin) * scale
        if transpose_pass:
            r_ref[j] = roped.astype(BF16)
        else:
            r_ref[j] = roped.T.astype(BF16)


def _proj_rope(x, g, w_rope, w_pass, cos, sin, S, scale, transpose_pass):
    T = x.shape[0]
    tm = DIFF_TK if S % DIFF_TK == 0 else S
    n_s = S // tm
    row = lambda n: pl.BlockSpec((tm, n), lambda i: (i, 0))
    full = lambda a: pl.BlockSpec(a.shape, lambda i: (0, 0))
    tab = pl.BlockSpec((tm, LANES), lambda i: (i % n_s, 0))
    nr, np_ = w_rope.shape[1], w_pass.shape[1]
    if transpose_pass:
        nt = np_ // DIFF_VDIM * (DIFF_VDIM + DIFF_ONES_ROWS)
        pass_spec = pl.BlockSpec((1, nt, tm), lambda i: (i, 0, 0))
        pass_shape = jax.ShapeDtypeStruct((T // tm, nt, tm), BF16)
        rope_spec = pl.BlockSpec((nr // LANES, tm, LANES), lambda i: (0, i, 0))
        rope_shape = jax.ShapeDtypeStruct((nr // LANES, T, LANES), BF16)
    else:
        pass_spec = row(np_)
        pass_shape = jax.ShapeDtypeStruct((T, np_), BF16)
        rope_spec = pl.BlockSpec((nr // LANES, LANES, tm), lambda i: (0, 0, i))
        rope_shape = jax.ShapeDtypeStruct((nr // LANES, LANES, T), BF16)
    return pl.pallas_call(
        functools.partial(_proj_rope_kernel, scale=scale, transpose_pass=transpose_pass),
        grid=(T // tm,),
        in_specs=[row(D_MODEL), full(g), full(w_rope), full(w_pass), tab, tab],
        out_specs=[rope_spec, pass_spec],
        out_shape=[rope_shape, pass_shape],
        compiler_params=_cparams(("parallel",)),
        name="proj_rope",
    )(x, g, w_rope, w_pass, cos, sin)


def _diff_attn_kernel(q_ref, k_ref, vt_ref, lam_ref, g_ref, o_ref, qs_sc, m_sc, acc_sc, s0_sc, s1_sc,
                      *, tq, tk):
    i = pl.program_id(2)
    qt = q_ref[...]
    feat = lax.broadcasted_iota(jnp.int32, qt.shape, 0)
    qs_sc[:, 0:tq] = jnp.where(feat < HEAD_DIM, qt, jnp.zeros_like(qt))
    qs_sc[:, tq:2 * tq] = jnp.where(feat >= HEAD_DIM, qt, jnp.zeros_like(qt))
    m_sc[...] = jnp.full_like(m_sc, -1e30)
    acc_sc[...] = jnp.zeros_like(acc_sc)

    def scores(j, s_ref):
        kj = k_ref[pl.ds(pl.multiple_of(j * tk, tk), tk), :]
        s_ref[...] = _dot(kj, qs_sc[...])

    def update(j, s_ref, masked):
        s = s_ref[...]
        if masked:
            kpos = j * tk + lax.broadcasted_iota(jnp.int32, s.shape, 0)
            qpos = i * tq + lax.broadcasted_iota(jnp.int32, s.shape, 1) % tq
            s = jnp.where(kpos <= qpos, s, -1e30)
        m_prev = m_sc[...]
        m_new = jnp.maximum(m_prev, jnp.max(s, axis=0, keepdims=True))
        alpha = jnp.exp2(m_prev - m_new)
        p = jnp.exp2(s - m_new).astype(BF16)
        acc_sc[...] = alpha * acc_sc[...] + _dot(vt_ref[j], p)
        m_sc[...] = m_new

    n_full = (i * tq) // tk
    scores(0, s0_sc)

    def body(jj, carry):
        a = 2 * jj
        scores(a + 1, s1_sc)
        update(a, s0_sc, False)
        scores(a + 2, s0_sc)
        update(a + 1, s1_sc, False)
        return carry

    lax.fori_loop(0, n_full // 2, body, 0)

    @pl.when(n_full % 2 == 1)
    def _():
        scores(n_full, s1_sc)
        update(n_full - 1, s0_sc, False)
        update(n_full, s1_sc, True)

    @pl.when(n_full % 2 == 0)
    def _():
        update(n_full, s0_sc, True)

    lam_rows = lam_ref[...]
    dot1 = jnp.sum(lam_rows[0:1, :] * lam_rows[1:2, :], axis=-1, keepdims=True)
    dot2 = jnp.sum(lam_rows[2:3, :] * lam_rows[3:4, :], axis=-1, keepdims=True)
    lam = jnp.exp(dot1) - jnp.exp(dot2) + LAMBDA_INIT_L1
    dv = DIFF_VDIM
    ot = (acc_sc[0:dv, 0:tq] / acc_sc[dv:dv + 1, 0:tq]
          - lam * (acc_sc[0:dv, tq:2 * tq] / acc_sc[dv:dv + 1, tq:2 * tq]))
    ot = ot * lax.rsqrt(jnp.mean(ot * ot, axis=0, keepdims=True) + EPS)
    o_ref[...] = ((ot.T * g_ref[...]) * (1.0 - LAMBDA_INIT_L1)).astype(BF16)


def _diff_attn(q, k, vt, lam_rows, g_subln, B, S):
    tk = vt.shape[2]
    tq = DIFF_TQ if S % DIFF_TQ == 0 else S
    assert tq <= tk and tk % tq == 0 and S % tk == 0
    n_q = S // tq
    n_k = S // tk
    rows = DIFF_VDIM + DIFF_ONES_ROWS
    return pl.pallas_call(
        functools.partial(_diff_attn_kernel, tq=tq, tk=tk),
        grid=(B, DIFF_HEADS, n_q),
        in_specs=[pl.BlockSpec((None, DIFF_VDIM, tq), lambda b, h, i: (h, 0, b * n_q + i)),
                  pl.BlockSpec((None, S, DIFF_VDIM), lambda b, h, i: (h, b, 0)),
                  pl.BlockSpec((n_k, rows, tk), lambda b, h, i: (b, h, 0)),
                  pl.BlockSpec(lam_rows.shape, lambda b, h, i: (0, 0)),
                  pl.BlockSpec((1, DIFF_VDIM), lambda b, h, i: (0, 0))],
        out_specs=pl.BlockSpec((tq, DIFF_VDIM), lambda b, h, i: (b * n_q + i, h)),
        out_shape=jax.ShapeDtypeStruct((B * S, MIX_DIM), BF16),
        scratch_shapes=[pltpu.VMEM((DIFF_VDIM, 2 * tq), BF16), pltpu.VMEM((1, 2 * tq), F32),
                        pltpu.VMEM((rows, 2 * tq), F32), pltpu.VMEM((tk, 2 * tq), F32),
                        pltpu.VMEM((tk, 2 * tq), F32)],
        compiler_params=_cparams(("parallel", "parallel", "arbitrary")),
        name="diff_attn",
    )(q, k, vt, lam_rows, g_subln)


_R_E0, _R_E1, _R_RANK0, _R_RANK1, _R_G0, _R_G1 = range(6)


def _router_kernel(x_ref, g_ref, wr_ref, h_ref, info_ref, cnt_ref, run_sc):
    @pl.when(pl.program_id(0) == 0)
    def _():
        run_sc[...] = jnp.zeros_like(run_sc)

    h = _rms(x_ref[...], g_ref[...])
    h_ref[...] = h
    logits = jnp.dot(h, wr_ref[...], precision=lax.Precision.HIGHEST, preferred_element_type=F32)
    tm = logits.shape[0]
    lane = lax.broadcasted_iota(jnp.int32, logits.shape, 1).astype(F32)
    neg = -jnp.inf
    lg = jnp.where(lane < N_EXPERTS, logits, neg)
    m1 = jnp.max(lg, axis=-1, keepdims=True)
    i1 = jnp.min(jnp.where(lg == m1, lane, float(LANES)), axis=-1, keepdims=True)
    lg2 = jnp.where(lane == i1, neg, lg)
    m2 = jnp.max(lg2, axis=-1, keepdims=True)
    i2 = jnp.min(jnp.where(lg2 == m2, lane, float(LANES)), axis=-1, keepdims=True)
    e = jnp.exp(m2 - m1)
    g0 = 1.0 / (1.0 + e)
    g1 = e / (1.0 + e)
    onehot = jnp.where((lane == i1) | (lane == i2), 1.0, 0.0)
    r = lax.broadcasted_iota(jnp.int32, (tm, tm), 0)
    c = lax.broadcasted_iota(jnp.int32, (tm, tm), 1)
    before = _dot(jnp.where(c < r, 1.0, 0.0).astype(BF16), onehot.astype(BF16)) + run_sc[0:1, :]
    rank0 = jnp.sum(jnp.where(lane == i1, before, 0.0), axis=-1, keepdims=True)
    rank1 = jnp.sum(jnp.where(lane == i2, before, 0.0), axis=-1, keepdims=True)
    run_sc[...] = run_sc[...] + jnp.sum(onehot, axis=0, keepdims=True)
    info = jnp.zeros_like(logits)
    for idx, val in ((_R_E0, i1), (_R_E1, i2), (_R_RANK0, rank0), (_R_RANK1, rank1),
                     (_R_G0, g0), (_R_G1, g1)):
        info = jnp.where(lane == idx, val, info)
    info_ref[...] = info
    cnt_ref[...] = run_sc[...]


def _router(x, g, wr):
    T = x.shape[0]
    tm = _tiles(T)
    return pl.pallas_call(
        _router_kernel,
        grid=(T // tm,),
        in_specs=[pl.BlockSpec((tm, D_MODEL), lambda i: (i, 0)),
                  pl.BlockSpec((1, D_MODEL), lambda i: (0, 0)),
                  pl.BlockSpec((D_MODEL, LANES), lambda i: (0, 0))],
        out_specs=[pl.BlockSpec((tm, D_MODEL), lambda i: (i, 0)),
                   pl.BlockSpec((tm, LANES), lambda i: (i, 0)),
                   pl.BlockSpec((SUBLANES, LANES), lambda i: (0, 0))],
        out_shape=[jax.ShapeDtypeStruct((T, D_MODEL), F32), jax.ShapeDtypeStruct((T, LANES), F32),
                   jax.ShapeDtypeStruct((SUBLANES, LANES), F32)],
        scratch_shapes=[pltpu.VMEM((SUBLANES, LANES), F32)],
        compiler_params=_cparams(("arbitrary",)),
        name="moe_router",
    )(x, g, wr)


def _row_copy(src, src_row, dst, dst_row, sem):
    return pltpu.make_async_copy(src.at[pl.ds(src_row, 1)], dst.at[pl.ds(dst_row, 1)], sem)


def _dispatch_kernel(d0_ref, d1_ref, h_ref, xs_in_ref, xs_ref, sem):
    del xs_in_ref
    tm = h_ref.shape[0]

    def issue(r, carry):
        _row_copy(h_ref, r, xs_ref, d0_ref[r], sem).start()
        _row_copy(h_ref, r, xs_ref, d1_ref[r], sem).start()
        return carry

    lax.fori_loop(0, tm, issue, 0, unroll=ROW_DMA_UNROLL)

    def drain(r, carry):
        _row_copy(h_ref, 0, xs_ref, 0, sem).wait()
        _row_copy(h_ref, 0, xs_ref, 0, sem).wait()
        return carry

    lax.fori_loop(0, tm, drain, 0, unroll=ROW_DMA_UNROLL)


def _dispatch(dest0, dest1, h, xs_zero):
    T = h.shape[0]
    tm = _tiles(T)
    smem = pl.BlockSpec((tm,), lambda i: (i,), memory_space=pltpu.SMEM)
    return pl.pallas_call(
        _dispatch_kernel,
        grid=(T // tm,),
        in_specs=[smem, smem, pl.BlockSpec((tm, D_MODEL), lambda i: (i, 0)),
                  pl.BlockSpec(memory_space=pl.ANY)],
        out_specs=pl.BlockSpec(memory_space=pl.ANY),
        out_shape=jax.ShapeDtypeStruct(xs_zero.shape, xs_zero.dtype),
        scratch_shapes=[pltpu.SemaphoreType.DMA],
        input_output_aliases={3: 0},
        compiler_params=_cparams(("arbitrary",)),
        name="moe_dispatch",
    )(dest0, dest1, h, xs_zero)


def _expert_kernel(bexp_ref, nused_ref, xs_ref, wg_ref, wu_ref, wd_ref, y_ref, xb_sc, acc_sc):
    del bexp_ref
    j = pl.program_id(0)
    f = pl.program_id(1)

    @pl.when(j < nused_ref[0])
    def _():
        @pl.when(f == 0)
        def _():
            xb_sc[...] = xs_ref[...].astype(BF16)
            acc_sc[...] = jnp.zeros_like(acc_sc)

        xb = xb_sc[...]
        act = (_silu(_dot(xb, wg_ref[0].astype(BF16)))
               * _dot(xb, wu_ref[0].astype(BF16))).astype(BF16)
        acc_sc[...] += _dot(act, wd_ref[0].astype(BF16))

        @pl.when(f == pl.num_programs(1) - 1)
        def _():
            y_ref[...] = acc_sc[...]

    @pl.when((j >= nused_ref[0]) & (f == 0))
    def _():
        y_ref[...] = jnp.zeros_like(y_ref)


def _experts(block_expert, n_used, xs, wg, wu, wd, rb):
    L = xs.shape[0]
    nb = L // rb
    tf = 512
    nf = FFN_EXPERT // tf

    def blk(j, f, be, nu):
        return jnp.minimum(j, nu[0] - 1)

    def ftile(j, f, be, nu):
        return jnp.where(j < nu[0], f, nf - 1)

    return pl.pallas_call(
        _expert_kernel,
        grid_spec=pltpu.PrefetchScalarGridSpec(
            num_scalar_prefetch=2,
            grid=(nb, nf),
            in_specs=[
                pl.BlockSpec((rb, D_MODEL), lambda j, f, be, nu: (blk(j, f, be, nu), 0)),
                pl.BlockSpec((1, D_MODEL, tf),
                             lambda j, f, be, nu: (be[blk(j, f, be, nu)], 0, ftile(j, f, be, nu))),
                pl.BlockSpec((1, D_MODEL, tf),
                             lambda j, f, be, nu: (be[blk(j, f, be, nu)], 0, ftile(j, f, be, nu))),
                pl.BlockSpec((1, tf, D_MODEL),
                             lambda j, f, be, nu: (be[blk(j, f, be, nu)], ftile(j, f, be, nu), 0)),
            ],
            out_specs=pl.BlockSpec((rb, D_MODEL), lambda j, f, be, nu: (j, 0)),
            scratch_shapes=[pltpu.VMEM((rb, D_MODEL), BF16), pltpu.VMEM((rb, D_MODEL), F32)],
        ),
        out_shape=jax.ShapeDtypeStruct((L, D_MODEL), F32),
        compiler_params=_cparams(("arbitrary", "arbitrary")),
        name="moe_experts",
    )(block_expert, n_used, xs, wg, wu, wd)


def _combine_kernel(d0_ref, d1_ref, x_ref, info_ref, g_ref, y_ref, o_ref, b0_sc, b1_sc, sem):
    tm = x_ref.shape[0]

    def issue(r, carry):
        _row_copy(y_ref, d0_ref[r], b0_sc, r, sem).start()
        _row_copy(y_ref, d1_ref[r], b1_sc, r, sem).start()
        return carry

    lax.fori_loop(0, tm, issue, 0, unroll=ROW_DMA_UNROLL)

    def drain(r, carry):
        _row_copy(y_ref, 0, b0_sc, 0, sem).wait()
        _row_copy(y_ref, 0, b1_sc, 0, sem).wait()
        return carry

    lax.fori_loop(0, tm, drain, 0, unroll=ROW_DMA_UNROLL)
    info = info_ref[...]
    g0 = info[:, _R_G0:_R_G0 + 1]
    g1 = info[:, _R_G1:_R_G1 + 1]
    x = x_ref[...] + (b0_sc[...] * g0 + b1_sc[...] * g1)
    o_ref[...] = _rms(x, g_ref[...])


def _combine(dest0, dest1, x, info, g_final, y):
    T = x.shape[0]
    tm = _tiles(T)
    smem = pl.BlockSpec((tm,), lambda i: (i,), memory_space=pltpu.SMEM)
    return pl.pallas_call(
        _combine_kernel,
        grid=(T // tm,),
        in_specs=[smem, smem, pl.BlockSpec((tm, D_MODEL), lambda i: (i, 0)),
                  pl.BlockSpec((tm, LANES), lambda i: (i, 0)),
                  pl.BlockSpec((1, D_MODEL), lambda i: (0, 0)),
                  pl.BlockSpec(memory_space=pl.ANY)],
        out_specs=pl.BlockSpec((tm, D_MODEL), lambda i: (i, 0)),
        out_shape=jax.ShapeDtypeStruct((T, D_MODEL), F32),
        scratch_shapes=[pltpu.VMEM((tm, D_MODEL), F32), pltpu.VMEM((tm, D_MODEL), F32),
                        pltpu.SemaphoreType.DMA],
        compiler_params=_cparams(("arbitrary",)),
        name="moe_combine",
    )(dest0, dest1, x, info, g_final, y)


def _pad_heads_cols(w, heads, d, d_pad):
    k = w.shape[0]
    return jnp.pad(w.reshape(k, heads, d), ((0, 0), (0, 0), (0, d_pad - d))).reshape(k, heads * d_pad)


def _rope_tables(S):
    half = HEAD_DIM // 2
    inv = ROPE_THETA ** (-jnp.arange(half, dtype=F32) / half)
    ang = jnp.arange(S).astype(F32)[:, None] * inv[None, :]
    reps = LANES // half
    cos = jnp.tile(jnp.cos(ang), (1, reps))
    sign = jnp.tile(jnp.concatenate([-jnp.ones((half,), F32), jnp.ones((half,), F32)]),
                    LANES // HEAD_DIM)
    sin = jnp.tile(jnp.sin(ang), (1, reps)) * sign[None, :]
    return cos, sin


def _moe_block_rows(T):
    return 1024 if (2 * T) % 1024 == 0 else 256


def kernel(x, mem, g_mix, g_ffn, w_mem_kv, w_out, w_in_a, w_gk_up, b_gk, g_gla, g_kv, w_kv, w_in_b, lambda_q1, lambda_k1, lambda_q2, lambda_k2, g_subln, w_dense_gate, w_dense_up, w_dense_down, w_router, w_exp_gate, w_exp_up, w_exp_down, g_final):
    B, S, D = x.shape
    T = B * S
    x0 = x.reshape(T, D)
    mem2 = mem.reshape(B * MEM_LEN, D)
    row = lambda v: v.reshape(1, -1)

    wa = w_in_a[0]
    s0, s1, s2, s3, s4 = (GLA_KDIM, 2 * GLA_KDIM, 2 * GLA_KDIM + MIX_DIM, 2 * GLA_KDIM + 2 * MIX_DIM,
                          2 * GLA_KDIM + 2 * MIX_DIM + GLA_GATE_RANK)
    wq = _pad_heads_cols(wa[:, :s0], GLA_HEADS, GLA_DK, GLA_DK_PAD).astype(BF16)
    wk = _pad_heads_cols(wa[:, s0:s1], GLA_HEADS, GLA_DK, GLA_DK_PAD).astype(BF16)
    wv = _pad_heads_cols(wa[:, s1:s2], GLA_HEADS, GLA_DV, GLA_DV_PAD).astype(BF16)
    wgo = _pad_heads_cols(wa[:, s2:s3], GLA_HEADS, GLA_DV, GLA_DV_PAD).astype(BF16)
    wgl = jnp.pad(wa[:, s3:s4], ((0, 0), (0, LANES - GLA_GATE_RANK))).astype(BF16)
    wqm = wa[:, s4:].astype(BF16)
    wup = jnp.pad(_pad_heads_cols(w_gk_up[0], GLA_HEADS, GLA_DK, GLA_DK_PAD),
                  ((0, LANES - GLA_GATE_RANK), (0, 0))).astype(BF16)
    bgk = _pad_heads_cols(row(b_gk[0]), GLA_HEADS, GLA_DK, GLA_DK_PAD)
    ggla = jnp.pad(row(g_gla[0]), ((0, 0), (0, GLA_DV_PAD - GLA_DV)))
    wo_a = jnp.pad(w_out[0][:MIX_DIM].reshape(GLA_HEADS, GLA_DV, D),
                   ((0, 0), (0, GLA_DV_PAD - GLA_DV), (0, 0))).reshape(GLA_HEADS * GLA_DV_PAD, D)

    q, k, v, go, qm, gk = _in_proj_a(x0, row(g_mix[0]), wq, wk, wv, wgo, wqm, wgl, wup, bgk)
    o = _gla(q, k, gk, v, go, ggla, B, S)
    kv_mem0 = _matmul(mem2, w_mem_kv[0].astype(BF16), BF16).reshape(B, MEM_LEN, 2 * MEM_DIM)
    m = _mem_attn(qm, kv_mem0, B, S)
    x1 = _out_proj(x0, o, m, wo_a.astype(BF16), w_out[0][MIX_DIM:].astype(BF16))
    x2 = _ffn(x1, row(g_ffn[0]), w_dense_gate[0].astype(BF16), w_dense_up[0].astype(BF16),
              w_dense_down[0].astype(BF16))

    cos, sin = _rope_tables(S)
    k_sh, vt_sh = _proj_rope(x2, row(g_kv), w_kv[:, :MIX_DIM].astype(BF16),
                             w_kv[:, MIX_DIM:].astype(BF16), cos, sin, S, 1.0, True)

    q1, qm1 = _proj_rope(x2, row(g_mix[1]), w_in_b[0][:, :MIX_DIM].astype(BF16),
                         w_in_b[0][:, MIX_DIM:].astype(BF16), cos, sin, S,
                         HEAD_DIM ** -0.5 * LOG2_E, False)
    lam_rows = jnp.pad(jnp.stack([lambda_q1[0], lambda_k1[0], lambda_q2[0], lambda_k2[0]]),
                       ((0, SUBLANES - 4), (0, LANES - HEAD_DIM)))
    o1 = _diff_attn(q1, k_sh, vt_sh, lam_rows, row(g_subln[0]), B, S)
    kv_mem1 = _matmul(mem2, w_mem_kv[1].astype(BF16), BF16).reshape(B, MEM_LEN, 2 * MEM_DIM)
    m1 = _mem_attn(qm1, kv_mem1, B, S)
    x3 = _out_proj(x2, o1, m1, w_out[1][:MIX_DIM].astype(BF16), w_out[1][MIX_DIM:].astype(BF16))

    wr = jnp.pad(w_router[0], ((0, 0), (0, LANES - N_EXPERTS)))
    h, info, cnt = _router(x3, row(g_ffn[1]), wr)

    rb = _moe_block_rows(T)
    counts = cnt[0, :N_EXPERTS].astype(jnp.int32)
    padded = (counts + rb - 1) // rb * rb
    pend = jnp.cumsum(padded)
    pstart = pend - padded
    e0 = info[:, _R_E0].astype(jnp.int32)
    e1 = info[:, _R_E1].astype(jnp.int32)
    eids = jnp.arange(N_EXPERTS, dtype=jnp.int32)
    start_of = lambda e: jnp.sum(jnp.where(e[:, None] == eids[None, :], pstart[None, :], 0), axis=1)
    dest0 = start_of(e0) + info[:, _R_RANK0].astype(jnp.int32)
    dest1 = start_of(e1) + info[:, _R_RANK1].astype(jnp.int32)
    nb = (2 * T) // rb + N_EXPERTS
    block_expert = jnp.clip(jnp.sum(pend[None, :] <= (jnp.arange(nb, dtype=jnp.int32) * rb)[:, None],
                                    axis=1), 0, N_EXPERTS - 1).astype(jnp.int32)
    n_used = (pend[-1:] // rb).astype(jnp.int32)

    xs = _dispatch(dest0, dest1, h, jnp.zeros((nb * rb, D), F32))
    ys = _experts(block_expert, n_used, xs, w_exp_gate[0], w_exp_up[0], w_exp_down[0], rb)
    out = _combine(dest0, dest1, x3, info, row(g_final), ys)
    return out.reshape(B, S, D)
```

```python
import functools
import math

import jax
import jax.numpy as jnp
from jax import lax
from jax.experimental import pallas as pl
from jax.experimental.pallas import tpu as pltpu

D_MODEL = 1024
MEM_LEN = 256
HEAD_DIM = 64
MEM_HEADS = 4
MEM_DIM = MEM_HEADS * HEAD_DIM
MIX_DIM = D_MODEL - MEM_DIM
GLA_HEADS = 4
GLA_DV = MIX_DIM // GLA_HEADS
GLA_DK = GLA_DV // 2
GLA_KDIM = GLA_HEADS * GLA_DK
GLA_GATE_RANK = 16
GLA_GATE_NORM = 16.0
GLA_CHUNK = 64
DIFF_HEADS = MIX_DIM // (2 * HEAD_DIM)
DIFF_VDIM = 2 * HEAD_DIM
ROPE_THETA = 10000.0
FFN_DENSE = 2816
N_EXPERTS = 8
FFN_EXPERT = 3584
EPS = 1e-6
LAMBDA_INIT_L1 = 0.8 - 0.6 * math.exp(-0.3 * 1)

LANES = 128
SUBLANES = 8
VMEM_LIMIT_BYTES = 52 * 1024 * 1024

GLA_DK_PAD = LANES
GLA_DV_PAD = 2 * LANES
ROW_TILE = 512
GLA_BLOCK = 512
EXPERT_ROWS = 1024
EXPERT_ROWS_SMALL = 256
EXPERT_FFN_TILE = 512
DIFF_TQ = 512
ROW_DMA_UNROLL = 8
DIFF_TK = 512
DIFF_ONES_ROWS = 16
LOG2_E = 1.4426950408889634

F32 = jnp.float32
BF16 = jnp.bfloat16
_NT = (((1,), (1,)), ((), ()))
_TN = (((0,), (0,)), ((), ()))


def _tiles(T, tile=ROW_TILE):
    return tile if T % tile == 0 else T


def _cparams(sem):
    return pltpu.CompilerParams(dimension_semantics=sem, vmem_limit_bytes=VMEM_LIMIT_BYTES)


def _rms(xf, g):
    y = xf * lax.rsqrt(jnp.mean(xf * xf, axis=-1, keepdims=True) + EPS)
    return y * g


def _silu(a):
    return a / (1.0 + jnp.exp(-a))


def _dot(a, b):
    return jnp.dot(a, b, preferred_element_type=F32)


def _in_proj_a_kernel(x_ref, g_ref, wq_ref, wk_ref, wv_ref, wgo_ref, wqm_ref, wgl_ref, wup_ref,
                      bgk_ref, q_ref, k_ref, v_ref, go_ref, qm_ref, gk_ref):
    h = _rms(x_ref[...], g_ref[...]).astype(BF16)
    q_ref[...] = _dot(h, wq_ref[...])
    k_ref[...] = _dot(h, wk_ref[...])
    v_ref[...] = _dot(h, wv_ref[...]).astype(BF16)
    go_ref[...] = _dot(h, wgo_ref[...]).astype(BF16)
    qm_ref[...] = _dot(h, wqm_ref[...]).astype(BF16)
    low = _dot(h, wgl_ref[...]).astype(BF16)
    z = _dot(low, wup_ref[...]) + bgk_ref[...]
    log_sig = jnp.minimum(z, 0.0) - jnp.log1p(jnp.exp(-jnp.abs(z)))
    gk_ref[...] = log_sig / GLA_GATE_NORM


def _in_proj_a(x, g, wq, wk, wv, wgo, wqm, wgl, wup, bgk):
    T = x.shape[0]
    tm = _tiles(T)
    row = lambda n: pl.BlockSpec((tm, n), lambda i: (i, 0))
    full = lambda a: pl.BlockSpec(a.shape, lambda i: (0, 0))
    kq, kv_, km = GLA_HEADS * GLA_DK_PAD, GLA_HEADS * GLA_DV_PAD, MEM_DIM
    return pl.pallas_call(
        _in_proj_a_kernel,
        grid=(T // tm,),
        in_specs=[row(D_MODEL)] + [full(a) for a in (g, wq, wk, wv, wgo, wqm, wgl, wup, bgk)],
        out_specs=[row(kq), row(kq), row(kv_), row(kv_), row(km), row(kq)],
        out_shape=[jax.ShapeDtypeStruct((T, kq), F32), jax.ShapeDtypeStruct((T, kq), F32),
                   jax.ShapeDtypeStruct((T, kv_), BF16), jax.ShapeDtypeStruct((T, kv_), BF16),
                   jax.ShapeDtypeStruct((T, km), BF16), jax.ShapeDtypeStruct((T, kq), F32)],
        compiler_params=_cparams(("parallel",)),
        name="in_proj_a",
    )(x, g, wq, wk, wv, wgo, wqm, wgl, wup, bgk)


def _gla_kernel(q_ref, k_ref, gk_ref, v_ref, go_ref, g_ref, o_ref, st_ref, *, n_chunks):
    C = GLA_CHUNK

    @pl.when(pl.program_id(2) == 0)
    def _():
        st_ref[...] = jnp.zeros_like(st_ref)

    r = lax.broadcasted_iota(jnp.int32, (C, C), 0)
    c = lax.broadcasted_iota(jnp.int32, (C, C), 1)
    tril = c <= r
    tril_f = tril.astype(F32)
    g = g_ref[...]
    chunks = [pl.ds(ci * C, C) for ci in range(n_chunks)]
    bs = [jnp.dot(tril_f, gk_ref[sl, :], precision=lax.Precision.HIGHEST,
                  preferred_element_type=F32) for sl in chunks]
    q_in, k_in, q_state, k_state, decay = [], [], [], [], []
    for sl, b in zip(chunks, bs):
        b_mid = b[C // 2 - 1:C // 2, :]
        b_last = b[C - 1:C, :]
        qc = q_ref[sl, :] * (GLA_DK ** -0.5)
        kc = k_ref[sl, :]
        q_in.append((qc * jnp.exp(b - b_mid)).astype(BF16))
        k_in.append((kc * jnp.exp(b_mid - b)).astype(BF16))
        q_state.append((qc * jnp.exp(b)).astype(BF16))
        k_state.append((kc * jnp.exp(b_last - b)).astype(BF16))
        decay.append(jnp.exp(b_last))
    scores = [jnp.where(tril, lax.dot_general(qi, ki, _NT, preferred_element_type=F32), 0.0)
              for qi, ki in zip(q_in, k_in)]
    o_intra = [_dot(sc.astype(BF16), v_ref[sl, :]) for sl, sc in zip(chunks, scores)]
    kv_t = [lax.dot_general(v_ref[sl, :], ks, _TN, preferred_element_type=F32)
            for sl, ks in zip(chunks, k_state)]

    st = st_ref[...]
    for ci in range(n_chunks):
        sl = pl.ds(ci * C, C)
        o = o_intra[ci] + lax.dot_general(q_state[ci], st.astype(BF16), _NT,
                                          preferred_element_type=F32)
        st = st * decay[ci] + kv_t[ci]
        ms = jnp.sum(o * o, axis=-1, keepdims=True) * (1.0 / GLA_DV)
        on = o * lax.rsqrt(ms + EPS) * g
        go = go_ref[sl, :].astype(F32)
        o_ref[sl, :] = (on * _silu(go)).astype(BF16)
    st_ref[...] = st


def _gla(q, k, gk, v, go, g_gla, B, S):
    blk = _tiles(S, GLA_BLOCK)
    n_s = S // blk
    kmap = lambda b, h, s: (b * n_s + s, h)
    return pl.pallas_call(
        functools.partial(_gla_kernel, n_chunks=blk // GLA_CHUNK),
        grid=(B, GLA_HEADS, n_s),
        in_specs=[pl.BlockSpec((blk, GLA_DK_PAD), kmap), pl.BlockSpec((blk, GLA_DK_PAD), kmap),
                  pl.BlockSpec((blk, GLA_DK_PAD), kmap), pl.BlockSpec((blk, GLA_DV_PAD), kmap),
                  pl.BlockSpec((blk, GLA_DV_PAD), kmap),
                  pl.BlockSpec((1, GLA_DV_PAD), lambda b, h, s: (0, 0))],
        out_specs=pl.BlockSpec((blk, GLA_DV_PAD), kmap),
        out_shape=jax.ShapeDtypeStruct((B * S, GLA_HEADS * GLA_DV_PAD), BF16),
        scratch_shapes=[pltpu.VMEM((GLA_DV_PAD, GLA_DK_PAD), F32)],
        compiler_params=_cparams(("parallel", "parallel", "arbitrary")),
        name="gla",
    )(q, k, gk, v, go, g_gla)


def _matmul_kernel(x_ref, w_ref, o_ref):
    o_ref[...] = _dot(x_ref[...].astype(BF16), w_ref[...]).astype(o_ref.dtype)


def _matmul(x, w, out_dtype):
    M, K = x.shape
    N = w.shape[1]
    tm = _tiles(M)
    return pl.pallas_call(
        _matmul_kernel,
        grid=(M // tm,),
        in_specs=[pl.BlockSpec((tm, K), lambda i: (i, 0)), pl.BlockSpec((K, N), lambda i: (0, 0))],
        out_specs=pl.BlockSpec((tm, N), lambda i: (i, 0)),
        out_shape=jax.ShapeDtypeStruct((M, N), out_dtype),
        compiler_params=_cparams(("parallel",)),
        name="mem_kv_proj",
    )(x, w)


def _mem_attn_kernel(q_ref, kv_ref, o_ref):
    q = q_ref[...]
    kv = kv_ref[0]
    kmat = kv[:, :MEM_DIM]
    vmat = kv[:, MEM_DIM:]
    lane = lax.broadcasted_iota(jnp.int32, kmat.shape, 1)
    acc = jnp.zeros((q.shape[0], MEM_DIM), F32)
    for h in range(MEM_HEADS):
        in_head = (lane // HEAD_DIM) == h
        kh = jnp.where(in_head, kmat, jnp.zeros_like(kmat))
        vh = jnp.where(in_head, vmat, jnp.zeros_like(vmat))
        s = lax.dot_general(q, kh, _NT, preferred_element_type=F32) * (HEAD_DIM ** -0.5)
        p = jnp.exp(s - jnp.max(s, axis=-1, keepdims=True))
        p = p / jnp.sum(p, axis=-1, keepdims=True)
        acc = acc + _dot(p.astype(BF16), vh)
    o_ref[...] = acc.astype(BF16)


def _mem_attn(qm, kv_mem, B, S):
    tm = _tiles(S)
    n_s = S // tm
    return pl.pallas_call(
        _mem_attn_kernel,
        grid=(B, n_s),
        in_specs=[pl.BlockSpec((tm, MEM_DIM), lambda b, s: (b * n_s + s, 0)),
                  pl.BlockSpec((1, MEM_LEN, 2 * MEM_DIM), lambda b, s: (b, 0, 0))],
        out_specs=pl.BlockSpec((tm, MEM_DIM), lambda b, s: (b * n_s + s, 0)),
        out_shape=jax.ShapeDtypeStruct((B * S, MEM_DIM), BF16),
        compiler_params=_cparams(("parallel", "parallel")),
        name="mem_attn",
    )(qm, kv_mem)


def _out_proj_kernel(x_ref, o_ref, m_ref, w1_ref, w2_ref, y_ref):
    y_ref[...] = x_ref[...] + _dot(o_ref[...], w1_ref[...]) + _dot(m_ref[...], w2_ref[...])


def _out_proj(x, o, m, w1, w2):
    T = x.shape[0]
    tm = _tiles(T)
    row = lambda n: pl.BlockSpec((tm, n), lambda i: (i, 0))
    full = lambda a: pl.BlockSpec(a.shape, lambda i: (0, 0))
    return pl.pallas_call(
        _out_proj_kernel,
        grid=(T // tm,),
        in_specs=[row(D_MODEL), row(o.shape[1]), row(MEM_DIM), full(w1), full(w2)],
        out_specs=row(D_MODEL),
        out_shape=jax.ShapeDtypeStruct((T, D_MODEL), F32),
        compiler_params=_cparams(("parallel",)),
        name="out_proj",
    )(x, o, m, w1, w2)


def _ffn_kernel(x_ref, g_ref, wg_ref, wu_ref, wd_ref, y_ref, h_sc, acc_sc):
    f = pl.program_id(1)

    @pl.when(f == 0)
    def _():
        h_sc[...] = _rms(x_ref[...], g_ref[...]).astype(BF16)
        acc_sc[...] = jnp.zeros_like(acc_sc)

    h = h_sc[...]
    act = (_silu(_dot(h, wg_ref[...])) * _dot(h, wu_ref[...])).astype(BF16)
    acc_sc[...] += _dot(act, wd_ref[...])

    @pl.when(f == pl.num_programs(1) - 1)
    def _():
        y_ref[...] = x_ref[...] + acc_sc[...]


def _ffn(x, g, wg, wu, wd):
    T = x.shape[0]
    tm = _tiles(T)
    F = wg.shape[1]
    tf = F // 2
    return pl.pallas_call(
        _ffn_kernel,
        grid=(T // tm, F // tf),
        in_specs=[pl.BlockSpec((tm, D_MODEL), lambda i, f: (i, 0)),
                  pl.BlockSpec((1, D_MODEL), lambda i, f: (0, 0)),
                  pl.BlockSpec((D_MODEL, tf), lambda i, f: (0, f)),
                  pl.BlockSpec((D_MODEL, tf), lambda i, f: (0, f)),
                  pl.BlockSpec((tf, D_MODEL), lambda i, f: (f, 0))],
        out_specs=pl.BlockSpec((tm, D_MODEL), lambda i, f: (i, 0)),
        out_shape=jax.ShapeDtypeStruct((T, D_MODEL), F32),
        scratch_shapes=[pltpu.VMEM((tm, D_MODEL), BF16), pltpu.VMEM((tm, D_MODEL), F32)],
        compiler_params=_cparams(("parallel", "arbitrary")),
        name="dense_ffn",
    )(x, g, wg, wu, wd)


def _proj_rope_kernel(x_ref, g_ref, wr_ref, wp_ref, cos_ref, sin_ref, r_ref, p_ref, *, scale,
                      transpose_pass):
    h = _rms(x_ref[...], g_ref[...]).astype(BF16)
    p = _dot(h, wp_ref[...])
    if transpose_pass:
        pt = p.T.astype(BF16)
        rows = DIFF_VDIM + DIFF_ONES_ROWS
        for hh in range(p.shape[1] // DIFF_VDIM):
            p_ref[0, hh * rows:hh * rows + DIFF_VDIM, :] = pt[hh * DIFF_VDIM:(hh + 1) * DIFF_VDIM, :]
            p_ref[0, hh * rows + DIFF_VDIM:(hh + 1) * rows, :] = jnp.ones(
                (DIFF_ONES_ROWS, pt.shape[1]), BF16)
    else:
        p_ref[...] = p.astype(BF16)
    cos = cos_ref[...]
    sin = sin_ref[...]
    lane = lax.broadcasted_iota(jnp.int32, cos.shape, 1)
    first_half = (lane % HEAD_DIM) < (HEAD_DIM // 2)
    y_all = _dot(h, wr_ref[...])
    for j in range(wr_ref.shape[1] // LANES):
        sl = slice(j * LANES, (j + 1) * LANES)
        y = y_all[:, sl]
        partner = jnp.where(first_half, pltpu.roll(y, LANES - HEAD_DIM // 2, 1),
                            pltpu.roll(y, HEAD_DIM // 2, 1))
        roped = (y * cos + partner * sin) * scale
        if transpose_pass:
            r_ref[j] = roped.astype(BF16)
        else:
            r_ref[j] = roped.T.astype(BF16)


def _proj_rope(x, g, w_rope, w_pass, cos, sin, S, scale, transpose_pass):
    T = x.shape[0]
    tm = _tiles(S, DIFF_TK)
    n_s = S // tm
    row = lambda n: pl.BlockSpec((tm, n), lambda i: (i, 0))
    full = lambda a: pl.BlockSpec(a.shape, lambda i: (0, 0))
    tab = pl.BlockSpec((tm, LANES), lambda i: (i % n_s, 0))
    nr, np_ = w_rope.shape[1], w_pass.shape[1]
    if transpose_pass:
        nt = np_ // DIFF_VDIM * (DIFF_VDIM + DIFF_ONES_ROWS)
        pass_spec = pl.BlockSpec((1, nt, tm), lambda i: (i, 0, 0))
        pass_shape = jax.ShapeDtypeStruct((T // tm, nt, tm), BF16)
        rope_spec = pl.BlockSpec((nr // LANES, tm, LANES), lambda i: (0, i, 0))
        rope_shape = jax.ShapeDtypeStruct((nr // LANES, T, LANES), BF16)
    else:
        pass_spec = row(np_)
        pass_shape = jax.ShapeDtypeStruct((T, np_), BF16)
        rope_spec = pl.BlockSpec((nr // LANES, LANES, tm), lambda i: (0, 0, i))
        rope_shape = jax.ShapeDtypeStruct((nr // LANES, LANES, T), BF16)
    return pl.pallas_call(
        functools.partial(_proj_rope_kernel, scale=scale, transpose_pass=transpose_pass),
        grid=(T // tm,),
        in_specs=[row(D_MODEL), full(g), full(w_rope), full(w_pass), tab, tab],
        out_specs=[rope_spec, pass_spec],
        out_shape=[rope_shape, pass_shape],
        compiler_params=_cparams(("parallel",)),
        name="proj_rope",
    )(x, g, w_rope, w_pass, cos, sin)


def _diff_attn_kernel(q_ref, k_ref, vt_ref, lam_ref, g_ref, o_ref, qs_sc, m_sc, acc_sc, s0_sc, s1_sc,
                      *, tq, tk):
    i = pl.program_id(2)
    qt = q_ref[...]
    feat = lax.broadcasted_iota(jnp.int32, qt.shape, 0)
    qs_sc[:, 0:tq] = jnp.where(feat < HEAD_DIM, qt, jnp.zeros_like(qt))
    qs_sc[:, tq:2 * tq] = jnp.where(feat >= HEAD_DIM, qt, jnp.zeros_like(qt))
    m_sc[...] = jnp.full_like(m_sc, -1e30)
    acc_sc[...] = jnp.zeros_like(acc_sc)

    def scores(j, s_ref):
        kj = k_ref[pl.ds(pl.multiple_of(j * tk, tk), tk), :]
        s_ref[...] = _dot(kj, qs_sc[...])

    def update(j, s_ref, masked):
        s = s_ref[...]
        if masked:
            kpos = j * tk + lax.broadcasted_iota(jnp.int32, s.shape, 0)
            qpos = i * tq + lax.broadcasted_iota(jnp.int32, s.shape, 1) % tq
            s = jnp.where(kpos <= qpos, s, -1e30)
        m_prev = m_sc[...]
        m_new = jnp.maximum(m_prev, jnp.max(s, axis=0, keepdims=True))
        alpha = jnp.exp2(m_prev - m_new)
        p = jnp.exp2(s - m_new).astype(BF16)
        acc_sc[...] = alpha * acc_sc[...] + _dot(vt_ref[j], p)
        m_sc[...] = m_new

    n_full = (i * tq) // tk
    scores(0, s0_sc)

    def body(jj, carry):
        a = 2 * jj
        scores(a + 1, s1_sc)
        update(a, s0_sc, False)
        scores(a + 2, s0_sc)
        update(a + 1, s1_sc, False)
        return carry

    lax.fori_loop(0, n_full // 2, body, 0)

    @pl.when(n_full % 2 == 1)
    def _():
        scores(n_full, s1_sc)
        update(n_full - 1, s0_sc, False)
        update(n_full, s1_sc, True)

    @pl.when(n_full % 2 == 0)
    def _():
        update(n_full, s0_sc, True)

    lam_rows = lam_ref[...]
    dot1 = jnp.sum(lam_rows[0:1, :] * lam_rows[1:2, :], axis=-1, keepdims=True)
    dot2 = jnp.sum(lam_rows[2:3, :] * lam_rows[3:4, :], axis=-1, keepdims=True)
    lam = jnp.exp(dot1) - jnp.exp(dot2) + LAMBDA_INIT_L1
    dv = DIFF_VDIM
    ot = (acc_sc[0:dv, 0:tq] / acc_sc[dv:dv + 1, 0:tq]
          - lam * (acc_sc[0:dv, tq:2 * tq] / acc_sc[dv:dv + 1, tq:2 * tq]))
    ot = ot * lax.rsqrt(jnp.mean(ot * ot, axis=0, keepdims=True) + EPS)
    o_ref[...] = ((ot.T * g_ref[...]) * (1.0 - LAMBDA_INIT_L1)).astype(BF16)


def _diff_attn(q, k, vt, lam_rows, g_subln, B, S):
    tk = vt.shape[2]
    tq = _tiles(S, DIFF_TQ)
    assert tq <= tk and tk % tq == 0 and S % tk == 0
    n_q = S // tq
    n_k = S // tk
    rows = DIFF_VDIM + DIFF_ONES_ROWS
    return pl.pallas_call(
        functools.partial(_diff_attn_kernel, tq=tq, tk=tk),
        grid=(B, DIFF_HEADS, n_q),
        in_specs=[pl.BlockSpec((None, DIFF_VDIM, tq), lambda b, h, i: (h, 0, b * n_q + i)),
                  pl.BlockSpec((None, S, DIFF_VDIM), lambda b, h, i: (h, b, 0)),
                  pl.BlockSpec((n_k, rows, tk), lambda b, h, i: (b, h, 0)),
                  pl.BlockSpec(lam_rows.shape, lambda b, h, i: (0, 0)),
                  pl.BlockSpec((1, DIFF_VDIM), lambda b, h, i: (0, 0))],
        out_specs=pl.BlockSpec((tq, DIFF_VDIM), lambda b, h, i: (b * n_q + i, h)),
        out_shape=jax.ShapeDtypeStruct((B * S, MIX_DIM), BF16),
        scratch_shapes=[pltpu.VMEM((DIFF_VDIM, 2 * tq), BF16), pltpu.VMEM((1, 2 * tq), F32),
                        pltpu.VMEM((rows, 2 * tq), F32), pltpu.VMEM((tk, 2 * tq), F32),
                        pltpu.VMEM((tk, 2 * tq), F32)],
        compiler_params=_cparams(("parallel", "parallel", "arbitrary")),
        name="diff_attn",
    )(q, k, vt, lam_rows, g_subln)


_R_E0, _R_E1, _R_RANK0, _R_RANK1, _R_G0, _R_G1 = range(6)


def _router_kernel(x_ref, g_ref, wr_ref, h_ref, info_ref, cnt_ref, run_sc):
    @pl.when(pl.program_id(0) == 0)
    def _():
        run_sc[...] = jnp.zeros_like(run_sc)

    h = _rms(x_ref[...], g_ref[...])
    h_ref[...] = h
    logits = jnp.dot(h, wr_ref[...], precision=lax.Precision.HIGHEST, preferred_element_type=F32)
    tm = logits.shape[0]
    lane = lax.broadcasted_iota(jnp.int32, logits.shape, 1).astype(F32)
    neg = -jnp.inf
    lg = jnp.where(lane < N_EXPERTS, logits, neg)
    m1 = jnp.max(lg, axis=-1, keepdims=True)
    i1 = jnp.min(jnp.where(lg == m1, lane, float(LANES)), axis=-1, keepdims=True)
    lg2 = jnp.where(lane == i1, neg, lg)
    m2 = jnp.max(lg2, axis=-1, keepdims=True)
    i2 = jnp.min(jnp.where(lg2 == m2, lane, float(LANES)), axis=-1, keepdims=True)
    e = jnp.exp(m2 - m1)
    g0 = 1.0 / (1.0 + e)
    g1 = e / (1.0 + e)
    onehot = jnp.where((lane == i1) | (lane == i2), 1.0, 0.0)
    r = lax.broadcasted_iota(jnp.int32, (tm, tm), 0)
    c = lax.broadcasted_iota(jnp.int32, (tm, tm), 1)
    before = _dot(jnp.where(c < r, 1.0, 0.0).astype(BF16), onehot.astype(BF16)) + run_sc[0:1, :]
    rank0 = jnp.sum(jnp.where(lane == i1, before, 0.0), axis=-1, keepdims=True)
    rank1 = jnp.sum(jnp.where(lane == i2, before, 0.0), axis=-1, keepdims=True)
    run_sc[...] = run_sc[...] + jnp.sum(onehot, axis=0, keepdims=True)
    info = jnp.zeros_like(logits)
    for idx, val in ((_R_E0, i1), (_R_E1, i2), (_R_RANK0, rank0), (_R_RANK1, rank1),
                     (_R_G0, g0), (_R_G1, g1)):
        info = jnp.where(lane == idx, val, info)
    info_ref[...] = info
    cnt_ref[...] = run_sc[...]


def _router(x, g, wr):
    T = x.shape[0]
    tm = _tiles(T)
    return pl.pallas_call(
        _router_kernel,
        grid=(T // tm,),
        in_specs=[pl.BlockSpec((tm, D_MODEL), lambda i: (i, 0)),
                  pl.BlockSpec((1, D_MODEL), lambda i: (0, 0)),
                  pl.BlockSpec((D_MODEL, LANES), lambda i: (0, 0))],
        out_specs=[pl.BlockSpec((tm, D_MODEL), lambda i: (i, 0)),
                   pl.BlockSpec((tm, LANES), lambda i: (i, 0)),
                   pl.BlockSpec((SUBLANES, LANES), lambda i: (0, 0))],
        out_shape=[jax.ShapeDtypeStruct((T, D_MODEL), F32), jax.ShapeDtypeStruct((T, LANES), F32),
                   jax.ShapeDtypeStruct((SUBLANES, LANES), F32)],
        scratch_shapes=[pltpu.VMEM((SUBLANES, LANES), F32)],
        compiler_params=_cparams(("arbitrary",)),
        name="moe_router",
    )(x, g, wr)


def _row_copy(src, src_row, dst, dst_row, sem):
    return pltpu.make_async_copy(src.at[pl.ds(src_row, 1)], dst.at[pl.ds(dst_row, 1)], sem)


def _dispatch_kernel(d0_ref, d1_ref, h_ref, xs_in_ref, xs_ref, sem):
    del xs_in_ref
    tm = h_ref.shape[0]

    def issue(r, carry):
        _row_copy(h_ref, r, xs_ref, d0_ref[r], sem).start()
        _row_copy(h_ref, r, xs_ref, d1_ref[r], sem).start()
        return carry

    lax.fori_loop(0, tm, issue, 0, unroll=ROW_DMA_UNROLL)

    def drain(r, carry):
        _row_copy(h_ref, 0, xs_ref, 0, sem).wait()
        _row_copy(h_ref, 0, xs_ref, 0, sem).wait()
        return carry

    lax.fori_loop(0, tm, drain, 0, unroll=ROW_DMA_UNROLL)


def _dispatch(dest0, dest1, h, xs_zero):
    T = h.shape[0]
    tm = _tiles(T)
    smem = pl.BlockSpec((tm,), lambda i: (i,), memory_space=pltpu.SMEM)
    return pl.pallas_call(
        _dispatch_kernel,
        grid=(T // tm,),
        in_specs=[smem, smem, pl.BlockSpec((tm, D_MODEL), lambda i: (i, 0)),
                  pl.BlockSpec(memory_space=pl.ANY)],
        out_specs=pl.BlockSpec(memory_space=pl.ANY),
        out_shape=jax.ShapeDtypeStruct(xs_zero.shape, xs_zero.dtype),
        scratch_shapes=[pltpu.SemaphoreType.DMA],
        input_output_aliases={3: 0},
        compiler_params=_cparams(("arbitrary",)),
        name="moe_dispatch",
    )(dest0, dest1, h, xs_zero)


def _expert_kernel(bexp_ref, nused_ref, xs_ref, wg_ref, wu_ref, wd_ref, y_ref, xb_sc, acc_sc):
    del bexp_ref
    j = pl.program_id(0)
    f = pl.program_id(1)

    @pl.when(j < nused_ref[0])
    def _():
        @pl.when(f == 0)
        def _():
            xb_sc[...] = xs_ref[...].astype(BF16)
            acc_sc[...] = jnp.zeros_like(acc_sc)

        xb = xb_sc[...]
        act = (_silu(_dot(xb, wg_ref[0].astype(BF16)))
               * _dot(xb, wu_ref[0].astype(BF16))).astype(BF16)
        acc_sc[...] += _dot(act, wd_ref[0].astype(BF16))

        @pl.when(f == pl.num_programs(1) - 1)
        def _():
            y_ref[...] = acc_sc[...]

    @pl.when((j >= nused_ref[0]) & (f == 0))
    def _():
        y_ref[...] = jnp.zeros_like(y_ref)


def _experts(block_expert, n_used, xs, wg, wu, wd, rb):
    L = xs.shape[0]
    nb = L // rb
    tf = EXPERT_FFN_TILE
    nf = FFN_EXPERT // tf

    def blk(j, f, be, nu):
        return jnp.minimum(j, nu[0] - 1)

    def ftile(j, f, be, nu):
        return jnp.where(j < nu[0], f, nf - 1)

    return pl.pallas_call(
        _expert_kernel,
        grid_spec=pltpu.PrefetchScalarGridSpec(
            num_scalar_prefetch=2,
            grid=(nb, nf),
            in_specs=[
                pl.BlockSpec((rb, D_MODEL), lambda j, f, be, nu: (blk(j, f, be, nu), 0)),
                pl.BlockSpec((1, D_MODEL, tf),
                             lambda j, f, be, nu: (be[blk(j, f, be, nu)], 0, ftile(j, f, be, nu))),
                pl.BlockSpec((1, D_MODEL, tf),
                             lambda j, f, be, nu: (be[blk(j, f, be, nu)], 0, ftile(j, f, be, nu))),
                pl.BlockSpec((1, tf, D_MODEL),
                             lambda j, f, be, nu: (be[blk(j, f, be, nu)], ftile(j, f, be, nu), 0)),
            ],
            out_specs=pl.BlockSpec((rb, D_MODEL), lambda j, f, be, nu: (j, 0)),
            scratch_shapes=[pltpu.VMEM((rb, D_MODEL), BF16), pltpu.VMEM((rb, D_MODEL), F32)],
        ),
        out_shape=jax.ShapeDtypeStruct((L, D_MODEL), F32),
        compiler_params=_cparams(("arbitrary", "arbitrary")),
        name="moe_experts",
    )(block_expert, n_used, xs, wg, wu, wd)


def _combine_kernel(d0_ref, d1_ref, x_ref, info_ref, g_ref, y_ref, o_ref, b0_sc, b1_sc, sem):
    tm = x_ref.shape[0]

    def issue(r, carry):
        _row_copy(y_ref, d0_ref[r], b0_sc, r, sem).start()
        _row_copy(y_ref, d1_ref[r], b1_sc, r, sem).start()
        return carry

    lax.fori_loop(0, tm, issue, 0, unroll=ROW_DMA_UNROLL)

    def drain(r, carry):
        _row_copy(y_ref, 0, b0_sc, 0, sem).wait()
        _row_copy(y_ref, 0, b1_sc, 0, sem).wait()
        return carry

    lax.fori_loop(0, tm, drain, 0, unroll=ROW_DMA_UNROLL)
    info = info_ref[...]
    g0 = info[:, _R_G0:_R_G0 + 1]
    g1 = info[:, _R_G1:_R_G1 + 1]
    x = x_ref[...] + (b0_sc[...] * g0 + b1_sc[...] * g1)
    o_ref[...] = _rms(x, g_ref[...])


def _combine(dest0, dest1, x, info, g_final, y):
    T = x.shape[0]
    tm = _tiles(T)
    smem = pl.BlockSpec((tm,), lambda i: (i,), memory_space=pltpu.SMEM)
    return pl.pallas_call(
        _combine_kernel,
        grid=(T // tm,),
        in_specs=[smem, smem, pl.BlockSpec((tm, D_MODEL), lambda i: (i, 0)),
                  pl.BlockSpec((tm, LANES), lambda i: (i, 0)),
                  pl.BlockSpec((1, D_MODEL), lambda i: (0, 0)),
                  pl.BlockSpec(memory_space=pl.ANY)],
        out_specs=pl.BlockSpec((tm, D_MODEL), lambda i: (i, 0)),
        out_shape=jax.ShapeDtypeStruct((T, D_MODEL), F32),
        scratch_shapes=[pltpu.VMEM((tm, D_MODEL), F32), pltpu.VMEM((tm, D_MODEL), F32),
                        pltpu.SemaphoreType.DMA],
        compiler_params=_cparams(("arbitrary",)),
        name="moe_combine",
    )(dest0, dest1, x, info, g_final, y)


def _pad_heads_cols(w, heads, d, d_pad):
    k = w.shape[0]
    return jnp.pad(w.reshape(k, heads, d), ((0, 0), (0, 0), (0, d_pad - d))).reshape(k, heads * d_pad)


def _rope_tables(S):
    half = HEAD_DIM // 2
    inv = ROPE_THETA ** (-jnp.arange(half, dtype=F32) / half)
    ang = jnp.arange(S).astype(F32)[:, None] * inv[None, :]
    reps = LANES // half
    cos = jnp.tile(jnp.cos(ang), (1, reps))
    sign = jnp.tile(jnp.concatenate([-jnp.ones((half,), F32), jnp.ones((half,), F32)]),
                    LANES // HEAD_DIM)
    sin = jnp.tile(jnp.sin(ang), (1, reps)) * sign[None, :]
    return cos, sin


def _moe_block_rows(T):
    return EXPERT_ROWS if (2 * T) % EXPERT_ROWS == 0 else EXPERT_ROWS_SMALL


def kernel(x, mem, g_mix, g_ffn, w_mem_kv, w_out, w_in_a, w_gk_up, b_gk, g_gla, g_kv, w_kv, w_in_b, lambda_q1, lambda_k1, lambda_q2, lambda_k2, g_subln, w_dense_gate, w_dense_up, w_dense_down, w_router, w_exp_gate, w_exp_up, w_exp_down, g_final):
    B, S, D = x.shape
    T = B * S
    x0 = x.reshape(T, D)
    mem2 = mem.reshape(B * MEM_LEN, D)
    row = lambda v: v.reshape(1, -1)

    wa = w_in_a[0]
    s0, s1, s2, s3, s4 = (GLA_KDIM, 2 * GLA_KDIM, 2 * GLA_KDIM + MIX_DIM, 2 * GLA_KDIM + 2 * MIX_DIM,
                          2 * GLA_KDIM + 2 * MIX_DIM + GLA_GATE_RANK)
    wq = _pad_heads_cols(wa[:, :s0], GLA_HEADS, GLA_DK, GLA_DK_PAD).astype(BF16)
    wk = _pad_heads_cols(wa[:, s0:s1], GLA_HEADS, GLA_DK, GLA_DK_PAD).astype(BF16)
    wv = _pad_heads_cols(wa[:, s1:s2], GLA_HEADS, GLA_DV, GLA_DV_PAD).astype(BF16)
    wgo = _pad_heads_cols(wa[:, s2:s3], GLA_HEADS, GLA_DV, GLA_DV_PAD).astype(BF16)
    wgl = jnp.pad(wa[:, s3:s4], ((0, 0), (0, LANES - GLA_GATE_RANK))).astype(BF16)
    wqm = wa[:, s4:].astype(BF16)
    wup = jnp.pad(_pad_heads_cols(w_gk_up[0], GLA_HEADS, GLA_DK, GLA_DK_PAD),
                  ((0, LANES - GLA_GATE_RANK), (0, 0))).astype(BF16)
    bgk = _pad_heads_cols(row(b_gk[0]), GLA_HEADS, GLA_DK, GLA_DK_PAD)
    ggla = jnp.pad(row(g_gla[0]), ((0, 0), (0, GLA_DV_PAD - GLA_DV)))
    wo_a = jnp.pad(w_out[0][:MIX_DIM].reshape(GLA_HEADS, GLA_DV, D),
                   ((0, 0), (0, GLA_DV_PAD - GLA_DV), (0, 0))).reshape(GLA_HEADS * GLA_DV_PAD, D)

    q, k, v, go, qm, gk = _in_proj_a(x0, row(g_mix[0]), wq, wk, wv, wgo, wqm, wgl, wup, bgk)
    o = _gla(q, k, gk, v, go, ggla, B, S)
    kv_mem0 = _matmul(mem2, w_mem_kv[0].astype(BF16), BF16).reshape(B, MEM_LEN, 2 * MEM_DIM)
    m = _mem_attn(qm, kv_mem0, B, S)
    x1 = _out_proj(x0, o, m, wo_a.astype(BF16), w_out[0][MIX_DIM:].astype(BF16))
    x2 = _ffn(x1, row(g_ffn[0]), w_dense_gate[0].astype(BF16), w_dense_up[0].astype(BF16),
              w_dense_down[0].astype(BF16))

    cos, sin = _rope_tables(S)
    k_sh, vt_sh = _proj_rope(x2, row(g_kv), w_kv[:, :MIX_DIM].astype(BF16),
                             w_kv[:, MIX_DIM:].astype(BF16), cos, sin, S, 1.0, True)

    q1, qm1 = _proj_rope(x2, row(g_mix[1]), w_in_b[0][:, :MIX_DIM].astype(BF16),
                         w_in_b[0][:, MIX_DIM:].astype(BF16), cos, sin, S,
                         HEAD_DIM ** -0.5 * LOG2_E, False)
    lam_rows = jnp.pad(jnp.stack([lambda_q1[0], lambda_k1[0], lambda_q2[0], lambda_k2[0]]),
                       ((0, SUBLANES - 4), (0, LANES - HEAD_DIM)))
    o1 = _diff_attn(q1, k_sh, vt_sh, lam_rows, row(g_subln[0]), B, S)
    kv_mem1 = _matmul(mem2, w_mem_kv[1].astype(BF16), BF16).reshape(B, MEM_LEN, 2 * MEM_DIM)
    m1 = _mem_attn(qm1, kv_mem1, B, S)
    x3 = _out_proj(x2, o1, m1, w_out[1][:MIX_DIM].astype(BF16), w_out[1][MIX_DIM:].astype(BF16))

    wr = jnp.pad(w_router[0], ((0, 0), (0, LANES - N_EXPERTS)))
    h, info, cnt = _router(x3, row(g_ffn[1]), wr)

    rb = _moe_block_rows(T)
    counts = cnt[0, :N_EXPERTS].astype(jnp.int32)
    padded = (counts + rb - 1) // rb * rb
    pend = jnp.cumsum(padded)
    pstart = pend - padded
    e0 = info[:, _R_E0].astype(jnp.int32)
    e1 = info[:, _R_E1].astype(jnp.int32)
    eids = jnp.arange(N_EXPERTS, dtype=jnp.int32)
    start_of = lambda e: jnp.sum(jnp.where(e[:, None] == eids[None, :], pstart[None, :], 0), axis=1)
    dest0 = start_of(e0) + info[:, _R_RANK0].astype(jnp.int32)
    dest1 = start_of(e1) + info[:, _R_RANK1].astype(jnp.int32)
    nb = (2 * T) // rb + N_EXPERTS
    block_expert = jnp.clip(jnp.sum(pend[None, :] <= (jnp.arange(nb, dtype=jnp.int32) * rb)[:, None],
                                    axis=1), 0, N_EXPERTS - 1).astype(jnp.int32)
    n_used = (pend[-1:] // rb).astype(jnp.int32)

    xs = _dispatch(dest0, dest1, h, jnp.zeros((nb * rb, D), F32))
    ys = _experts(block_expert, n_used, xs, w_exp_gate[0], w_exp_up[0], w_exp_down[0], rb)
    out = _combine(dest0, dest1, x3, info, row(g_final), ys)
    return out.reshape(B, S, D)
```

```python
import functools
import math

import jax
import jax.numpy as jnp
from jax import lax
from jax.experimental import pallas as pl
from jax.experimental.pallas import tpu as pltpu

D_MODEL = 1024
MEM_LEN = 256
HEAD_DIM = 64
MEM_HEADS = 4
MEM_DIM = MEM_HEADS * HEAD_DIM
MIX_DIM = D_MODEL - MEM_DIM
GLA_HEADS = 4
GLA_DV = MIX_DIM // GLA_HEADS
GLA_DK = GLA_DV // 2
GLA_KDIM = GLA_HEADS * GLA_DK
GLA_GATE_RANK = 16
GLA_GATE_NORM = 16.0
GLA_CHUNK = 64
DIFF_HEADS = MIX_DIM // (2 * HEAD_DIM)
DIFF_VDIM = 2 * HEAD_DIM
ROPE_THETA = 10000.0
FFN_DENSE = 2816
N_EXPERTS = 8
FFN_EXPERT = 3584
EPS = 1e-6
LAMBDA_INIT_L1 = 0.8 - 0.6 * math.exp(-0.3 * 1)

LANES = 128
SUBLANES = 8
VMEM_LIMIT_BYTES = 52 * 1024 * 1024

GLA_DK_PAD = LANES
GLA_DV_PAD = 2 * LANES
ROW_TILE = 512
GLA_BLOCK = 512
EXPERT_ROWS = 1024
EXPERT_ROWS_SMALL = 256
EXPERT_FFN_TILE = 512
DIFF_TQ = 512
ROW_DMA_UNROLL = 8
DIFF_TK = 512
DIFF_ONES_ROWS = 16
LOG2_E = 1.4426950408889634

F32 = jnp.float32
BF16 = jnp.bfloat16
_NT = (((1,), (1,)), ((), ()))
_TN = (((0,), (0,)), ((), ()))


def _tiles(T, tile=ROW_TILE):
    return tile if T % tile == 0 else T


def _cparams(sem):
    return pltpu.CompilerParams(dimension_semantics=sem, vmem_limit_bytes=VMEM_LIMIT_BYTES)


def _rms(xf, g):
    y = xf * lax.rsqrt(jnp.mean(xf * xf, axis=-1, keepdims=True) + EPS)
    return y * g


def _silu(a):
    return a / (1.0 + jnp.exp(-a))


def _dot(a, b):
    return jnp.dot(a, b, preferred_element_type=F32)


def _in_proj_a_kernel(x_ref, g_ref, wq_ref, wk_ref, wv_ref, wgo_ref, wqm_ref, wgl_ref, wup_ref,
                      bgk_ref, q_ref, k_ref, v_ref, go_ref, qm_ref, gk_ref):
    h = _rms(x_ref[...], g_ref[...]).astype(BF16)
    q_ref[...] = _dot(h, wq_ref[...])
    k_ref[...] = _dot(h, wk_ref[...])
    v_ref[...] = _dot(h, wv_ref[...]).astype(BF16)
    go_ref[...] = _dot(h, wgo_ref[...]).astype(BF16)
    qm_ref[...] = _dot(h, wqm_ref[...]).astype(BF16)
    low = _dot(h, wgl_ref[...]).astype(BF16)
    z = _dot(low, wup_ref[...]) + bgk_ref[...]
    log_sig = jnp.minimum(z, 0.0) - jnp.log1p(jnp.exp(-jnp.abs(z)))
    gk_ref[...] = log_sig / GLA_GATE_NORM


def _in_proj_a(x, g, wq, wk, wv, wgo, wqm, wgl, wup, bgk):
    T = x.shape[0]
    tm = _tiles(T)
    row = lambda n: pl.BlockSpec((tm, n), lambda i: (i, 0))
    full = lambda a: pl.BlockSpec(a.shape, lambda i: (0, 0))
    kq, kv_, km = GLA_HEADS * GLA_DK_PAD, GLA_HEADS * GLA_DV_PAD, MEM_DIM
    return pl.pallas_call(
        _in_proj_a_kernel,
        grid=(T // tm,),
        in_specs=[row(D_MODEL)] + [full(a) for a in (g, wq, wk, wv, wgo, wqm, wgl, wup, bgk)],
        out_specs=[row(kq), row(kq), row(kv_), row(kv_), row(km), row(kq)],
        out_shape=[jax.ShapeDtypeStruct((T, kq), F32), jax.ShapeDtypeStruct((T, kq), F32),
                   jax.ShapeDtypeStruct((T, kv_), BF16), jax.ShapeDtypeStruct((T, kv_), BF16),
                   jax.ShapeDtypeStruct((T, km), BF16), jax.ShapeDtypeStruct((T, kq), F32)],
        compiler_params=_cparams(("parallel",)),
        name="in_proj_a",
    )(x, g, wq, wk, wv, wgo, wqm, wgl, wup, bgk)


def _gla_kernel(q_ref, k_ref, gk_ref, v_ref, go_ref, g_ref, o_ref, st_ref, *, n_chunks):
    C = GLA_CHUNK

    @pl.when(pl.program_id(2) == 0)
    def _():
        st_ref[...] = jnp.zeros_like(st_ref)

    r = lax.broadcasted_iota(jnp.int32, (C, C), 0)
    c = lax.broadcasted_iota(jnp.int32, (C, C), 1)
    tril = c <= r
    tril_f = tril.astype(F32)
    g = g_ref[...]
    chunks = [pl.ds(ci * C, C) for ci in range(n_chunks)]
    bs = [jnp.dot(tril_f, gk_ref[sl, :], precision=lax.Precision.HIGHEST,
                  preferred_element_type=F32) for sl in chunks]
    q_in, k_in, q_state, k_state, decay = [], [], [], [], []
    for sl, b in zip(chunks, bs):
        b_mid = b[C // 2 - 1:C // 2, :]
        b_last = b[C - 1:C, :]
        qc = q_ref[sl, :] * (GLA_DK ** -0.5)
        kc = k_ref[sl, :]
        q_in.append((qc * jnp.exp(b - b_mid)).astype(BF16))
        k_in.append((kc * jnp.exp(b_mid - b)).astype(BF16))
        q_state.append((qc * jnp.exp(b)).astype(BF16))
        k_state.append((kc * jnp.exp(b_last - b)).astype(BF16))
        decay.append(jnp.exp(b_last))
    scores = [jnp.where(tril, lax.dot_general(qi, ki, _NT, preferred_element_type=F32), 0.0)
              for qi, ki in zip(q_in, k_in)]
    o_intra = [_dot(sc.astype(BF16), v_ref[sl, :]) for sl, sc in zip(chunks, scores)]
    kv_t = [lax.dot_general(v_ref[sl, :], ks, _TN, preferred_element_type=F32)
            for sl, ks in zip(chunks, k_state)]

    st = st_ref[...]
    for ci in range(n_chunks):
        sl = pl.ds(ci * C, C)
        o = o_intra[ci] + lax.dot_general(q_state[ci], st.astype(BF16), _NT,
                                          preferred_element_type=F32)
        st = st * decay[ci] + kv_t[ci]
        ms = jnp.sum(o * o, axis=-1, keepdims=True) * (1.0 / GLA_DV)
        on = o * lax.rsqrt(ms + EPS) * g
        go = go_ref[sl, :].astype(F32)
        o_ref[sl, :] = (on * _silu(go)).astype(BF16)
    st_ref[...] = st


def _gla(q, k, gk, v, go, g_gla, B, S):
    blk = _tiles(S, GLA_BLOCK)
    n_s = S // blk
    kmap = lambda b, h, s: (b * n_s + s, h)
    return pl.pallas_call(
        functools.partial(_gla_kernel, n_chunks=blk // GLA_CHUNK),
        grid=(B, GLA_HEADS, n_s),
        in_specs=[pl.BlockSpec((blk, GLA_DK_PAD), kmap), pl.BlockSpec((blk, GLA_DK_PAD), kmap),
                  pl.BlockSpec((blk, GLA_DK_PAD), kmap), pl.BlockSpec((blk, GLA_DV_PAD), kmap),
                  pl.BlockSpec((blk, GLA_DV_PAD), kmap),
                  pl.BlockSpec((1, GLA_DV_PAD), lambda b, h, s: (0, 0))],
        out_specs=pl.BlockSpec((blk, GLA_DV_PAD), kmap),
        out_shape=jax.ShapeDtypeStruct((B * S, GLA_HEADS * GLA_DV_PAD), BF16),
        scratch_shapes=[pltpu.VMEM((GLA_DV_PAD, GLA_DK_PAD), F32)],
        compiler_params=_cparams(("parallel", "parallel", "arbitrary")),
        name="gla",
    )(q, k, gk, v, go, g_gla)


def _matmul_kernel(x_ref, w_ref, o_ref):
    o_ref[...] = _dot(x_ref[...].astype(BF16), w_ref[...]).astype(o_ref.dtype)


def _matmul(x, w, out_dtype):
    M, K = x.shape
    N = w.shape[1]
    tm = _tiles(M)
    return pl.pallas_call(
        _matmul_kernel,
        grid=(M // tm,),
        in_specs=[pl.BlockSpec((tm, K), lambda i: (i, 0)), pl.BlockSpec((K, N), lambda i: (0, 0))],
        out_specs=pl.BlockSpec((tm, N), lambda i: (i, 0)),
        out_shape=jax.ShapeDtypeStruct((M, N), out_dtype),
        compiler_params=_cparams(("parallel",)),
        name="mem_kv_proj",
    )(x, w)


def _mem_attn_kernel(q_ref, kv_ref, o_ref):
    q = q_ref[...]
    kv = kv_ref[0]
    kmat = kv[:, :MEM_DIM]
    vmat = kv[:, MEM_DIM:]
    lane = lax.broadcasted_iota(jnp.int32, kmat.shape, 1)
    acc = jnp.zeros((q.shape[0], MEM_DIM), F32)
    for h in range(MEM_HEADS):
        in_head = (lane // HEAD_DIM) == h
        kh = jnp.where(in_head, kmat, jnp.zeros_like(kmat))
        vh = jnp.where(in_head, vmat, jnp.zeros_like(vmat))
        s = lax.dot_general(q, kh, _NT, preferred_element_type=F32) * (HEAD_DIM ** -0.5)
        p = jnp.exp(s - jnp.max(s, axis=-1, keepdims=True))
        p = p / jnp.sum(p, axis=-1, keepdims=True)
        acc = acc + _dot(p.astype(BF16), vh)
    o_ref[...] = acc.astype(BF16)


def _mem_attn(qm, kv_mem, B, S):
    tm = _tiles(S)
    n_s = S // tm
    return pl.pallas_call(
        _mem_attn_kernel,
        grid=(B, n_s),
        in_specs=[pl.BlockSpec((tm, MEM_DIM), lambda b, s: (b * n_s + s, 0)),
                  pl.BlockSpec((1, MEM_LEN, 2 * MEM_DIM), lambda b, s: (b, 0, 0))],
        out_specs=pl.BlockSpec((tm, MEM_DIM), lambda b, s: (b * n_s + s, 0)),
        out_shape=jax.ShapeDtypeStruct((B * S, MEM_DIM), BF16),
        compiler_params=_cparams(("parallel", "parallel")),
        name="mem_attn",
    )(qm, kv_mem)


def _out_proj_kernel(x_ref, o_ref, m_ref, w1_ref, w2_ref, y_ref):
    y_ref[...] = x_ref[...] + _dot(o_ref[...], w1_ref[...]) + _dot(m_ref[...], w2_ref[...])


def _out_proj(x, o, m, w1, w2):
    T = x.shape[0]
    tm = _tiles(T)
    row = lambda n: pl.BlockSpec((tm, n), lambda i: (i, 0))
    full = lambda a: pl.BlockSpec(a.shape, lambda i: (0, 0))
    return pl.pallas_call(
        _out_proj_kernel,
        grid=(T // tm,),
        in_specs=[row(D_MODEL), row(o.shape[1]), row(MEM_DIM), full(w1), full(w2)],
        out_specs=row(D_MODEL),
        out_shape=jax.ShapeDtypeStruct((T, D_MODEL), F32),
        compiler_params=_cparams(("parallel",)),
        name="out_proj",
    )(x, o, m, w1, w2)


def _ffn_kernel(x_ref, g_ref, wg_ref, wu_ref, wd_ref, y_ref, h_sc, acc_sc):
    f = pl.program_id(1)

    @pl.when(f == 0)
    def _():
        h_sc[...] = _rms(x_ref[...], g_ref[...]).astype(BF16)
        acc_sc[...] = jnp.zeros_like(acc_sc)

    h = h_sc[...]
    act = (_silu(_dot(h, wg_ref[...])) * _dot(h, wu_ref[...])).astype(BF16)
    acc_sc[...] += _dot(act, wd_ref[...])

    @pl.when(f == pl.num_programs(1) - 1)
    def _():
        y_ref[...] = x_ref[...] + acc_sc[...]


def _ffn(x, g, wg, wu, wd):
    T = x.shape[0]
    tm = _tiles(T)
    F = wg.shape[1]
    tf = F // 2
    return pl.pallas_call(
        _ffn_kernel,
        grid=(T // tm, F // tf),
        in_specs=[pl.BlockSpec((tm, D_MODEL), lambda i, f: (i, 0)),
                  pl.BlockSpec((1, D_MODEL), lambda i, f: (0, 0)),
                  pl.BlockSpec((D_MODEL, tf), lambda i, f: (0, f)),
                  pl.BlockSpec((D_MODEL, tf), lambda i, f: (0, f)),
                  pl.BlockSpec((tf, D_MODEL), lambda i, f: (f, 0))],
        out_specs=pl.BlockSpec((tm, D_MODEL), lambda i, f: (i, 0)),
        out_shape=jax.ShapeDtypeStruct((T, D_MODEL), F32),
        scratch_shapes=[pltpu.VMEM((tm, D_MODEL), BF16), pltpu.VMEM((tm, D_MODEL), F32)],
        compiler_params=_cparams(("parallel", "arbitrary")),
        name="dense_ffn",
    )(x, g, wg, wu, wd)


def _proj_rope_kernel(x_ref, g_ref, wr_ref, wp_ref, cos_ref, sin_ref, r_ref, p_ref, *, scale,
                      transpose_pass):
    h = _rms(x_ref[...], g_ref[...]).astype(BF16)
    p = _dot(h, wp_ref[...])
    if transpose_pass:
        pt = p.T.astype(BF16)
        rows = DIFF_VDIM + DIFF_ONES_ROWS
        for hh in range(p.shape[1] // DIFF_VDIM):
            p_ref[0, hh * rows:hh * rows + DIFF_VDIM, :] = pt[hh * DIFF_VDIM:(hh + 1) * DIFF_VDIM, :]
            p_ref[0, hh * rows + DIFF_VDIM:(hh + 1) * rows, :] = jnp.ones(
                (DIFF_ONES_ROWS, pt.shape[1]), BF16)
    else:
        p_ref[...] = p.astype(BF16)
    cos = cos_ref[...]
    sin = sin_ref[...]
    lane = lax.broadcasted_iota(jnp.int32, cos.shape, 1)
    first_half = (lane % HEAD_DIM) < (HEAD_DIM // 2)
    y_all = _dot(h, wr_ref[...])
    for j in range(wr_ref.shape[1] // LANES):
        sl = slice(j * LANES, (j + 1) * LANES)
        y = y_all[:, sl]
        partner = jnp.where(first_half, pltpu.roll(y, LANES - HEAD_DIM // 2, 1),
                            pltpu.roll(y, HEAD_DIM // 2, 1))
        roped = (y * cos + partner * sin) * scale
        if transpose_pass:
            r_ref[j] = roped.astype(BF16)
        else:
            r_ref[j] = roped.T.astype(BF16)


def _proj_rope(x, g, w_rope, w_pass, cos, sin, S, scale, transpose_pass):
    T = x.shape[0]
    tm = _tiles(S, DIFF_TK)
    n_s = S // tm
    row = lambda n: pl.BlockSpec((tm, n), lambda i: (i, 0))
    full = lambda a: pl.BlockSpec(a.shape, lambda i: (0, 0))
    tab = pl.BlockSpec((tm, LANES), lambda i: (i % n_s, 0))
    nr, np_ = w_rope.shape[1], w_pass.shape[1]
    if transpose_pass:
        nt = np_ // DIFF_VDIM * (DIFF_VDIM + DIFF_ONES_ROWS)
        pass_spec = pl.BlockSpec((1, nt, tm), lambda i: (i, 0, 0))
        pass_shape = jax.ShapeDtypeStruct((T // tm, nt, tm), BF16)
        rope_spec = pl.BlockSpec((nr // LANES, tm, LANES), lambda i: (0, i, 0))
        rope_shape = jax.ShapeDtypeStruct((nr // LANES, T, LANES), BF16)
    else:
        pass_spec = row(np_)
        pass_shape = jax.ShapeDtypeStruct((T, np_), BF16)
        rope_spec = pl.BlockSpec((nr // LANES, LANES, tm), lambda i: (0, 0, i))
        rope_shape = jax.ShapeDtypeStruct((nr // LANES, LANES, T), BF16)
    return pl.pallas_call(
        functools.partial(_proj_rope_kernel, scale=scale, transpose_pass=transpose_pass),
        grid=(T // tm,),
        in_specs=[row(D_MODEL), full(g), full(w_rope), full(w_pass), tab, tab],
        out_specs=[rope_spec, pass_spec],
        out_shape=[rope_shape, pass_shape],
        compiler_params=_cparams(("parallel",)),
        name="proj_rope",
    )(x, g, w_rope, w_pass, cos, sin)


def _diff_attn_kernel(q_ref, k_ref, vt_ref, lam_ref, g_ref, o_ref, qs_sc, m_sc, acc_sc, s0_sc, s1_sc,
                      *, tq, tk):
    i = pl.program_id(2)
    qt = q_ref[...]
    feat = lax.broadcasted_iota(jnp.int32, qt.shape, 0)
    qs_sc[:, 0:tq] = jnp.where(feat < HEAD_DIM, qt, jnp.zeros_like(qt))
    qs_sc[:, tq:2 * tq] = jnp.where(feat >= HEAD_DIM, qt, jnp.zeros_like(qt))
    m_sc[...] = jnp.full_like(m_sc, -1e30)
    acc_sc[...] = jnp.zeros_like(acc_sc)

    def scores(j, s_ref):
        kj = k_ref[pl.ds(pl.multiple_of(j * tk, tk), tk), :]
        s_ref[...] = _dot(kj, qs_sc[...])

    def update(j, s_ref, masked):
        s = s_ref[...]
        if masked:
            kpos = j * tk + lax.broadcasted_iota(jnp.int32, s.shape, 0)
            qpos = i * tq + lax.broadcasted_iota(jnp.int32, s.shape, 1) % tq
            s = jnp.where(kpos <= qpos, s, -1e30)
        m_prev = m_sc[...]
        m_new = jnp.maximum(m_prev, jnp.max(s, axis=0, keepdims=True))
        alpha = jnp.exp2(m_prev - m_new)
        p = jnp.exp2(s - m_new).astype(BF16)
        acc_sc[...] = alpha * acc_sc[...] + _dot(vt_ref[j], p)
        m_sc[...] = m_new

    n_full = (i * tq) // tk
    scores(0, s0_sc)

    def body(jj, carry):
        a = 2 * jj
        scores(a + 1, s1_sc)
        update(a, s0_sc, False)
        scores(a + 2, s0_sc)
        update(a + 1, s1_sc, False)
        return carry

    lax.fori_loop(0, n_full // 2, body, 0)

    @pl.when(n_full % 2 == 1)
    def _():
        scores(n_full, s1_sc)
        update(n_full - 1, s0_sc, False)
        update(n_full, s1_sc, True)

    @pl.when(n_full % 2 == 0)
    def _():
        update(n_full, s0_sc, True)

    lam_rows = lam_ref[...]
    dot1 = jnp.sum(lam_rows[0:1, :] * lam_rows[1:2, :], axis=-1, keepdims=True)
    dot2 = jnp.sum(lam_rows[2:3, :] * lam_rows[3:4, :], axis=-1, keepdims=True)
    lam = jnp.exp(dot1) - jnp.exp(dot2) + LAMBDA_INIT_L1
    dv = DIFF_VDIM
    ot = (acc_sc[0:dv, 0:tq] / acc_sc[dv:dv + 1, 0:tq]
          - lam * (acc_sc[0:dv, tq:2 * tq] / acc_sc[dv:dv + 1, tq:2 * tq]))
    ot = ot * lax.rsqrt(jnp.mean(ot * ot, axis=0, keepdims=True) + EPS)
    o_ref[...] = ((ot.T * g_ref[...]) * (1.0 - LAMBDA_INIT_L1)).astype(BF16)


def _diff_attn(q, k, vt, lam_rows, g_subln, B, S):
    tk = vt.shape[2]
    tq = _tiles(S, DIFF_TQ)
    assert tq <= tk and tk % tq == 0 and S % tk == 0
    n_q = S // tq
    n_k = S // tk
    rows = DIFF_VDIM + DIFF_ONES_ROWS
    return pl.pallas_call(
        functools.partial(_diff_attn_kernel, tq=tq, tk=tk),
        grid=(B, DIFF_HEADS, n_q),
        in_specs=[pl.BlockSpec((None, DIFF_VDIM, tq), lambda b, h, i: (h, 0, b * n_q + i)),
                  pl.BlockSpec((None, S, DIFF_VDIM), lambda b, h, i: (h, b, 0)),
                  pl.BlockSpec((n_k, rows, tk), lambda b, h, i: (b, h, 0)),
                  pl.BlockSpec(lam_rows.shape, lambda b, h, i: (0, 0)),
                  pl.BlockSpec((1, DIFF_VDIM), lambda b, h, i: (0, 0))],
        out_specs=pl.BlockSpec((tq, DIFF_VDIM), lambda b, h, i: (b * n_q + i, h)),
        out_shape=jax.ShapeDtypeStruct((B * S, MIX_DIM), BF16),
        scratch_shapes=[pltpu.VMEM((DIFF_VDIM, 2 * tq), BF16), pltpu.VMEM((1, 2 * tq), F32),
                        pltpu.VMEM((rows, 2 * tq), F32), pltpu.VMEM((tk, 2 * tq), F32),
                        pltpu.VMEM((tk, 2 * tq), F32)],
        compiler_params=_cparams(("parallel", "parallel", "arbitrary")),
        name="diff_attn",
    )(q, k, vt, lam_rows, g_subln)


_R_E0, _R_E1, _R_RANK0, _R_RANK1, _R_G0, _R_G1 = range(6)


def _router_kernel(x_ref, g_ref, wr_ref, h_ref, info_ref, cnt_ref, run_sc):
    @pl.when(pl.program_id(0) == 0)
    def _():
        run_sc[...] = jnp.zeros_like(run_sc)

    h = _rms(x_ref[...], g_ref[...])
    h_ref[...] = h
    logits = jnp.dot(h, wr_ref[...], precision=lax.Precision.HIGHEST, preferred_element_type=F32)
    tm = logits.shape[0]
    lane = lax.broadcasted_iota(jnp.int32, logits.shape, 1).astype(F32)
    neg = -jnp.inf
    lg = jnp.where(lane < N_EXPERTS, logits, neg)
    m1 = jnp.max(lg, axis=-1, keepdims=True)
    i1 = jnp.min(jnp.where(lg == m1, lane, float(LANES)), axis=-1, keepdims=True)
    lg2 = jnp.where(lane == i1, neg, lg)
    m2 = jnp.max(lg2, axis=-1, keepdims=True)
    i2 = jnp.min(jnp.where(lg2 == m2, lane, float(LANES)), axis=-1, keepdims=True)
    e = jnp.exp(m2 - m1)
    g0 = 1.0 / (1.0 + e)
    g1 = e / (1.0 + e)
    onehot = jnp.where((lane == i1) | (lane == i2), 1.0, 0.0)
    r = lax.broadcasted_iota(jnp.int32, (tm, tm), 0)
    c = lax.broadcasted_iota(jnp.int32, (tm, tm), 1)
    before = _dot(jnp.where(c < r, 1.0, 0.0).astype(BF16), onehot.astype(BF16)) + run_sc[0:1, :]
    rank0 = jnp.sum(jnp.where(lane == i1, before, 0.0), axis=-1, keepdims=True)
    rank1 = jnp.sum(jnp.where(lane == i2, before, 0.0), axis=-1, keepdims=True)
    run_sc[...] = run_sc[...] + jnp.sum(onehot, axis=0, keepdims=True)
    info = jnp.zeros_like(logits)
    for idx, val in ((_R_E0, i1), (_R_E1, i2), (_R_RANK0, rank0), (_R_RANK1, rank1),
                     (_R_G0, g0), (_R_G1, g1)):
        info = jnp.where(lane == idx, val, info)
    info_ref[...] = info
    cnt_ref[...] = run_sc[...]


def _router(x, g, wr):
    T = x.shape[0]
    tm = _tiles(T)
    return pl.pallas_call(
        _router_kernel,
        grid=(T // tm,),
        in_specs=[pl.BlockSpec((tm, D_MODEL), lambda i: (i, 0)),
                  pl.BlockSpec((1, D_MODEL), lambda i: (0, 0)),
                  pl.BlockSpec((D_MODEL, LANES), lambda i: (0, 0))],
        out_specs=[pl.BlockSpec((tm, D_MODEL), lambda i: (i, 0)),
                   pl.BlockSpec((tm, LANES), lambda i: (i, 0)),
                   pl.BlockSpec((SUBLANES, LANES), lambda i: (0, 0))],
        out_shape=[jax.ShapeDtypeStruct((T, D_MODEL), F32), jax.ShapeDtypeStruct((T, LANES), F32),
                   jax.ShapeDtypeStruct((SUBLANES, LANES), F32)],
        scratch_shapes=[pltpu.VMEM((SUBLANES, LANES), F32)],
        compiler_params=_cparams(("arbitrary",)),
        name="moe_router",
    )(x, g, wr)


def _row_copy(src, src_row, dst, dst_row, sem):
    return pltpu.make_async_copy(src.at[pl.ds(src_row, 1)], dst.at[pl.ds(dst_row, 1)], sem)


def _dispatch_kernel(d0_ref, d1_ref, h_ref, xs_in_ref, xs_ref, sem):
    del xs_in_ref
    tm = h_ref.shape[0]

    def issue(r, carry):
        _row_copy(h_ref, r, xs_ref, d0_ref[r], sem).start(priority=0)
        _row_copy(h_ref, r, xs_ref, d1_ref[r], sem).start(priority=1)
        return carry

    lax.fori_loop(0, tm, issue, 0, unroll=ROW_DMA_UNROLL)

    def drain(r, carry):
        _row_copy(h_ref, 0, xs_ref, 0, sem).wait()
        _row_copy(h_ref, 0, xs_ref, 0, sem).wait()
        return carry

    lax.fori_loop(0, tm, drain, 0, unroll=ROW_DMA_UNROLL)


def _dispatch(dest0, dest1, h, xs_zero):
    T = h.shape[0]
    tm = _tiles(T)
    smem = pl.BlockSpec((tm,), lambda i: (i,), memory_space=pltpu.SMEM)
    return pl.pallas_call(
        _dispatch_kernel,
        grid=(T // tm,),
        in_specs=[smem, smem, pl.BlockSpec((tm, D_MODEL), lambda i: (i, 0)),
                  pl.BlockSpec(memory_space=pl.ANY)],
        out_specs=pl.BlockSpec(memory_space=pl.ANY),
        out_shape=jax.ShapeDtypeStruct(xs_zero.shape, xs_zero.dtype),
        scratch_shapes=[pltpu.SemaphoreType.DMA],
        input_output_aliases={3: 0},
        compiler_params=_cparams(("arbitrary",)),
        name="moe_dispatch",
    )(dest0, dest1, h, xs_zero)


def _expert_kernel(bexp_ref, nused_ref, xs_ref, wg_ref, wu_ref, wd_ref, y_ref, xb_sc, acc_sc):
    del bexp_ref
    j = pl.program_id(0)
    f = pl.program_id(1)

    @pl.when(j < nused_ref[0])
    def _():
        @pl.when(f == 0)
        def _():
            xb_sc[...] = xs_ref[...].astype(BF16)
            acc_sc[...] = jnp.zeros_like(acc_sc)

        xb = xb_sc[...]
        act = (_silu(_dot(xb, wg_ref[0].astype(BF16)))
               * _dot(xb, wu_ref[0].astype(BF16))).astype(BF16)
        acc_sc[...] += _dot(act, wd_ref[0].astype(BF16))

        @pl.when(f == pl.num_programs(1) - 1)
        def _():
            y_ref[...] = acc_sc[...]

    @pl.when((j >= nused_ref[0]) & (f == 0))
    def _():
        y_ref[...] = jnp.zeros_like(y_ref)


def _experts(block_expert, n_used, xs, wg, wu, wd, rb):
    L = xs.shape[0]
    nb = L // rb
    tf = EXPERT_FFN_TILE
    nf = FFN_EXPERT // tf

    def blk(j, f, be, nu):
        return jnp.minimum(j, nu[0] - 1)

    def ftile(j, f, be, nu):
        return jnp.where(j < nu[0], f, nf - 1)

    return pl.pallas_call(
        _expert_kernel,
        grid_spec=pltpu.PrefetchScalarGridSpec(
            num_scalar_prefetch=2,
            grid=(nb, nf),
            in_specs=[
                pl.BlockSpec((rb, D_MODEL), lambda j, f, be, nu: (blk(j, f, be, nu), 0)),
                pl.BlockSpec((1, D_MODEL, tf),
                             lambda j, f, be, nu: (be[blk(j, f, be, nu)], 0, ftile(j, f, be, nu))),
                pl.BlockSpec((1, D_MODEL, tf),
                             lambda j, f, be, nu: (be[blk(j, f, be, nu)], 0, ftile(j, f, be, nu))),
                pl.BlockSpec((1, tf, D_MODEL),
                             lambda j, f, be, nu: (be[blk(j, f, be, nu)], ftile(j, f, be, nu), 0)),
            ],
            out_specs=pl.BlockSpec((rb, D_MODEL), lambda j, f, be, nu: (j, 0)),
            scratch_shapes=[pltpu.VMEM((rb, D_MODEL), BF16), pltpu.VMEM((rb, D_MODEL), F32)],
        ),
        out_shape=jax.ShapeDtypeStruct((L, D_MODEL), F32),
        compiler_params=_cparams(("arbitrary", "arbitrary")),
        name="moe_experts",
    )(block_expert, n_used, xs, wg, wu, wd)


def _combine_kernel(d0_ref, d1_ref, x_ref, info_ref, g_ref, y_ref, o_ref, b0_sc, b1_sc, sem):
    tm = x_ref.shape[0]

    def issue(r, carry):
        _row_copy(y_ref, d0_ref[r], b0_sc, r, sem).start(priority=0)
        _row_copy(y_ref, d1_ref[r], b1_sc, r, sem).start(priority=1)
        return carry

    lax.fori_loop(0, tm, issue, 0, unroll=ROW_DMA_UNROLL)

    def drain(r, carry):
        _row_copy(y_ref, 0, b0_sc, 0, sem).wait()
        _row_copy(y_ref, 0, b1_sc, 0, sem).wait()
        return carry

    lax.fori_loop(0, tm, drain, 0, unroll=ROW_DMA_UNROLL)
    info = info_ref[...]
    g0 = info[:, _R_G0:_R_G0 + 1]
    g1 = info[:, _R_G1:_R_G1 + 1]
    x = x_ref[...] + (b0_sc[...] * g0 + b1_sc[...] * g1)
    o_ref[...] = _rms(x, g_ref[...])


def _combine(dest0, dest1, x, info, g_final, y):
    T = x.shape[0]
    tm = _tiles(T)
    smem = pl.BlockSpec((tm,), lambda i: (i,), memory_space=pltpu.SMEM)
    return pl.pallas_call(
        _combine_kernel,
        grid=(T // tm,),
        in_specs=[smem, smem, pl.BlockSpec((tm, D_MODEL), lambda i: (i, 0)),
                  pl.BlockSpec((tm, LANES), lambda i: (i, 0)),
                  pl.BlockSpec((1, D_MODEL), lambda i: (0, 0)),
                  pl.BlockSpec(memory_space=pl.ANY)],
        out_specs=pl.BlockSpec((tm, D_MODEL), lambda i: (i, 0)),
        out_shape=jax.ShapeDtypeStruct((T, D_MODEL), F32),
        scratch_shapes=[pltpu.VMEM((tm, D_MODEL), F32), pltpu.VMEM((tm, D_MODEL), F32),
                        pltpu.SemaphoreType.DMA],
        compiler_params=_cparams(("arbitrary",)),
        name="moe_combine",
    )(dest0, dest1, x, info, g_final, y)


def _pad_heads_cols(w, heads, d, d_pad):
    k = w.shape[0]
    return jnp.pad(w.reshape(k, heads, d), ((0, 0), (0, 0), (0, d_pad - d))).reshape(k, heads * d_pad)


def _rope_tables(S):
    half = HEAD_DIM // 2
    inv = ROPE_THETA ** (-jnp.arange(half, dtype=F32) / half)
    ang = jnp.arange(S).astype(F32)[:, None] * inv[None, :]
    reps = LANES // half
    cos = jnp.tile(jnp.cos(ang), (1, reps))
    sign = jnp.tile(jnp.concatenate([-jnp.ones((half,), F32), jnp.ones((half,), F32)]),
                    LANES // HEAD_DIM)
    sin = jnp.tile(jnp.sin(ang), (1, reps)) * sign[None, :]
    return cos, sin


def _moe_block_rows(T):
    return EXPERT_ROWS if (2 * T) % EXPERT_ROWS == 0 else EXPERT_ROWS_SMALL


def kernel(x, mem, g_mix, g_ffn, w_mem_kv, w_out, w_in_a, w_gk_up, b_gk, g_gla, g_kv, w_kv, w_in_b, lambda_q1, lambda_k1, lambda_q2, lambda_k2, g_subln, w_dense_gate, w_dense_up, w_dense_down, w_router, w_exp_gate, w_exp_up, w_exp_down, g_final):
    B, S, D = x.shape
    T = B * S
    x0 = x.reshape(T, D)
    mem2 = mem.reshape(B * MEM_LEN, D)
    row = lambda v: v.reshape(1, -1)

    wa = w_in_a[0]
    s0, s1, s2, s3, s4 = (GLA_KDIM, 2 * GLA_KDIM, 2 * GLA_KDIM + MIX_DIM, 2 * GLA_KDIM + 2 * MIX_DIM,
                          2 * GLA_KDIM + 2 * MIX_DIM + GLA_GATE_RANK)
    wq = _pad_heads_cols(wa[:, :s0], GLA_HEADS, GLA_DK, GLA_DK_PAD).astype(BF16)
    wk = _pad_heads_cols(wa[:, s0:s1], GLA_HEADS, GLA_DK, GLA_DK_PAD).astype(BF16)
    wv = _pad_heads_cols(wa[:, s1:s2], GLA_HEADS, GLA_DV, GLA_DV_PAD).astype(BF16)
    wgo = _pad_heads_cols(wa[:, s2:s3], GLA_HEADS, GLA_DV, GLA_DV_PAD).astype(BF16)
    wgl = jnp.pad(wa[:, s3:s4], ((0, 0), (0, LANES - GLA_GATE_RANK))).astype(BF16)
    wqm = wa[:, s4:].astype(BF16)
    wup = jnp.pad(_pad_heads_cols(w_gk_up[0], GLA_HEADS, GLA_DK, GLA_DK_PAD),
                  ((0, LANES - GLA_GATE_RANK), (0, 0))).astype(BF16)
    bgk = _pad_heads_cols(row(b_gk[0]), GLA_HEADS, GLA_DK, GLA_DK_PAD)
    ggla = jnp.pad(row(g_gla[0]), ((0, 0), (0, GLA_DV_PAD - GLA_DV)))
    wo_a = jnp.pad(w_out[0][:MIX_DIM].reshape(GLA_HEADS, GLA_DV, D),
                   ((0, 0), (0, GLA_DV_PAD - GLA_DV), (0, 0))).reshape(GLA_HEADS * GLA_DV_PAD, D)

    q, k, v, go, qm, gk = _in_proj_a(x0, row(g_mix[0]), wq, wk, wv, wgo, wqm, wgl, wup, bgk)
    o = _gla(q, k, gk, v, go, ggla, B, S)
    kv_mem0 = _matmul(mem2, w_mem_kv[0].astype(BF16), BF16).reshape(B, MEM_LEN, 2 * MEM_DIM)
    m = _mem_attn(qm, kv_mem0, B, S)
    x1 = _out_proj(x0, o, m, wo_a.astype(BF16), w_out[0][MIX_DIM:].astype(BF16))
    x2 = _ffn(x1, row(g_ffn[0]), w_dense_gate[0].astype(BF16), w_dense_up[0].astype(BF16),
              w_dense_down[0].astype(BF16))

    cos, sin = _rope_tables(S)
    k_sh, vt_sh = _proj_rope(x2, row(g_kv), w_kv[:, :MIX_DIM].astype(BF16),
                             w_kv[:, MIX_DIM:].astype(BF16), cos, sin, S, 1.0, True)

    q1, qm1 = _proj_rope(x2, row(g_mix[1]), w_in_b[0][:, :MIX_DIM].astype(BF16),
                         w_in_b[0][:, MIX_DIM:].astype(BF16), cos, sin, S,
                         HEAD_DIM ** -0.5 * LOG2_E, False)
    lam_rows = jnp.pad(jnp.stack([lambda_q1[0], lambda_k1[0], lambda_q2[0], lambda_k2[0]]),
                       ((0, SUBLANES - 4), (0, LANES - HEAD_DIM)))
    o1 = _diff_attn(q1, k_sh, vt_sh, lam_rows, row(g_subln[0]), B, S)
    kv_mem1 = _matmul(mem2, w_mem_kv[1].astype(BF16), BF16).reshape(B, MEM_LEN, 2 * MEM_DIM)
    m1 = _mem_attn(qm1, kv_mem1, B, S)
    x3 = _out_proj(x2, o1, m1, w_out[1][:MIX_DIM].astype(BF16), w_out[1][MIX_DIM:].astype(BF16))

    wr = jnp.pad(w_router[0], ((0, 0), (0, LANES - N_EXPERTS)))
    h, info, cnt = _router(x3, row(g_ffn[1]), wr)

    rb = _moe_block_rows(T)
    counts = cnt[0, :N_EXPERTS].astype(jnp.int32)
    padded = (counts + rb - 1) // rb * rb
    pend = jnp.cumsum(padded)
    pstart = pend - padded
    e0 = info[:, _R_E0].astype(jnp.int32)
    e1 = info[:, _R_E1].astype(jnp.int32)
    eids = jnp.arange(N_EXPERTS, dtype=jnp.int32)
    start_of = lambda e: jnp.sum(jnp.where(e[:, None] == eids[None, :], pstart[None, :], 0), axis=1)
    dest0 = start_of(e0) + info[:, _R_RANK0].astype(jnp.int32)
    dest1 = start_of(e1) + info[:, _R_RANK1].astype(jnp.int32)
    nb = (2 * T) // rb + N_EXPERTS
    block_expert = jnp.clip(jnp.sum(pend[None, :] <= (jnp.arange(nb, dtype=jnp.int32) * rb)[:, None],
                                    axis=1), 0, N_EXPERTS - 1).astype(jnp.int32)
    n_used = (pend[-1:] // rb).astype(jnp.int32)

    xs = _dispatch(dest0, dest1, h, jnp.zeros((nb * rb, D), F32))
    ys = _experts(block_expert, n_used, xs, w_exp_gate[0], w_exp_up[0], w_exp_down[0], rb)
    out = _combine(dest0, dest1, x3, info, row(g_final), ys)
    return out.reshape(B, S, D)
```

```python
import functools
import math

import jax
import jax.numpy as jnp
from jax import lax
from jax.experimental import pallas as pl
from jax.experimental.pallas import tpu as pltpu

D_MODEL = 1024
MEM_LEN = 256
HEAD_DIM = 64
MEM_HEADS = 4
MEM_DIM = MEM_HEADS * HEAD_DIM
MIX_DIM = D_MODEL - MEM_DIM
GLA_HEADS = 4
GLA_DV = MIX_DIM // GLA_HEADS
GLA_DK = GLA_DV // 2
GLA_KDIM = GLA_HEADS * GLA_DK
GLA_GATE_RANK = 16
GLA_GATE_NORM = 16.0
GLA_CHUNK = 64
DIFF_HEADS = MIX_DIM // (2 * HEAD_DIM)
DIFF_VDIM = 2 * HEAD_DIM
ROPE_THETA = 10000.0
FFN_DENSE = 2816
N_EXPERTS = 8
FFN_EXPERT = 3584
EPS = 1e-6
LAMBDA_INIT_L1 = 0.8 - 0.6 * math.exp(-0.3 * 1)

LANES = 128
SUBLANES = 8
VMEM_LIMIT_BYTES = 52 * 1024 * 1024

GLA_DK_PAD = LANES
GLA_DV_PAD = 2 * LANES
ROW_TILE = 512
GLA_BLOCK = 512
EXPERT_ROWS = 1024
EXPERT_ROWS_SMALL = 256
EXPERT_FFN_TILE = 512
DIFF_TQ = 512
ROW_DMA_UNROLL = 8
DIFF_TK = 512
DIFF_ONES_ROWS = 16
LOG2_E = 1.4426950408889634

F32 = jnp.float32
BF16 = jnp.bfloat16
_NT = (((1,), (1,)), ((), ()))
_TN = (((0,), (0,)), ((), ()))


def _tiles(T, tile=ROW_TILE):
    return tile if T % tile == 0 else T


def _cparams(sem):
    return pltpu.CompilerParams(dimension_semantics=sem, vmem_limit_bytes=VMEM_LIMIT_BYTES)


def _rms(xf, g):
    y = xf * lax.rsqrt(jnp.mean(xf * xf, axis=-1, keepdims=True) + EPS)
    return y * g


def _silu(a):
    return a / (1.0 + jnp.exp(-a))


def _dot(a, b):
    return jnp.dot(a, b, preferred_element_type=F32)


def _in_proj_a_kernel(x_ref, g_ref, wq_ref, wk_ref, wv_ref, wgo_ref, wqm_ref, wgl_ref, wup_ref,
                      bgk_ref, q_ref, k_ref, v_ref, go_ref, qm_ref, gk_ref):
    h = _rms(x_ref[...], g_ref[...]).astype(BF16)
    q_ref[...] = _dot(h, wq_ref[...])
    k_ref[...] = _dot(h, wk_ref[...])
    v_ref[...] = _dot(h, wv_ref[...]).astype(BF16)
    go_ref[...] = _dot(h, wgo_ref[...]).astype(BF16)
    qm_ref[...] = _dot(h, wqm_ref[...]).astype(BF16)
    low = _dot(h, wgl_ref[...]).astype(BF16)
    z = _dot(low, wup_ref[...]) + bgk_ref[...]
    log_sig = jnp.minimum(z, 0.0) - jnp.log1p(jnp.exp(-jnp.abs(z)))
    gk_ref[...] = log_sig / GLA_GATE_NORM


def _in_proj_a(x, g, wq, wk, wv, wgo, wqm, wgl, wup, bgk):
    T = x.shape[0]
    tm = _tiles(T)
    row = lambda n: pl.BlockSpec((tm, n), lambda i: (i, 0))
    full = lambda a: pl.BlockSpec(a.shape, lambda i: (0, 0))
    kq, kv_, km = GLA_HEADS * GLA_DK_PAD, GLA_HEADS * GLA_DV_PAD, MEM_DIM
    return pl.pallas_call(
        _in_proj_a_kernel,
        grid=(T // tm,),
        in_specs=[row(D_MODEL)] + [full(a) for a in (g, wq, wk, wv, wgo, wqm, wgl, wup, bgk)],
        out_specs=[row(kq), row(kq), row(kv_), row(kv_), row(km), row(kq)],
        out_shape=[jax.ShapeDtypeStruct((T, kq), F32), jax.ShapeDtypeStruct((T, kq), F32),
                   jax.ShapeDtypeStruct((T, kv_), BF16), jax.ShapeDtypeStruct((T, kv_), BF16),
                   jax.ShapeDtypeStruct((T, km), BF16), jax.ShapeDtypeStruct((T, kq), F32)],
        compiler_params=_cparams(("parallel",)),
        name="in_proj_a",
    )(x, g, wq, wk, wv, wgo, wqm, wgl, wup, bgk)


def _gla_kernel(q_ref, k_ref, gk_ref, v_ref, go_ref, g_ref, o_ref, st_ref, *, n_chunks):
    C = GLA_CHUNK

    @pl.when(pl.program_id(2) == 0)
    def _():
        st_ref[...] = jnp.zeros_like(st_ref)

    r = lax.broadcasted_iota(jnp.int32, (C, C), 0)
    c = lax.broadcasted_iota(jnp.int32, (C, C), 1)
    tril = c <= r
    tril_f = tril.astype(F32)
    g = g_ref[...]
    chunks = [pl.ds(ci * C, C) for ci in range(n_chunks)]
    bs = [jnp.dot(tril_f, gk_ref[sl, :], precision=lax.Precision.HIGHEST,
                  preferred_element_type=F32) for sl in chunks]
    q_in, k_in, q_state, k_state, decay = [], [], [], [], []
    for sl, b in zip(chunks, bs):
        b_mid = b[C // 2 - 1:C // 2, :]
        b_last = b[C - 1:C, :]
        qc = q_ref[sl, :] * (GLA_DK ** -0.5)
        kc = k_ref[sl, :]
        q_in.append((qc * jnp.exp(b - b_mid)).astype(BF16))
        k_in.append((kc * jnp.exp(b_mid - b)).astype(BF16))
        q_state.append((qc * jnp.exp(b)).astype(BF16))
        k_state.append((kc * jnp.exp(b_last - b)).astype(BF16))
        decay.append(jnp.exp(b_last))
    scores = [jnp.where(tril, lax.dot_general(qi, ki, _NT, preferred_element_type=F32), 0.0)
              for qi, ki in zip(q_in, k_in)]
    o_intra = [_dot(sc.astype(BF16), v_ref[sl, :]) for sl, sc in zip(chunks, scores)]
    kv_t = [lax.dot_general(v_ref[sl, :], ks, _TN, preferred_element_type=F32)
            for sl, ks in zip(chunks, k_state)]

    st = st_ref[...]
    for ci in range(n_chunks):
        sl = pl.ds(ci * C, C)
        o = o_intra[ci] + lax.dot_general(q_state[ci], st.astype(BF16), _NT,
                                          preferred_element_type=F32)
        st = st * decay[ci] + kv_t[ci]
        ms = jnp.sum(o * o, axis=-1, keepdims=True) * (1.0 / GLA_DV)
        on = o * lax.rsqrt(ms + EPS) * g
        go = go_ref[sl, :].astype(F32)
        o_ref[sl, :] = (on * _silu(go)).astype(BF16)
    st_ref[...] = st


def _gla(q, k, gk, v, go, g_gla, B, S):
    blk = _tiles(S, GLA_BLOCK)
    n_s = S // blk
    kmap = lambda b, h, s: (b * n_s + s, h)
    return pl.pallas_call(
        functools.partial(_gla_kernel, n_chunks=blk // GLA_CHUNK),
        grid=(B, GLA_HEADS, n_s),
        in_specs=[pl.BlockSpec((blk, GLA_DK_PAD), kmap), pl.BlockSpec((blk, GLA_DK_PAD), kmap),
                  pl.BlockSpec((blk, GLA_DK_PAD), kmap), pl.BlockSpec((blk, GLA_DV_PAD), kmap),
                  pl.BlockSpec((blk, GLA_DV_PAD), kmap),
                  pl.BlockSpec((1, GLA_DV_PAD), lambda b, h, s: (0, 0))],
        out_specs=pl.BlockSpec((blk, GLA_DV_PAD), kmap),
        out_shape=jax.ShapeDtypeStruct((B * S, GLA_HEADS * GLA_DV_PAD), BF16),
        scratch_shapes=[pltpu.VMEM((GLA_DV_PAD, GLA_DK_PAD), F32)],
        compiler_params=_cparams(("parallel", "parallel", "arbitrary")),
        name="gla",
    )(q, k, gk, v, go, g_gla)


def _matmul_kernel(x_ref, w_ref, o_ref):
    o_ref[...] = _dot(x_ref[...].astype(BF16), w_ref[...]).astype(o_ref.dtype)


def _matmul(x, w, out_dtype):
    M, K = x.shape
    N = w.shape[1]
    tm = _tiles(M)
    return pl.pallas_call(
        _matmul_kernel,
        grid=(M // tm,),
        in_specs=[pl.BlockSpec((tm, K), lambda i: (i, 0)), pl.BlockSpec((K, N), lambda i: (0, 0))],
        out_specs=pl.BlockSpec((tm, N), lambda i: (i, 0)),
        out_shape=jax.ShapeDtypeStruct((M, N), out_dtype),
        compiler_params=_cparams(("parallel",)),
        name="mem_kv_proj",
    )(x, w)


def _mem_attn_kernel(q_ref, kv_ref, o_ref):
    q = q_ref[...]
    kv = kv_ref[0]
    kmat = kv[:, :MEM_DIM]
    vmat = kv[:, MEM_DIM:]
    lane = lax.broadcasted_iota(jnp.int32, kmat.shape, 1)
    acc = jnp.zeros((q.shape[0], MEM_DIM), F32)
    for h in range(MEM_HEADS):
        in_head = (lane // HEAD_DIM) == h
        kh = jnp.where(in_head, kmat, jnp.zeros_like(kmat))
        vh = jnp.where(in_head, vmat, jnp.zeros_like(vmat))
        s = lax.dot_general(q, kh, _NT, preferred_element_type=F32) * (HEAD_DIM ** -0.5)
        p = jnp.exp(s - jnp.max(s, axis=-1, keepdims=True))
        p = p / jnp.sum(p, axis=-1, keepdims=True)
        acc = acc + _dot(p.astype(BF16), vh)
    o_ref[...] = acc.astype(BF16)


def _mem_attn(qm, kv_mem, B, S):
    tm = _tiles(S)
    n_s = S // tm
    return pl.pallas_call(
        _mem_attn_kernel,
        grid=(B, n_s),
        in_specs=[pl.BlockSpec((tm, MEM_DIM), lambda b, s: (b * n_s + s, 0)),
                  pl.BlockSpec((1, MEM_LEN, 2 * MEM_DIM), lambda b, s: (b, 0, 0))],
        out_specs=pl.BlockSpec((tm, MEM_DIM), lambda b, s: (b * n_s + s, 0)),
        out_shape=jax.ShapeDtypeStruct((B * S, MEM_DIM), BF16),
        compiler_params=_cparams(("parallel", "parallel")),
        name="mem_attn",
    )(qm, kv_mem)


def _out_proj_kernel(x_ref, o_ref, m_ref, w1_ref, w2_ref, y_ref):
    y_ref[...] = x_ref[...] + _dot(o_ref[...], w1_ref[...]) + _dot(m_ref[...], w2_ref[...])


def _out_proj(x, o, m, w1, w2):
    T = x.shape[0]
    tm = _tiles(T)
    row = lambda n: pl.BlockSpec((tm, n), lambda i: (i, 0))
    full = lambda a: pl.BlockSpec(a.shape, lambda i: (0, 0))
    return pl.pallas_call(
        _out_proj_kernel,
        grid=(T // tm,),
        in_specs=[row(D_MODEL), row(o.shape[1]), row(MEM_DIM), full(w1), full(w2)],
        out_specs=row(D_MODEL),
        out_shape=jax.ShapeDtypeStruct((T, D_MODEL), F32),
        compiler_params=_cparams(("parallel",)),
        name="out_proj",
    )(x, o, m, w1, w2)


def _ffn_kernel(x_ref, g_ref, wg_ref, wu_ref, wd_ref, y_ref, h_sc, acc_sc):
    f = pl.program_id(1)

    @pl.when(f == 0)
    def _():
        h_sc[...] = _rms(x_ref[...], g_ref[...]).astype(BF16)
        acc_sc[...] = jnp.zeros_like(acc_sc)

    h = h_sc[...]
    act = (_silu(_dot(h, wg_ref[...])) * _dot(h, wu_ref[...])).astype(BF16)
    acc_sc[...] += _dot(act, wd_ref[...])

    @pl.when(f == pl.num_programs(1) - 1)
    def _():
        y_ref[...] = x_ref[...] + acc_sc[...]


def _ffn(x, g, wg, wu, wd):
    T = x.shape[0]
    tm = _tiles(T)
    F = wg.shape[1]
    tf = F // 2
    return pl.pallas_call(
        _ffn_kernel,
        grid=(T // tm, F // tf),
        in_specs=[pl.BlockSpec((tm, D_MODEL), lambda i, f: (i, 0)),
                  pl.BlockSpec((1, D_MODEL), lambda i, f: (0, 0)),
                  pl.BlockSpec((D_MODEL, tf), lambda i, f: (0, f)),
                  pl.BlockSpec((D_MODEL, tf), lambda i, f: (0, f)),
                  pl.BlockSpec((tf, D_MODEL), lambda i, f: (f, 0))],
        out_specs=pl.BlockSpec((tm, D_MODEL), lambda i, f: (i, 0)),
        out_shape=jax.ShapeDtypeStruct((T, D_MODEL), F32),
        scratch_shapes=[pltpu.VMEM((tm, D_MODEL), BF16), pltpu.VMEM((tm, D_MODEL), F32)],
        compiler_params=_cparams(("parallel", "arbitrary")),
        name="dense_ffn",
    )(x, g, wg, wu, wd)


def _proj_rope_kernel(x_ref, g_ref, wr_ref, wp_ref, cos_ref, sin_ref, r_ref, p_ref, *, scale,
                      transpose_pass):
    h = _rms(x_ref[...], g_ref[...]).astype(BF16)
    p = _dot(h, wp_ref[...])
    if transpose_pass:
        pt = p.T.astype(BF16)
        rows = DIFF_VDIM + DIFF_ONES_ROWS
        for hh in range(p.shape[1] // DIFF_VDIM):
            p_ref[0, hh * rows:hh * rows + DIFF_VDIM, :] = pt[hh * DIFF_VDIM:(hh + 1) * DIFF_VDIM, :]
            p_ref[0, hh * rows + DIFF_VDIM:(hh + 1) * rows, :] = jnp.ones(
                (DIFF_ONES_ROWS, pt.shape[1]), BF16)
    else:
        p_ref[...] = p.astype(BF16)
    cos = cos_ref[...]
    sin = sin_ref[...]
    lane = lax.broadcasted_iota(jnp.int32, cos.shape, 1)
    first_half = (lane % HEAD_DIM) < (HEAD_DIM // 2)
    y_all = _dot(h, wr_ref[...])
    for j in range(wr_ref.shape[1] // LANES):
        sl = slice(j * LANES, (j + 1) * LANES)
        y = y_all[:, sl]
        partner = jnp.where(first_half, pltpu.roll(y, LANES - HEAD_DIM // 2, 1),
                            pltpu.roll(y, HEAD_DIM // 2, 1))
        roped = (y * cos + partner * sin) * scale
        if transpose_pass:
            r_ref[j] = roped.astype(BF16)
        else:
            r_ref[j] = roped.T.astype(BF16)


def _proj_rope(x, g, w_rope, w_pass, cos, sin, S, scale, transpose_pass):
    T = x.shape[0]
    tm = _tiles(S, DIFF_TK)
    n_s = S // tm
    row = lambda n: pl.BlockSpec((tm, n), lambda i: (i, 0))
    full = lambda a: pl.BlockSpec(a.shape, lambda i: (0, 0))
    tab = pl.BlockSpec((tm, LANES), lambda i: (i % n_s, 0))
    nr, np_ = w_rope.shape[1], w_pass.shape[1]
    if transpose_pass:
        nt = np_ // DIFF_VDIM * (DIFF_VDIM + DIFF_ONES_ROWS)
        pass_spec = pl.BlockSpec((1, nt, tm), lambda i: (i, 0, 0))
        pass_shape = jax.ShapeDtypeStruct((T // tm, nt, tm), BF16)
        rope_spec = pl.BlockSpec((nr // LANES, tm, LANES), lambda i: (0, i, 0))
        rope_shape = jax.ShapeDtypeStruct((nr // LANES, T, LANES), BF16)
    else:
        pass_spec = row(np_)
        pass_shape = jax.ShapeDtypeStruct((T, np_), BF16)
        rope_spec = pl.BlockSpec((nr // LANES, LANES, tm), lambda i: (0, 0, i))
        rope_shape = jax.ShapeDtypeStruct((nr // LANES, LANES, T), BF16)
    return pl.pallas_call(
        functools.partial(_proj_rope_kernel, scale=scale, transpose_pass=transpose_pass),
        grid=(T // tm,),
        in_specs=[row(D_MODEL), full(g), full(w_rope), full(w_pass), tab, tab],
        out_specs=[rope_spec, pass_spec],
        out_shape=[rope_shape, pass_shape],
        compiler_params=_cparams(("parallel",)),
        name="proj_rope",
    )(x, g, w_rope, w_pass, cos, sin)


def _diff_attn_kernel(q_ref, k_ref, vt_ref, lam_ref, g_ref, o_ref, qs_sc, m_sc, acc_sc, s0_sc, s1_sc,
                      *, tq, tk):
    i = pl.program_id(2)
    qt = q_ref[...]
    feat = lax.broadcasted_iota(jnp.int32, qt.shape, 0)
    qs_sc[:, 0:tq] = jnp.where(feat < HEAD_DIM, qt, jnp.zeros_like(qt))
    qs_sc[:, tq:2 * tq] = jnp.where(feat >= HEAD_DIM, qt, jnp.zeros_like(qt))
    m_sc[...] = jnp.full_like(m_sc, -1e30)
    acc_sc[...] = jnp.zeros_like(acc_sc)

    def scores(j, s_ref):
        kj = k_ref[pl.ds(pl.multiple_of(j * tk, tk), tk), :]
        s_ref[...] = _dot(kj, qs_sc[...])

    def update(j, s_ref, masked):
        s = s_ref[...]
        if masked:
            kpos = j * tk + lax.broadcasted_iota(jnp.int32, s.shape, 0)
            qpos = i * tq + lax.broadcasted_iota(jnp.int32, s.shape, 1) % tq
            s = jnp.where(kpos <= qpos, s, -1e30)
        m_prev = m_sc[...]
        m_new = jnp.maximum(m_prev, jnp.max(s, axis=0, keepdims=True))
        alpha = jnp.exp2(m_prev - m_new)
        p = jnp.exp2(s - m_new).astype(BF16)
        acc_sc[...] = alpha * acc_sc[...] + _dot(vt_ref[j], p)
        m_sc[...] = m_new

    n_full = (i * tq) // tk
    scores(0, s0_sc)

    def pair(a):
        scores(a + 1, s1_sc)
        update(a, s0_sc, False)
        scores(a + 2, s0_sc)
        update(a + 1, s1_sc, False)

    n_pairs = n_full // 2

    def body(qq, carry):
        pair(4 * qq)
        pair(4 * qq + 2)
        return carry

    lax.fori_loop(0, n_pairs // 2, body, 0)

    @pl.when(n_pairs % 2 == 1)
    def _():
        pair(2 * (n_pairs - 1))

    @pl.when(n_full % 2 == 1)
    def _():
        scores(n_full, s1_sc)
        update(n_full - 1, s0_sc, False)
        update(n_full, s1_sc, True)

    @pl.when(n_full % 2 == 0)
    def _():
        update(n_full, s0_sc, True)

    lam_rows = lam_ref[...]
    dot1 = jnp.sum(lam_rows[0:1, :] * lam_rows[1:2, :], axis=-1, keepdims=True)
    dot2 = jnp.sum(lam_rows[2:3, :] * lam_rows[3:4, :], axis=-1, keepdims=True)
    lam = jnp.exp(dot1) - jnp.exp(dot2) + LAMBDA_INIT_L1
    dv = DIFF_VDIM
    ot = (acc_sc[0:dv, 0:tq] / acc_sc[dv:dv + 1, 0:tq]
          - lam * (acc_sc[0:dv, tq:2 * tq] / acc_sc[dv:dv + 1, tq:2 * tq]))
    ot = ot * lax.rsqrt(jnp.mean(ot * ot, axis=0, keepdims=True) + EPS)
    o_ref[...] = ((ot.T * g_ref[...]) * (1.0 - LAMBDA_INIT_L1)).astype(BF16)


def _diff_attn(q, k, vt, lam_rows, g_subln, B, S):
    tk = vt.shape[2]
    tq = _tiles(S, DIFF_TQ)
    assert tq <= tk and tk % tq == 0 and S % tk == 0
    n_q = S // tq
    n_k = S // tk
    rows = DIFF_VDIM + DIFF_ONES_ROWS
    return pl.pallas_call(
        functools.partial(_diff_attn_kernel, tq=tq, tk=tk),
        grid=(B, DIFF_HEADS, n_q),
        in_specs=[pl.BlockSpec((None, DIFF_VDIM, tq), lambda b, h, i: (h, 0, b * n_q + i)),
                  pl.BlockSpec((None, S, DIFF_VDIM), lambda b, h, i: (h, b, 0)),
                  pl.BlockSpec((n_k, rows, tk), lambda b, h, i: (b, h, 0)),
                  pl.BlockSpec(lam_rows.shape, lambda b, h, i: (0, 0)),
                  pl.BlockSpec((1, DIFF_VDIM), lambda b, h, i: (0, 0))],
        out_specs=pl.BlockSpec((tq, DIFF_VDIM), lambda b, h, i: (b * n_q + i, h)),
        out_shape=jax.ShapeDtypeStruct((B * S, MIX_DIM), BF16),
        scratch_shapes=[pltpu.VMEM((DIFF_VDIM, 2 * tq), BF16), pltpu.VMEM((1, 2 * tq), F32),
                        pltpu.VMEM((rows, 2 * tq), F32), pltpu.VMEM((tk, 2 * tq), F32),
                        pltpu.VMEM((tk, 2 * tq), F32)],
        compiler_params=_cparams(("parallel", "parallel", "arbitrary")),
        name="diff_attn",
    )(q, k, vt, lam_rows, g_subln)


_R_E0, _R_E1, _R_RANK0, _R_RANK1, _R_G0, _R_G1 = range(6)


def _router_kernel(x_ref, g_ref, wr_ref, h_ref, info_ref, cnt_ref, run_sc):
    @pl.when(pl.program_id(0) == 0)
    def _():
        run_sc[...] = jnp.zeros_like(run_sc)

    h = _rms(x_ref[...], g_ref[...])
    h_ref[...] = h
    logits = jnp.dot(h, wr_ref[...], precision=lax.Precision.HIGHEST, preferred_element_type=F32)
    tm = logits.shape[0]
    lane = lax.broadcasted_iota(jnp.int32, logits.shape, 1).astype(F32)
    neg = -jnp.inf
    lg = jnp.where(lane < N_EXPERTS, logits, neg)
    m1 = jnp.max(lg, axis=-1, keepdims=True)
    i1 = jnp.min(jnp.where(lg == m1, lane, float(LANES)), axis=-1, keepdims=True)
    lg2 = jnp.where(lane == i1, neg, lg)
    m2 = jnp.max(lg2, axis=-1, keepdims=True)
    i2 = jnp.min(jnp.where(lg2 == m2, lane, float(LANES)), axis=-1, keepdims=True)
    e = jnp.exp(m2 - m1)
    g0 = 1.0 / (1.0 + e)
    g1 = e / (1.0 + e)
    onehot = jnp.where((lane == i1) | (lane == i2), 1.0, 0.0)
    r = lax.broadcasted_iota(jnp.int32, (tm, tm), 0)
    c = lax.broadcasted_iota(jnp.int32, (tm, tm), 1)
    before = _dot(jnp.where(c < r, 1.0, 0.0).astype(BF16), onehot.astype(BF16)) + run_sc[0:1, :]
    rank0 = jnp.sum(jnp.where(lane == i1, before, 0.0), axis=-1, keepdims=True)
    rank1 = jnp.sum(jnp.where(lane == i2, before, 0.0), axis=-1, keepdims=True)
    run_sc[...] = run_sc[...] + jnp.sum(onehot, axis=0, keepdims=True)
    info = jnp.zeros_like(logits)
    for idx, val in ((_R_E0, i1), (_R_E1, i2), (_R_RANK0, rank0), (_R_RANK1, rank1),
                     (_R_G0, g0), (_R_G1, g1)):
        info = jnp.where(lane == idx, val, info)
    info_ref[...] = info
    cnt_ref[...] = run_sc[...]


def _router(x, g, wr):
    T = x.shape[0]
    tm = _tiles(T)
    return pl.pallas_call(
        _router_kernel,
        grid=(T // tm,),
        in_specs=[pl.BlockSpec((tm, D_MODEL), lambda i: (i, 0)),
                  pl.BlockSpec((1, D_MODEL), lambda i: (0, 0)),
                  pl.BlockSpec((D_MODEL, LANES), lambda i: (0, 0))],
        out_specs=[pl.BlockSpec((tm, D_MODEL), lambda i: (i, 0)),
                   pl.BlockSpec((tm, LANES), lambda i: (i, 0)),
                   pl.BlockSpec((SUBLANES, LANES), lambda i: (0, 0))],
        out_shape=[jax.ShapeDtypeStruct((T, D_MODEL), F32), jax.ShapeDtypeStruct((T, LANES), F32),
                   jax.ShapeDtypeStruct((SUBLANES, LANES), F32)],
        scratch_shapes=[pltpu.VMEM((SUBLANES, LANES), F32)],
        compiler_params=_cparams(("arbitrary",)),
        name="moe_router",
    )(x, g, wr)


def _row_copy(src, src_row, dst, dst_row, sem):
    return pltpu.make_async_copy(src.at[pl.ds(src_row, 1)], dst.at[pl.ds(dst_row, 1)], sem)


def _dispatch_kernel(d0_ref, d1_ref, h_ref, xs_in_ref, xs_ref, sem):
    del xs_in_ref
    tm = h_ref.shape[0]

    def issue(r, carry):
        _row_copy(h_ref, r, xs_ref, d0_ref[r], sem).start(priority=0)
        _row_copy(h_ref, r, xs_ref, d1_ref[r], sem).start(priority=1)
        return carry

    lax.fori_loop(0, tm, issue, 0, unroll=ROW_DMA_UNROLL)

    def drain(r, carry):
        _row_copy(h_ref, 0, xs_ref, 0, sem).wait()
        _row_copy(h_ref, 0, xs_ref, 0, sem).wait()
        return carry

    lax.fori_loop(0, tm, drain, 0, unroll=ROW_DMA_UNROLL)


def _dispatch(dest0, dest1, h, xs_zero):
    T = h.shape[0]
    tm = _tiles(T)
    smem = pl.BlockSpec((tm,), lambda i: (i,), memory_space=pltpu.SMEM)
    return pl.pallas_call(
        _dispatch_kernel,
        grid=(T // tm,),
        in_specs=[smem, smem, pl.BlockSpec((tm, D_MODEL), lambda i: (i, 0)),
                  pl.BlockSpec(memory_space=pl.ANY)],
        out_specs=pl.BlockSpec(memory_space=pl.ANY),
        out_shape=jax.ShapeDtypeStruct(xs_zero.shape, xs_zero.dtype),
        scratch_shapes=[pltpu.SemaphoreType.DMA],
        input_output_aliases={3: 0},
        compiler_params=_cparams(("arbitrary",)),
        name="moe_dispatch",
    )(dest0, dest1, h, xs_zero)


def _expert_kernel(bexp_ref, nused_ref, xs_ref, wg_ref, wu_ref, wd_ref, y_ref, xb_sc, acc_sc):
    del bexp_ref
    j = pl.program_id(0)
    f = pl.program_id(1)

    @pl.when(j < nused_ref[0])
    def _():
        @pl.when(f == 0)
        def _():
            xb_sc[...] = xs_ref[...].astype(BF16)
            acc_sc[...] = jnp.zeros_like(acc_sc)

        xb = xb_sc[...]
        act = (_silu(_dot(xb, wg_ref[0].astype(BF16)))
               * _dot(xb, wu_ref[0].astype(BF16))).astype(BF16)
        acc_sc[...] += _dot(act, wd_ref[0].astype(BF16))

        @pl.when(f == pl.num_programs(1) - 1)
        def _():
            y_ref[...] = acc_sc[...]

    @pl.when((j >= nused_ref[0]) & (f == 0))
    def _():
        y_ref[...] = jnp.zeros_like(y_ref)


def _experts(block_expert, n_used, xs, wg, wu, wd, rb):
    L = xs.shape[0]
    nb = L // rb
    tf = EXPERT_FFN_TILE
    nf = FFN_EXPERT // tf

    def blk(j, f, be, nu):
        return jnp.minimum(j, nu[0] - 1)

    def ftile(j, f, be, nu):
        return jnp.where(j < nu[0], f, nf - 1)

    return pl.pallas_call(
        _expert_kernel,
        grid_spec=pltpu.PrefetchScalarGridSpec(
            num_scalar_prefetch=2,
            grid=(nb, nf),
            in_specs=[
                pl.BlockSpec((rb, D_MODEL), lambda j, f, be, nu: (blk(j, f, be, nu), 0)),
                pl.BlockSpec((1, D_MODEL, tf),
                             lambda j, f, be, nu: (be[blk(j, f, be, nu)], 0, ftile(j, f, be, nu))),
                pl.BlockSpec((1, D_MODEL, tf),
                             lambda j, f, be, nu: (be[blk(j, f, be, nu)], 0, ftile(j, f, be, nu))),
                pl.BlockSpec((1, tf, D_MODEL),
                             lambda j, f, be, nu: (be[blk(j, f, be, nu)], ftile(j, f, be, nu), 0)),
            ],
            out_specs=pl.BlockSpec((rb, D_MODEL), lambda j, f, be, nu: (j, 0)),
            scratch_shapes=[pltpu.VMEM((rb, D_MODEL), BF16), pltpu.VMEM((rb, D_MODEL), F32)],
        ),
        out_shape=jax.ShapeDtypeStruct((L, D_MODEL), F32),
        compiler_params=_cparams(("arbitrary", "arbitrary")),
        name="moe_experts",
    )(block_expert, n_used, xs, wg, wu, wd)


def _combine_kernel(d0_ref, d1_ref, x_ref, info_ref, g_ref, y_ref, o_ref, b0_sc, b1_sc, sem):
    tm = x_ref.shape[0]

    def issue(r, carry):
        _row_copy(y_ref, d0_ref[r], b0_sc, r, sem).start(priority=0)
        _row_copy(y_ref, d1_ref[r], b1_sc, r, sem).start(priority=1)
        return carry

    lax.fori_loop(0, tm, issue, 0, unroll=ROW_DMA_UNROLL)

    def drain(r, carry):
        _row_copy(y_ref, 0, b0_sc, 0, sem).wait()
        _row_copy(y_ref, 0, b1_sc, 0, sem).wait()
        return carry

    lax.fori_loop(0, tm, drain, 0, unroll=ROW_DMA_UNROLL)
    info = info_ref[...]
    g0 = info[:, _R_G0:_R_G0 + 1]
    g1 = info[:, _R_G1:_R_G1 + 1]
    x = x_ref[...] + (b0_sc[...] * g0 + b1_sc[...] * g1)
    o_ref[...] = _rms(x, g_ref[...])


def _combine(dest0, dest1, x, info, g_final, y):
    T = x.shape[0]
    tm = _tiles(T)
    smem = pl.BlockSpec((tm,), lambda i: (i,), memory_space=pltpu.SMEM)
    return pl.pallas_call(
        _combine_kernel,
        grid=(T // tm,),
        in_specs=[smem, smem, pl.BlockSpec((tm, D_MODEL), lambda i: (i, 0)),
                  pl.BlockSpec((tm, LANES), lambda i: (i, 0)),
                  pl.BlockSpec((1, D_MODEL), lambda i: (0, 0)),
                  pl.BlockSpec(memory_space=pl.ANY)],
        out_specs=pl.BlockSpec((tm, D_MODEL), lambda i: (i, 0)),
        out_shape=jax.ShapeDtypeStruct((T, D_MODEL), F32),
        scratch_shapes=[pltpu.VMEM((tm, D_MODEL), F32), pltpu.VMEM((tm, D_MODEL), F32),
                        pltpu.SemaphoreType.DMA],
        compiler_params=_cparams(("arbitrary",)),
        name="moe_combine",
    )(dest0, dest1, x, info, g_final, y)


def _pad_heads_cols(w, heads, d, d_pad):
    k = w.shape[0]
    return jnp.pad(w.reshape(k, heads, d), ((0, 0), (0, 0), (0, d_pad - d))).reshape(k, heads * d_pad)


def _rope_tables(S):
    half = HEAD_DIM // 2
    inv = ROPE_THETA ** (-jnp.arange(half, dtype=F32) / half)
    ang = jnp.arange(S).astype(F32)[:, None] * inv[None, :]
    reps = LANES // half
    cos = jnp.tile(jnp.cos(ang), (1, reps))
    sign = jnp.tile(jnp.concatenate([-jnp.ones((half,), F32), jnp.ones((half,), F32)]),
                    LANES // HEAD_DIM)
    sin = jnp.tile(jnp.sin(ang), (1, reps)) * sign[None, :]
    return cos, sin


def _moe_block_rows(T):
    return EXPERT_ROWS if (2 * T) % EXPERT_ROWS == 0 else EXPERT_ROWS_SMALL


def kernel(x, mem, g_mix, g_ffn, w_mem_kv, w_out, w_in_a, w_gk_up, b_gk, g_gla, g_kv, w_kv, w_in_b, lambda_q1, lambda_k1, lambda_q2, lambda_k2, g_subln, w_dense_gate, w_dense_up, w_dense_down, w_router, w_exp_gate, w_exp_up, w_exp_down, g_final):
    B, S, D = x.shape
    T = B * S
    x0 = x.reshape(T, D)
    mem2 = mem.reshape(B * MEM_LEN, D)
    row = lambda v: v.reshape(1, -1)

    wa = w_in_a[0]
    s0, s1, s2, s3, s4 = (GLA_KDIM, 2 * GLA_KDIM, 2 * GLA_KDIM + MIX_DIM, 2 * GLA_KDIM + 2 * MIX_DIM,
                          2 * GLA_KDIM + 2 * MIX_DIM + GLA_GATE_RANK)
    wq = _pad_heads_cols(wa[:, :s0], GLA_HEADS, GLA_DK, GLA_DK_PAD).astype(BF16)
    wk = _pad_heads_cols(wa[:, s0:s1], GLA_HEADS, GLA_DK, GLA_DK_PAD).astype(BF16)
    wv = _pad_heads_cols(wa[:, s1:s2], GLA_HEADS, GLA_DV, GLA_DV_PAD).astype(BF16)
    wgo = _pad_heads_cols(wa[:, s2:s3], GLA_HEADS, GLA_DV, GLA_DV_PAD).astype(BF16)
    wgl = jnp.pad(wa[:, s3:s4], ((0, 0), (0, LANES - GLA_GATE_RANK))).astype(BF16)
    wqm = wa[:, s4:].astype(BF16)
    wup = jnp.pad(_pad_heads_cols(w_gk_up[0], GLA_HEADS, GLA_DK, GLA_DK_PAD),
                  ((0, LANES - GLA_GATE_RANK), (0, 0))).astype(BF16)
    bgk = _pad_heads_cols(row(b_gk[0]), GLA_HEADS, GLA_DK, GLA_DK_PAD)
    ggla = jnp.pad(row(g_gla[0]), ((0, 0), (0, GLA_DV_PAD - GLA_DV)))
    wo_a = jnp.pad(w_out[0][:MIX_DIM].reshape(GLA_HEADS, GLA_DV, D),
                   ((0, 0), (0, GLA_DV_PAD - GLA_DV), (0, 0))).reshape(GLA_HEADS * GLA_DV_PAD, D)

    q, k, v, go, qm, gk = _in_proj_a(x0, row(g_mix[0]), wq, wk, wv, wgo, wqm, wgl, wup, bgk)
    o = _gla(q, k, gk, v, go, ggla, B, S)
    kv_mem0 = _matmul(mem2, w_mem_kv[0].astype(BF16), BF16).reshape(B, MEM_LEN, 2 * MEM_DIM)
    m = _mem_attn(qm, kv_mem0, B, S)
    x1 = _out_proj(x0, o, m, wo_a.astype(BF16), w_out[0][MIX_DIM:].astype(BF16))
    x2 = _ffn(x1, row(g_ffn[0]), w_dense_gate[0].astype(BF16), w_dense_up[0].astype(BF16),
              w_dense_down[0].astype(BF16))

    cos, sin = _rope_tables(S)
    k_sh, vt_sh = _proj_rope(x2, row(g_kv), w_kv[:, :MIX_DIM].astype(BF16),
                             w_kv[:, MIX_DIM:].astype(BF16), cos, sin, S, 1.0, True)

    q1, qm1 = _proj_rope(x2, row(g_mix[1]), w_in_b[0][:, :MIX_DIM].astype(BF16),
                         w_in_b[0][:, MIX_DIM:].astype(BF16), cos, sin, S,
                         HEAD_DIM ** -0.5 * LOG2_E, False)
    lam_rows = jnp.pad(jnp.stack([lambda_q1[0], lambda_k1[0], lambda_q2[0], lambda_k2[0]]),
                       ((0, SUBLANES - 4), (0, LANES - HEAD_DIM)))
    o1 = _diff_attn(q1, k_sh, vt_sh, lam_rows, row(g_subln[0]), B, S)
    kv_mem1 = _matmul(mem2, w_mem_kv[1].astype(BF16), BF16).reshape(B, MEM_LEN, 2 * MEM_DIM)
    m1 = _mem_attn(qm1, kv_mem1, B, S)
    x3 = _out_proj(x2, o1, m1, w_out[1][:MIX_DIM].astype(BF16), w_out[1][MIX_DIM:].astype(BF16))

    wr = jnp.pad(w_router[0], ((0, 0), (0, LANES - N_EXPERTS)))
    h, info, cnt = _router(x3, row(g_ffn[1]), wr)

    rb = _moe_block_rows(T)
    counts = cnt[0, :N_EXPERTS].astype(jnp.int32)
    padded = (counts + rb - 1) // rb * rb
    pend = jnp.cumsum(padded)
    pstart = pend - padded
    e0 = info[:, _R_E0].astype(jnp.int32)
    e1 = info[:, _R_E1].astype(jnp.int32)
    eids = jnp.arange(N_EXPERTS, dtype=jnp.int32)
    start_of = lambda e: jnp.sum(jnp.where(e[:, None] == eids[None, :], pstart[None, :], 0), axis=1)
    dest0 = start_of(e0) + info[:, _R_RANK0].astype(jnp.int32)
    dest1 = start_of(e1) + info[:, _R_RANK1].astype(jnp.int32)
    nb = (2 * T) // rb + N_EXPERTS
    block_expert = jnp.clip(jnp.sum(pend[None, :] <= (jnp.arange(nb, dtype=jnp.int32) * rb)[:, None],
                                    axis=1), 0, N_EXPERTS - 1).astype(jnp.int32)
    n_used = (pend[-1:] // rb).astype(jnp.int32)

    xs = _dispatch(dest0, dest1, h, jnp.zeros((nb * rb, D), F32))
    ys = _experts(block_expert, n_used, xs, w_exp_gate[0], w_exp_up[0], w_exp_down[0], rb)
    out = _combine(dest0, dest1, x3, info, row(g_final), ys)
    return out.reshape(B, S, D)
```

```python
import functools
import math

import jax
import jax.numpy as jnp
from jax import lax
from jax.experimental import pallas as pl
from jax.experimental.pallas import tpu as pltpu

D_MODEL = 1024
MEM_LEN = 256
HEAD_DIM = 64
MEM_HEADS = 4
MEM_DIM = MEM_HEADS * HEAD_DIM
MIX_DIM = D_MODEL - MEM_DIM
GLA_HEADS = 4
GLA_DV = MIX_DIM // GLA_HEADS
GLA_DK = GLA_DV // 2
GLA_KDIM = GLA_HEADS * GLA_DK
GLA_GATE_RANK = 16
GLA_GATE_NORM = 16.0
GLA_CHUNK = 64
DIFF_HEADS = MIX_DIM // (2 * HEAD_DIM)
DIFF_VDIM = 2 * HEAD_DIM
ROPE_THETA = 10000.0
FFN_DENSE = 2816
N_EXPERTS = 8
FFN_EXPERT = 3584
EPS = 1e-6
LAMBDA_INIT_L1 = 0.8 - 0.6 * math.exp(-0.3 * 1)

LANES = 128
SUBLANES = 8
VMEM_LIMIT_BYTES = 52 * 1024 * 1024

GLA_DK_PAD = LANES
GLA_DV_PAD = 2 * LANES
ROW_TILE = 512
GLA_BLOCK = 512
EXPERT_ROWS = 1024
EXPERT_ROWS_SMALL = 256
EXPERT_FFN_TILE = 512
DIFF_TQ = 512
ROW_DMA_UNROLL = 8
DIFF_TK = 512
DIFF_ONES_ROWS = 16
LOG2_E = 1.4426950408889634

F32 = jnp.float32
BF16 = jnp.bfloat16
_NT = (((1,), (1,)), ((), ()))
_TN = (((0,), (0,)), ((), ()))


def _tiles(T, tile=ROW_TILE):
    return tile if T % tile == 0 else T


def _cparams(sem):
    return pltpu.CompilerParams(dimension_semantics=sem, vmem_limit_bytes=VMEM_LIMIT_BYTES)


def _rms(xf, g):
    y = xf * lax.rsqrt(jnp.mean(xf * xf, axis=-1, keepdims=True) + EPS)
    return y * g


def _silu(a):
    return a / (1.0 + jnp.exp(-a))


def _dot(a, b):
    return jnp.dot(a, b, preferred_element_type=F32)


def _in_proj_a_kernel(x_ref, g_ref, wq_ref, wk_ref, wv_ref, wgo_ref, wqm_ref, wgl_ref, wup_ref,
                      bgk_ref, q_ref, k_ref, v_ref, go_ref, qm_ref, gk_ref):
    h = _rms(x_ref[...], g_ref[...]).astype(BF16)
    q_ref[...] = _dot(h, wq_ref[...])
    k_ref[...] = _dot(h, wk_ref[...])
    v_ref[...] = _dot(h, wv_ref[...]).astype(BF16)
    go_ref[...] = _dot(h, wgo_ref[...]).astype(BF16)
    qm_ref[...] = _dot(h, wqm_ref[...]).astype(BF16)
    low = _dot(h, wgl_ref[...]).astype(BF16)
    z = _dot(low, wup_ref[...]) + bgk_ref[...]
    log_sig = jnp.minimum(z, 0.0) - jnp.log1p(jnp.exp(-jnp.abs(z)))
    gk_ref[...] = log_sig / GLA_GATE_NORM


def _in_proj_a(x, g, wq, wk, wv, wgo, wqm, wgl, wup, bgk):
    T = x.shape[0]
    tm = _tiles(T)
    row = lambda n: pl.BlockSpec((tm, n), lambda i: (i, 0))
    full = lambda a: pl.BlockSpec(a.shape, lambda i: (0, 0))
    kq, kv_, km = GLA_HEADS * GLA_DK_PAD, GLA_HEADS * GLA_DV_PAD, MEM_DIM
    return pl.pallas_call(
        _in_proj_a_kernel,
        grid=(T // tm,),
        in_specs=[row(D_MODEL)] + [full(a) for a in (g, wq, wk, wv, wgo, wqm, wgl, wup, bgk)],
        out_specs=[row(kq), row(kq), row(kv_), row(kv_), row(km), row(kq)],
        out_shape=[jax.ShapeDtypeStruct((T, kq), F32), jax.ShapeDtypeStruct((T, kq), F32),
                   jax.ShapeDtypeStruct((T, kv_), BF16), jax.ShapeDtypeStruct((T, kv_), BF16),
                   jax.ShapeDtypeStruct((T, km), BF16), jax.ShapeDtypeStruct((T, kq), F32)],
        compiler_params=_cparams(("parallel",)),
        name="in_proj_a",
    )(x, g, wq, wk, wv, wgo, wqm, wgl, wup, bgk)


def _gla_kernel(q_ref, k_ref, gk_ref, v_ref, go_ref, g_ref, o_ref, st_ref, *, n_chunks):
    C = GLA_CHUNK

    @pl.when(pl.program_id(2) == 0)
    def _():
        st_ref[...] = jnp.zeros_like(st_ref)

    r = lax.broadcasted_iota(jnp.int32, (C, C), 0)
    c = lax.broadcasted_iota(jnp.int32, (C, C), 1)
    tril = c <= r
    tril_f = tril.astype(F32)
    g = g_ref[...]
    chunks = [pl.ds(ci * C, C) for ci in range(n_chunks)]
    bs = [jnp.dot(tril_f, gk_ref[sl, :], precision=lax.Precision.HIGHEST,
                  preferred_element_type=F32) for sl in chunks]
    q_in, k_in, q_state, k_state, decay = [], [], [], [], []
    for sl, b in zip(chunks, bs):
        b_mid = b[C // 2 - 1:C // 2, :]
        b_last = b[C - 1:C, :]
        qc = q_ref[sl, :] * (GLA_DK ** -0.5)
        kc = k_ref[sl, :]
        q_in.append((qc * jnp.exp(b - b_mid)).astype(BF16))
        k_in.append((kc * jnp.exp(b_mid - b)).astype(BF16))
        q_state.append((qc * jnp.exp(b)).astype(BF16))
        k_state.append((kc * jnp.exp(b_last - b)).astype(BF16))
        decay.append(jnp.exp(b_last))
    scores = [jnp.where(tril, lax.dot_general(qi, ki, _NT, preferred_element_type=F32), 0.0)
              for qi, ki in zip(q_in, k_in)]
    o_intra = [_dot(sc.astype(BF16), v_ref[sl, :]) for sl, sc in zip(chunks, scores)]
    kv_t = [lax.dot_general(v_ref[sl, :], ks, _TN, preferred_element_type=F32)
            for sl, ks in zip(chunks, k_state)]

    st = st_ref[...]
    for ci in range(n_chunks):
        sl = pl.ds(ci * C, C)
        o = o_intra[ci] + lax.dot_general(q_state[ci], st.astype(BF16), _NT,
                                          preferred_element_type=F32)
        st = st * decay[ci] + kv_t[ci]
        ms = jnp.sum(o * o, axis=-1, keepdims=True) * (1.0 / GLA_DV)
        on = o * lax.rsqrt(ms + EPS) * g
        go = go_ref[sl, :].astype(F32)
        o_ref[sl, :] = (on * _silu(go)).astype(BF16)
    st_ref[...] = st


def _gla(q, k, gk, v, go, g_gla, B, S):
    blk = _tiles(S, GLA_BLOCK)
    n_s = S // blk
    kmap = lambda b, h, s: (b * n_s + s, h)
    return pl.pallas_call(
        functools.partial(_gla_kernel, n_chunks=blk // GLA_CHUNK),
        grid=(B, GLA_HEADS, n_s),
        in_specs=[pl.BlockSpec((blk, GLA_DK_PAD), kmap), pl.BlockSpec((blk, GLA_DK_PAD), kmap),
                  pl.BlockSpec((blk, GLA_DK_PAD), kmap), pl.BlockSpec((blk, GLA_DV_PAD), kmap),
                  pl.BlockSpec((blk, GLA_DV_PAD), kmap),
                  pl.BlockSpec((1, GLA_DV_PAD), lambda b, h, s: (0, 0))],
        out_specs=pl.BlockSpec((blk, GLA_DV_PAD), kmap),
        out_shape=jax.ShapeDtypeStruct((B * S, GLA_HEADS * GLA_DV_PAD), BF16),
        scratch_shapes=[pltpu.VMEM((GLA_DV_PAD, GLA_DK_PAD), F32)],
        compiler_params=_cparams(("parallel", "parallel", "arbitrary")),
        name="gla",
    )(q, k, gk, v, go, g_gla)


def _matmul_kernel(x_ref, w_ref, o_ref):
    o_ref[...] = _dot(x_ref[...].astype(BF16), w_ref[...]).astype(o_ref.dtype)


def _matmul(x, w, out_dtype):
    M, K = x.shape
    N = w.shape[1]
    tm = _tiles(M)
    return pl.pallas_call(
        _matmul_kernel,
        grid=(M // tm,),
        in_specs=[pl.BlockSpec((tm, K), lambda i: (i, 0)), pl.BlockSpec((K, N), lambda i: (0, 0))],
        out_specs=pl.BlockSpec((tm, N), lambda i: (i, 0)),
        out_shape=jax.ShapeDtypeStruct((M, N), out_dtype),
        compiler_params=_cparams(("parallel",)),
        name="mem_kv_proj",
    )(x, w)


def _mem_attn_kernel(q_ref, kv_ref, o_ref):
    q = q_ref[...]
    kv = kv_ref[0]
    kmat = kv[:, :MEM_DIM]
    vmat = kv[:, MEM_DIM:]
    lane = lax.broadcasted_iota(jnp.int32, kmat.shape, 1)
    acc = jnp.zeros((q.shape[0], MEM_DIM), F32)
    for h in range(MEM_HEADS):
        in_head = (lane // HEAD_DIM) == h
        kh = jnp.where(in_head, kmat, jnp.zeros_like(kmat))
        vh = jnp.where(in_head, vmat, jnp.zeros_like(vmat))
        s = lax.dot_general(q, kh, _NT, preferred_element_type=F32) * (HEAD_DIM ** -0.5)
        p = jnp.exp(s - jnp.max(s, axis=-1, keepdims=True))
        p = p / jnp.sum(p, axis=-1, keepdims=True)
        acc = acc + _dot(p.astype(BF16), vh)
    o_ref[...] = acc.astype(BF16)


def _mem_attn(qm, kv_mem, B, S):
    tm = _tiles(S)
    n_s = S // tm
    return pl.pallas_call(
        _mem_attn_kernel,
        grid=(B, n_s),
        in_specs=[pl.BlockSpec((tm, MEM_DIM), lambda b, s: (b * n_s + s, 0)),
                  pl.BlockSpec((1, MEM_LEN, 2 * MEM_DIM), lambda b, s: (b, 0, 0))],
        out_specs=pl.BlockSpec((tm, MEM_DIM), lambda b, s: (b * n_s + s, 0)),
        out_shape=jax.ShapeDtypeStruct((B * S, MEM_DIM), BF16),
        compiler_params=_cparams(("parallel", "parallel")),
        name="mem_attn",
    )(qm, kv_mem)


def _out_proj_kernel(x_ref, o_ref, m_ref, w1_ref, w2_ref, y_ref):
    y_ref[...] = x_ref[...] + _dot(o_ref[...], w1_ref[...]) + _dot(m_ref[...], w2_ref[...])


def _out_proj(x, o, m, w1, w2):
    T = x.shape[0]
    tm = _tiles(T)
    row = lambda n: pl.BlockSpec((tm, n), lambda i: (i, 0))
    full = lambda a: pl.BlockSpec(a.shape, lambda i: (0, 0))
    return pl.pallas_call(
        _out_proj_kernel,
        grid=(T // tm,),
        in_specs=[row(D_MODEL), row(o.shape[1]), row(MEM_DIM), full(w1), full(w2)],
        out_specs=row(D_MODEL),
        out_shape=jax.ShapeDtypeStruct((T, D_MODEL), F32),
        compiler_params=_cparams(("parallel",)),
        name="out_proj",
    )(x, o, m, w1, w2)


def _ffn_kernel(x_ref, g_ref, wg_ref, wu_ref, wd_ref, y_ref, acc_sc):
    f = pl.program_id(1)

    @pl.when(f == 0)
    def _():
        acc_sc[...] = jnp.zeros_like(acc_sc)

    h = _rms(x_ref[...], g_ref[...]).astype(BF16)
    act = (_silu(_dot(h, wg_ref[...])) * _dot(h, wu_ref[...])).astype(BF16)
    acc_sc[...] += _dot(act, wd_ref[...])

    @pl.when(f == pl.num_programs(1) - 1)
    def _():
        y_ref[...] = x_ref[...] + acc_sc[...]


def _ffn(x, g, wg, wu, wd):
    T = x.shape[0]
    tm = _tiles(T)
    F = wg.shape[1]
    tf = F // 2
    return pl.pallas_call(
        _ffn_kernel,
        grid=(T // tm, F // tf),
        in_specs=[pl.BlockSpec((tm, D_MODEL), lambda i, f: (i, 0)),
                  pl.BlockSpec((1, D_MODEL), lambda i, f: (0, 0)),
                  pl.BlockSpec((D_MODEL, tf), lambda i, f: (0, f)),
                  pl.BlockSpec((D_MODEL, tf), lambda i, f: (0, f)),
                  pl.BlockSpec((tf, D_MODEL), lambda i, f: (f, 0))],
        out_specs=pl.BlockSpec((tm, D_MODEL), lambda i, f: (i, 0)),
        out_shape=jax.ShapeDtypeStruct((T, D_MODEL), F32),
        scratch_shapes=[pltpu.VMEM((tm, D_MODEL), F32)],
        compiler_params=_cparams(("parallel", "arbitrary")),
        name="dense_ffn",
    )(x, g, wg, wu, wd)


def _proj_rope_kernel(x_ref, g_ref, wr_ref, wp_ref, cos_ref, sin_ref, r_ref, p_ref, *, scale,
                      transpose_pass):
    h = _rms(x_ref[...], g_ref[...]).astype(BF16)
    p = _dot(h, wp_ref[...])
    if transpose_pass:
        pt = p.T.astype(BF16)
        rows = DIFF_VDIM + DIFF_ONES_ROWS
        for hh in range(p.shape[1] // DIFF_VDIM):
            p_ref[0, hh * rows:hh * rows + DIFF_VDIM, :] = pt[hh * DIFF_VDIM:(hh + 1) * DIFF_VDIM, :]
            p_ref[0, hh * rows + DIFF_VDIM:(hh + 1) * rows, :] = jnp.ones(
                (DIFF_ONES_ROWS, pt.shape[1]), BF16)
    else:
        p_ref[...] = p.astype(BF16)
    cos = cos_ref[...]
    sin = sin_ref[...]
    lane = lax.broadcasted_iota(jnp.int32, cos.shape, 1)
    first_half = (lane % HEAD_DIM) < (HEAD_DIM // 2)
    y_all = _dot(h, wr_ref[...])
    for j in range(wr_ref.shape[1] // LANES):
        sl = slice(j * LANES, (j + 1) * LANES)
        y = y_all[:, sl]
        partner = jnp.where(first_half, pltpu.roll(y, LANES - HEAD_DIM // 2, 1),
                            pltpu.roll(y, HEAD_DIM // 2, 1))
        roped = (y * cos + partner * sin) * scale
        if transpose_pass:
            r_ref[j] = roped.astype(BF16)
        else:
            r_ref[j] = roped.T.astype(BF16)


def _proj_rope(x, g, w_rope, w_pass, cos, sin, S, scale, transpose_pass):
    T = x.shape[0]
    tm = _tiles(S, DIFF_TK)
    n_s = S // tm
    row = lambda n: pl.BlockSpec((tm, n), lambda i: (i, 0))
    full = lambda a: pl.BlockSpec(a.shape, lambda i: (0, 0))
    tab = pl.BlockSpec((tm, LANES), lambda i: (i % n_s, 0))
    nr, np_ = w_rope.shape[1], w_pass.shape[1]
    if transpose_pass:
        nt = np_ // DIFF_VDIM * (DIFF_VDIM + DIFF_ONES_ROWS)
        pass_spec = pl.BlockSpec((1, nt, tm), lambda i: (i, 0, 0))
        pass_shape = jax.ShapeDtypeStruct((T // tm, nt, tm), BF16)
        rope_spec = pl.BlockSpec((nr // LANES, tm, LANES), lambda i: (0, i, 0))
        rope_shape = jax.ShapeDtypeStruct((nr // LANES, T, LANES), BF16)
    else:
        pass_spec = row(np_)
        pass_shape = jax.ShapeDtypeStruct((T, np_), BF16)
        rope_spec = pl.BlockSpec((nr // LANES, LANES, tm), lambda i: (0, 0, i))
        rope_shape = jax.ShapeDtypeStruct((nr // LANES, LANES, T), BF16)
    return pl.pallas_call(
        functools.partial(_proj_rope_kernel, scale=scale, transpose_pass=transpose_pass),
        grid=(T // tm,),
        in_specs=[row(D_MODEL), full(g), full(w_rope), full(w_pass), tab, tab],
        out_specs=[rope_spec, pass_spec],
        out_shape=[rope_shape, pass_shape],
        compiler_params=_cparams(("parallel",)),
        name="proj_rope",
    )(x, g, w_rope, w_pass, cos, sin)


def _diff_attn_kernel(q_ref, k_ref, vt_ref, lam_ref, g_ref, o_ref, qs_sc, m_sc, acc_sc, s0_sc, s1_sc,
                      *, tq, tk):
    i = pl.program_id(2)
    qt = q_ref[...]
    feat = lax.broadcasted_iota(jnp.int32, qt.shape, 0)
    qs_sc[:, 0:tq] = jnp.where(feat < HEAD_DIM, qt, jnp.zeros_like(qt))
    qs_sc[:, tq:2 * tq] = jnp.where(feat >= HEAD_DIM, qt, jnp.zeros_like(qt))
    m_sc[...] = jnp.full_like(m_sc, -1e30)
    acc_sc[...] = jnp.zeros_like(acc_sc)

    def scores(j, s_ref):
        kj = k_ref[pl.ds(pl.multiple_of(j * tk, tk), tk), :]
        s_ref[...] = _dot(kj, qs_sc[...])

    def update(j, s_ref, masked):
        s = s_ref[...]
        if masked:
            kpos = j * tk + lax.broadcasted_iota(jnp.int32, s.shape, 0)
            qpos = i * tq + lax.broadcasted_iota(jnp.int32, s.shape, 1) % tq
            s = jnp.where(kpos <= qpos, s, -1e30)
        m_prev = m_sc[...]
        m_new = jnp.maximum(m_prev, jnp.max(s, axis=0, keepdims=True))
        alpha = jnp.exp2(m_prev - m_new)
        p = jnp.exp2(s - m_new).astype(BF16)
        acc_sc[...] = alpha * acc_sc[...] + _dot(vt_ref[j], p)
        m_sc[...] = m_new

    n_full = (i * tq) // tk
    scores(0, s0_sc)

    def pair(a):
        scores(a + 1, s1_sc)
        update(a, s0_sc, False)
        scores(a + 2, s0_sc)
        update(a + 1, s1_sc, False)

    n_pairs = n_full // 2

    def body(qq, carry):
        pair(4 * qq)
        pair(4 * qq + 2)
        return carry

    lax.fori_loop(0, n_pairs // 2, body, 0)

    @pl.when(n_pairs % 2 == 1)
    def _():
        pair(2 * (n_pairs - 1))

    @pl.when(n_full % 2 == 1)
    def _():
        scores(n_full, s1_sc)
        update(n_full - 1, s0_sc, False)
        update(n_full, s1_sc, True)

    @pl.when(n_full % 2 == 0)
    def _():
        update(n_full, s0_sc, True)

    lam_rows = lam_ref[...]
    dot1 = jnp.sum(lam_rows[0:1, :] * lam_rows[1:2, :], axis=-1, keepdims=True)
    dot2 = jnp.sum(lam_rows[2:3, :] * lam_rows[3:4, :], axis=-1, keepdims=True)
    lam = jnp.exp(dot1) - jnp.exp(dot2) + LAMBDA_INIT_L1
    dv = DIFF_VDIM
    ot = (acc_sc[0:dv, 0:tq] / acc_sc[dv:dv + 1, 0:tq]
          - lam * (acc_sc[0:dv, tq:2 * tq] / acc_sc[dv:dv + 1, tq:2 * tq]))
    ot = ot * lax.rsqrt(jnp.mean(ot * ot, axis=0, keepdims=True) + EPS)
    o_ref[...] = ((ot.T * g_ref[...]) * (1.0 - LAMBDA_INIT_L1)).astype(BF16)


def _diff_attn(q, k, vt, lam_rows, g_subln, B, S):
    tk = vt.shape[2]
    tq = _tiles(S, DIFF_TQ)
    assert tq <= tk and tk % tq == 0 and S % tk == 0
    n_q = S // tq
    n_k = S // tk
    rows = DIFF_VDIM + DIFF_ONES_ROWS
    return pl.pallas_call(
        functools.partial(_diff_attn_kernel, tq=tq, tk=tk),
        grid=(B, DIFF_HEADS, n_q),
        in_specs=[pl.BlockSpec((None, DIFF_VDIM, tq), lambda b, h, i: (h, 0, b * n_q + i)),
                  pl.BlockSpec((None, S, DIFF_VDIM), lambda b, h, i: (h, b, 0)),
                  pl.BlockSpec((n_k, rows, tk), lambda b, h, i: (b, h, 0)),
                  pl.BlockSpec(lam_rows.shape, lambda b, h, i: (0, 0)),
                  pl.BlockSpec((1, DIFF_VDIM), lambda b, h, i: (0, 0))],
        out_specs=pl.BlockSpec((tq, DIFF_VDIM), lambda b, h, i: (b * n_q + i, h)),
        out_shape=jax.ShapeDtypeStruct((B * S, MIX_DIM), BF16),
        scratch_shapes=[pltpu.VMEM((DIFF_VDIM, 2 * tq), BF16), pltpu.VMEM((1, 2 * tq), F32),
                        pltpu.VMEM((rows, 2 * tq), F32), pltpu.VMEM((tk, 2 * tq), F32),
                        pltpu.VMEM((tk, 2 * tq), F32)],
        compiler_params=_cparams(("parallel", "parallel", "arbitrary")),
        name="diff_attn",
    )(q, k, vt, lam_rows, g_subln)


_R_E0, _R_E1, _R_RANK0, _R_RANK1, _R_G0, _R_G1 = range(6)


def _router_kernel(x_ref, g_ref, wr_ref, h_ref, info_ref, cnt_ref, run_sc):
    @pl.when(pl.program_id(0) == 0)
    def _():
        run_sc[...] = jnp.zeros_like(run_sc)

    h = _rms(x_ref[...], g_ref[...])
    h_ref[...] = h
    logits = jnp.dot(h, wr_ref[...], precision=lax.Precision.HIGHEST, preferred_element_type=F32)
    tm = logits.shape[0]
    lane = lax.broadcasted_iota(jnp.int32, logits.shape, 1).astype(F32)
    neg = -jnp.inf
    lg = jnp.where(lane < N_EXPERTS, logits, neg)
    m1 = jnp.max(lg, axis=-1, keepdims=True)
    i1 = jnp.min(jnp.where(lg == m1, lane, float(LANES)), axis=-1, keepdims=True)
    lg2 = jnp.where(lane == i1, neg, lg)
    m2 = jnp.max(lg2, axis=-1, keepdims=True)
    i2 = jnp.min(jnp.where(lg2 == m2, lane, float(LANES)), axis=-1, keepdims=True)
    e = jnp.exp(m2 - m1)
    g0 = 1.0 / (1.0 + e)
    g1 = e / (1.0 + e)
    onehot = jnp.where((lane == i1) | (lane == i2), 1.0, 0.0)
    r = lax.broadcasted_iota(jnp.int32, (tm, tm), 0)
    c = lax.broadcasted_iota(jnp.int32, (tm, tm), 1)
    before = _dot(jnp.where(c < r, 1.0, 0.0).astype(BF16), onehot.astype(BF16)) + run_sc[0:1, :]
    rank0 = jnp.sum(jnp.where(lane == i1, before, 0.0), axis=-1, keepdims=True)
    rank1 = jnp.sum(jnp.where(lane == i2, before, 0.0), axis=-1, keepdims=True)
    run_sc[...] = run_sc[...] + jnp.sum(onehot, axis=0, keepdims=True)
    info = jnp.zeros_like(logits)
    for idx, val in ((_R_E0, i1), (_R_E1, i2), (_R_RANK0, rank0), (_R_RANK1, rank1),
                     (_R_G0, g0), (_R_G1, g1)):
        info = jnp.where(lane == idx, val, info)
    info_ref[...] = info
    cnt_ref[...] = run_sc[...]


def _router(x, g, wr):
    T = x.shape[0]
    tm = _tiles(T)
    return pl.pallas_call(
        _router_kernel,
        grid=(T // tm,),
        in_specs=[pl.BlockSpec((tm, D_MODEL), lambda i: (i, 0)),
                  pl.BlockSpec((1, D_MODEL), lambda i: (0, 0)),
                  pl.BlockSpec((D_MODEL, LANES), lambda i: (0, 0))],
        out_specs=[pl.BlockSpec((tm, D_MODEL), lambda i: (i, 0)),
                   pl.BlockSpec((tm, LANES), lambda i: (i, 0)),
                   pl.BlockSpec((SUBLANES, LANES), lambda i: (0, 0))],
        out_shape=[jax.ShapeDtypeStruct((T, D_MODEL), F32), jax.ShapeDtypeStruct((T, LANES), F32),
                   jax.ShapeDtypeStruct((SUBLANES, LANES), F32)],
        scratch_shapes=[pltpu.VMEM((SUBLANES, LANES), F32)],
        compiler_params=_cparams(("arbitrary",)),
        name="moe_router",
    )(x, g, wr)


def _row_copy(src, src_row, dst, dst_row, sem):
    return pltpu.make_async_copy(src.at[pl.ds(src_row, 1)], dst.at[pl.ds(dst_row, 1)], sem)


def _dispatch_kernel(d0_ref, d1_ref, h_ref, xs_in_ref, xs_ref, sem):
    del xs_in_ref
    tm = h_ref.shape[0]

    def issue(r, carry):
        _row_copy(h_ref, r, xs_ref, d0_ref[r], sem).start(priority=0)
        _row_copy(h_ref, r, xs_ref, d1_ref[r], sem).start(priority=1)
        return carry

    lax.fori_loop(0, tm, issue, 0, unroll=ROW_DMA_UNROLL)

    def drain(r, carry):
        _row_copy(h_ref, 0, xs_ref, 0, sem).wait()
        _row_copy(h_ref, 0, xs_ref, 0, sem).wait()
        return carry

    lax.fori_loop(0, tm, drain, 0, unroll=ROW_DMA_UNROLL)


def _dispatch(dest0, dest1, h, xs_zero):
    T = h.shape[0]
    tm = _tiles(T)
    smem = pl.BlockSpec((tm,), lambda i: (i,), memory_space=pltpu.SMEM)
    return pl.pallas_call(
        _dispatch_kernel,
        grid=(T // tm,),
        in_specs=[smem, smem, pl.BlockSpec((tm, D_MODEL), lambda i: (i, 0)),
                  pl.BlockSpec(memory_space=pl.ANY)],
        out_specs=pl.BlockSpec(memory_space=pl.ANY),
        out_shape=jax.ShapeDtypeStruct(xs_zero.shape, xs_zero.dtype),
        scratch_shapes=[pltpu.SemaphoreType.DMA],
        input_output_aliases={3: 0},
        compiler_params=_cparams(("arbitrary",)),
        name="moe_dispatch",
    )(dest0, dest1, h, xs_zero)


def _expert_kernel(bexp_ref, nused_ref, xs_ref, wg_ref, wu_ref, wd_ref, y_ref, xb_sc, acc_sc):
    del bexp_ref
    j = pl.program_id(0)
    f = pl.program_id(1)

    @pl.when(j < nused_ref[0])
    def _():
        @pl.when(f == 0)
        def _():
            xb_sc[...] = xs_ref[...].astype(BF16)
            acc_sc[...] = jnp.zeros_like(acc_sc)

        xb = xb_sc[...]
        act = (_silu(_dot(xb, wg_ref[0].astype(BF16)))
               * _dot(xb, wu_ref[0].astype(BF16))).astype(BF16)
        acc_sc[...] += _dot(act, wd_ref[0].astype(BF16))

        @pl.when(f == pl.num_programs(1) - 1)
        def _():
            y_ref[...] = acc_sc[...]

    @pl.when((j >= nused_ref[0]) & (f == 0))
    def _():
        y_ref[...] = jnp.zeros_like(y_ref)


def _experts(block_expert, n_used, xs, wg, wu, wd, rb):
    L = xs.shape[0]
    nb = L // rb
    tf = EXPERT_FFN_TILE
    nf = FFN_EXPERT // tf

    def blk(j, f, be, nu):
        return jnp.minimum(j, nu[0] - 1)

    def ftile(j, f, be, nu):
        return jnp.where(j < nu[0], f, nf - 1)

    return pl.pallas_call(
        _expert_kernel,
        grid_spec=pltpu.PrefetchScalarGridSpec(
            num_scalar_prefetch=2,
            grid=(nb, nf),
            in_specs=[
                pl.BlockSpec((rb, D_MODEL), lambda j, f, be, nu: (blk(j, f, be, nu), 0)),
                pl.BlockSpec((1, D_MODEL, tf),
                             lambda j, f, be, nu: (be[blk(j, f, be, nu)], 0, ftile(j, f, be, nu))),
                pl.BlockSpec((1, D_MODEL, tf),
                             lambda j, f, be, nu: (be[blk(j, f, be, nu)], 0, ftile(j, f, be, nu))),
                pl.BlockSpec((1, tf, D_MODEL),
                             lambda j, f, be, nu: (be[blk(j, f, be, nu)], ftile(j, f, be, nu), 0)),
            ],
            out_specs=pl.BlockSpec((rb, D_MODEL), lambda j, f, be, nu: (j, 0)),
            scratch_shapes=[pltpu.VMEM((rb, D_MODEL), BF16), pltpu.VMEM((rb, D_MODEL), F32)],
        ),
        out_shape=jax.ShapeDtypeStruct((L, D_MODEL), F32),
        compiler_params=_cparams(("arbitrary", "arbitrary")),
        name="moe_experts",
    )(block_expert, n_used, xs, wg, wu, wd)


def _combine_kernel(d0_ref, d1_ref, x_ref, info_ref, g_ref, y_ref, o_ref, b0_sc, b1_sc, sem):
    tm = x_ref.shape[0]

    def issue(r, carry):
        _row_copy(y_ref, d0_ref[r], b0_sc, r, sem).start(priority=0)
        _row_copy(y_ref, d1_ref[r], b1_sc, r, sem).start(priority=1)
        return carry

    lax.fori_loop(0, tm, issue, 0, unroll=ROW_DMA_UNROLL)

    def drain(r, carry):
        _row_copy(y_ref, 0, b0_sc, 0, sem).wait()
        _row_copy(y_ref, 0, b1_sc, 0, sem).wait()
        return carry

    lax.fori_loop(0, tm, drain, 0, unroll=ROW_DMA_UNROLL)
    info = info_ref[...]
    g0 = info[:, _R_G0:_R_G0 + 1]
    g1 = info[:, _R_G1:_R_G1 + 1]
    x = x_ref[...] + (b0_sc[...] * g0 + b1_sc[...] * g1)
    o_ref[...] = _rms(x, g_ref[...])


def _combine(dest0, dest1, x, info, g_final, y):
    T = x.shape[0]
    tm = _tiles(T)
    smem = pl.BlockSpec((tm,), lambda i: (i,), memory_space=pltpu.SMEM)
    return pl.pallas_call(
        _combine_kernel,
        grid=(T // tm,),
        in_specs=[smem, smem, pl.BlockSpec((tm, D_MODEL), lambda i: (i, 0)),
                  pl.BlockSpec((tm, LANES), lambda i: (i, 0)),
                  pl.BlockSpec((1, D_MODEL), lambda i: (0, 0)),
                  pl.BlockSpec(memory_space=pl.ANY)],
        out_specs=pl.BlockSpec((tm, D_MODEL), lambda i: (i, 0)),
        out_shape=jax.ShapeDtypeStruct((T, D_MODEL), F32),
        scratch_shapes=[pltpu.VMEM((tm, D_MODEL), F32), pltpu.VMEM((tm, D_MODEL), F32),
                        pltpu.SemaphoreType.DMA],
        compiler_params=_cparams(("arbitrary",)),
        name="moe_combine",
    )(dest0, dest1, x, info, g_final, y)


def _pad_heads_cols(w, heads, d, d_pad):
    k = w.shape[0]
    return jnp.pad(w.reshape(k, heads, d), ((0, 0), (0, 0), (0, d_pad - d))).reshape(k, heads * d_pad)


def _rope_tables(S):
    half = HEAD_DIM // 2
    inv = ROPE_THETA ** (-jnp.arange(half, dtype=F32) / half)
    ang = jnp.arange(S).astype(F32)[:, None] * inv[None, :]
    reps = LANES // half
    cos = jnp.tile(jnp.cos(ang), (1, reps))
    sign = jnp.tile(jnp.concatenate([-jnp.ones((half,), F32), jnp.ones((half,), F32)]),
                    LANES // HEAD_DIM)
    sin = jnp.tile(jnp.sin(ang), (1, reps)) * sign[None, :]
    return cos, sin


def _moe_block_rows(T):
    return EXPERT_ROWS if (2 * T) % EXPERT_ROWS == 0 else EXPERT_ROWS_SMALL


def kernel(x, mem, g_mix, g_ffn, w_mem_kv, w_out, w_in_a, w_gk_up, b_gk, g_gla, g_kv, w_kv, w_in_b, lambda_q1, lambda_k1, lambda_q2, lambda_k2, g_subln, w_dense_gate, w_dense_up, w_dense_down, w_router, w_exp_gate, w_exp_up, w_exp_down, g_final):
    B, S, D = x.shape
    T = B * S
    x0 = x.reshape(T, D)
    mem2 = mem.reshape(B * MEM_LEN, D)
    row = lambda v: v.reshape(1, -1)

    wa = w_in_a[0]
    s0, s1, s2, s3, s4 = (GLA_KDIM, 2 * GLA_KDIM, 2 * GLA_KDIM + MIX_DIM, 2 * GLA_KDIM + 2 * MIX_DIM,
                          2 * GLA_KDIM + 2 * MIX_DIM + GLA_GATE_RANK)
    wq = _pad_heads_cols(wa[:, :s0], GLA_HEADS, GLA_DK, GLA_DK_PAD).astype(BF16)
    wk = _pad_heads_cols(wa[:, s0:s1], GLA_HEADS, GLA_DK, GLA_DK_PAD).astype(BF16)
    wv = _pad_heads_cols(wa[:, s1:s2], GLA_HEADS, GLA_DV, GLA_DV_PAD).astype(BF16)
    wgo = _pad_heads_cols(wa[:, s2:s3], GLA_HEADS, GLA_DV, GLA_DV_PAD).astype(BF16)
    wgl = jnp.pad(wa[:, s3:s4], ((0, 0), (0, LANES - GLA_GATE_RANK))).astype(BF16)
    wqm = wa[:, s4:].astype(BF16)
    wup = jnp.pad(_pad_heads_cols(w_gk_up[0], GLA_HEADS, GLA_DK, GLA_DK_PAD),
                  ((0, LANES - GLA_GATE_RANK), (0, 0))).astype(BF16)
    bgk = _pad_heads_cols(row(b_gk[0]), GLA_HEADS, GLA_DK, GLA_DK_PAD)
    ggla = jnp.pad(row(g_gla[0]), ((0, 0), (0, GLA_DV_PAD - GLA_DV)))
    wo_a = jnp.pad(w_out[0][:MIX_DIM].reshape(GLA_HEADS, GLA_DV, D),
                   ((0, 0), (0, GLA_DV_PAD - GLA_DV), (0, 0))).reshape(GLA_HEADS * GLA_DV_PAD, D)

    q, k, v, go, qm, gk = _in_proj_a(x0, row(g_mix[0]), wq, wk, wv, wgo, wqm, wgl, wup, bgk)
    o = _gla(q, k, gk, v, go, ggla, B, S)
    kv_mem0 = _matmul(mem2, w_mem_kv[0].astype(BF16), BF16).reshape(B, MEM_LEN, 2 * MEM_DIM)
    m = _mem_attn(qm, kv_mem0, B, S)
    x1 = _out_proj(x0, o, m, wo_a.astype(BF16), w_out[0][MIX_DIM:].astype(BF16))
    x2 = _ffn(x1, row(g_ffn[0]), w_dense_gate[0].astype(BF16), w_dense_up[0].astype(BF16),
              w_dense_down[0].astype(BF16))

    cos, sin = _rope_tables(S)
    k_sh, vt_sh = _proj_rope(x2, row(g_kv), w_kv[:, :MIX_DIM].astype(BF16),
                             w_kv[:, MIX_DIM:].astype(BF16), cos, sin, S, 1.0, True)

    q1, qm1 = _proj_rope(x2, row(g_mix[1]), w_in_b[0][:, :MIX_DIM].astype(BF16),
                         w_in_b[0][:, MIX_DIM:].astype(BF16), cos, sin, S,
                         HEAD_DIM ** -0.5 * LOG2_E, False)
    lam_rows = jnp.pad(jnp.stack([lambda_q1[0], lambda_k1[0], lambda_q2[0], lambda_k2[0]]),
                       ((0, SUBLANES - 4), (0, LANES - HEAD_DIM)))
    o1 = _diff_attn(q1, k_sh, vt_sh, lam_rows, row(g_subln[0]), B, S)
    kv_mem1 = _matmul(mem2, w_mem_kv[1].astype(BF16), BF16).reshape(B, MEM_LEN, 2 * MEM_DIM)
    m1 = _mem_attn(qm1, kv_mem1, B, S)
    x3 = _out_proj(x2, o1, m1, w_out[1][:MIX_DIM].astype(BF16), w_out[1][MIX_DIM:].astype(BF16))

    wr = jnp.pad(w_router[0], ((0, 0), (0, LANES - N_EXPERTS)))
    h, info, cnt = _router(x3, row(g_ffn[1]), wr)

    rb = _moe_block_rows(T)
    counts = cnt[0, :N_EXPERTS].astype(jnp.int32)
    padded = (counts + rb - 1) // rb * rb
    pend = jnp.cumsum(padded)
    pstart = pend - padded
    e0 = info[:, _R_E0].astype(jnp.int32)
    e1 = info[:, _R_E1].astype(jnp.int32)
    eids = jnp.arange(N_EXPERTS, dtype=jnp.int32)
    start_of = lambda e: jnp.sum(jnp.where(e[:, None] == eids[None, :], pstart[None, :], 0), axis=1)
    dest0 = start_of(e0) + info[:, _R_RANK0].astype(jnp.int32)
    dest1 = start_of(e1) + info[:, _R_RANK1].astype(jnp.int32)
    nb = (2 * T) // rb + N_EXPERTS
    block_expert = jnp.clip(jnp.sum(pend[None, :] <= (jnp.arange(nb, dtype=jnp.int32) * rb)[:, None],
                                    axis=1), 0, N_EXPERTS - 1).astype(jnp.int32)
    n_used = (pend[-1:] // rb).astype(jnp.int32)

    xs = _dispatch(dest0, dest1, h, jnp.zeros((nb * rb, D), F32))
    ys = _experts(block_expert, n_used, xs, w_exp_gate[0], w_exp_up[0], w_exp_down[0], rb)
    out = _combine(dest0, dest1, x3, info, row(g_final), ys)
    return out.reshape(B, S, D)
```
